```python
import jax
import jax.numpy as jnp
from jax import lax
import numpy as np

D_MODEL = 2048
BATCH = 8
SEQ = 2048
DEPTH = 2
DEC_BATCH = 128
DEC_SEQ = 8
PAST_LEN = 8192
PAGE_SIZE = 128

HEAD_DIM = 128
ATTN_SCALE = HEAD_DIM ** -0.5
A_GROUPS = ((128, 1), (512, 4), (2048, 16))
A_NG = len(A_GROUPS)
A_HEADS = 4
A_QKV = A_NG * A_HEADS * HEAD_DIM
A_WIDTH = A_HEADS * HEAD_DIM
B_HEADS = 8
B_KV = 2
B_WINDOW = 128
B_WIDTH = B_HEADS * HEAD_DIM
C_CHUNK = 128
C_GROUPS = 8
C_GDIM = 128
C_WIDTH = C_GROUPS * C_GDIM
D_HEADS = 8
D_KV = 2
D_BLOCK = 256
D_TOPK = 3
D_WIDTH = D_HEADS * HEAD_DIM
MOBA_Q_BLOCK = 64

EVEN_SPLITS = (A_QKV, A_QKV, A_QKV, A_WIDTH, B_WIDTH, B_KV * HEAD_DIM, B_KV * HEAD_DIM, B_WIDTH)
ODD_SPLITS = (C_WIDTH, C_WIDTH, C_WIDTH, D_WIDTH, D_KV * HEAD_DIM, D_KV * HEAD_DIM, D_WIDTH)
IN_EVEN = sum(EVEN_SPLITS)
OUT_EVEN = A_WIDTH + B_WIDTH
IN_ODD = sum(ODD_SPLITS)
OUT_ODD = C_WIDTH + D_WIDTH
N_EVEN = (DEPTH + 1) // 2
N_ODD = DEPTH // 2
EPS = 1e-6
NEG = -1e30

kernel_name = 'hybrid_dilated_sink_gmlp_moba_step'


def rmsnorm(x, g):
    xf = x.astype(jnp.float32)
    y = xf * lax.rsqrt(jnp.mean(xf * xf, axis=-1, keepdims=True) + EPS)
    return (y * g.astype(jnp.float32)).astype(x.dtype)


def layernorm(x, g, b):
    xf = x.astype(jnp.float32)
    xc = xf - jnp.mean(xf, axis=-1, keepdims=True)
    y = xc * lax.rsqrt(jnp.mean(xc * xc, axis=-1, keepdims=True) + EPS)
    return (y * g.astype(jnp.float32) + b.astype(jnp.float32)).astype(x.dtype)


def split_cols(h, sizes):
    return jnp.split(h, np.cumsum(sizes)[:-1].tolist(), axis=-1)


def to_strided(t, d):
    n, l = t.shape[:2]
    rest = t.shape[2:]
    return t.reshape(n, l // d, d, *rest).swapaxes(1, 2).reshape(n * d, l // d, *rest)


def from_strided(t, d, n):
    ld = t.shape[1]
    rest = t.shape[2:]
    return t.reshape(n, d, ld, *rest).swapaxes(1, 2).reshape(n, d * ld, *rest)


def band_attn_stats(q, k, v, window):
    n, l = q.shape[:2]
    nb = -(-l // window)
    padl = nb * window - l
    q = jnp.pad(q, ((0, 0), (0, padl), (0, 0), (0, 0), (0, 0)))
    k = jnp.pad(k, ((0, 0), (0, padl), (0, 0), (0, 0)))
    v = jnp.pad(v, ((0, 0), (0, padl), (0, 0), (0, 0)))
    qb = q.reshape(n, nb, window, *q.shape[2:])
    kb = k.reshape(n, nb, window, *k.shape[2:])
    vb = v.reshape(n, nb, window, *v.shape[2:])
    prev = ((0, 0), (1, 0), (0, 0), (0, 0), (0, 0))
    kband = jnp.concatenate([jnp.pad(kb, prev)[:, :-1], kb], axis=2)
    vband = jnp.concatenate([jnp.pad(vb, prev)[:, :-1], vb], axis=2)
    qpos = jnp.arange(nb * window).reshape(nb, window)
    kpos = qpos[:, :1] - window + jnp.arange(2 * window)[None, :]
    dist = qpos[:, :, None] - kpos[:, None, :]
    vis = (dist >= 0) & (dist <= window) & (kpos[:, None, :] >= 0)
    logits = jnp.einsum('nbqhgd,nbkhd->nbqhgk', qb, kband).astype(jnp.float32)
    logits = jnp.where(vis[None, :, :, None, None, :], logits, NEG)
    m = logits.max(-1)
    p = jnp.exp(logits - m[..., None])
    s = p.sum(-1)
    o = jnp.einsum('nbqhgk,nbkhd->nbqhgd', p, vband.astype(jnp.float32))
    o = o.reshape(n, nb * window, *o.shape[3:])[:, :l]
    m = m.reshape(n, nb * window, *m.shape[3:])[:, :l]
    s = s.reshape(n, nb * window, *s.shape[3:])[:, :l]
    return o, m, s


def dilated_sample_stats(q, kv, lb, dil, n_steps):
    t = q.shape[1]
    idx = lb + jnp.arange(t)[:, None] - dil * jnp.arange(n_steps + 1)[None, :]
    valid = idx >= 0
    g = kv[:, jnp.maximum(idx, 0)]
    logits = jnp.einsum('nthd,ntjhd->nthj', q, g[:, :, :, 0]).astype(jnp.float32)
    logits = jnp.where(valid[None, :, None, :], logits, NEG)
    m = logits.max(-1)
    p = jnp.exp(logits - m[..., None])
    o = jnp.einsum('nthj,ntjhd->nthd', p, g[:, :, :, 1].astype(jnp.float32))
    return o, m, p.sum(-1)


def window_sample_stats(q, kv, lb, window):
    t = q.shape[1]
    dist = (lb + jnp.arange(t))[:, None] - jnp.arange(lb + t)[None, :]
    vis = (dist >= 0) & (dist <= window)
    logits = jnp.einsum('nthgd,nkhd->nthgk', q, kv[:, :, 0]).astype(jnp.float32)
    logits = jnp.where(vis[None, :, None, None, :], logits, NEG)
    m = logits.max(-1)
    p = jnp.exp(logits - m[..., None])
    o = jnp.einsum('nthgk,nkhd->nthgd', p, kv[:, :, 1].astype(jnp.float32))
    return o, m, p.sum(-1)


def combine_by_denominator(stats):
    mmax = jnp.max(jnp.stack([m for _, m, _ in stats]), axis=0)
    num = sum(o * jnp.exp(m - mmax)[..., None] for o, m, _ in stats)
    den = sum(s * jnp.exp(m - mmax) for _, m, s in stats)
    return num / den[..., None]


def apply_sink(o, m, s, sink):
    sink = sink.astype(jnp.float32)
    m2 = jnp.maximum(m, sink)
    a = jnp.exp(m - m2)
    den = s * a + jnp.exp(sink - m2)
    return o * (a / den)[..., None]


def two_part_attend(l_sel, v_sel, l_own, v_own):
    m = jnp.maximum(l_sel.max((-2, -1)), l_own.max(-1))
    p_sel = jnp.exp(l_sel - m[..., None, None])
    p_own = jnp.exp(l_own - m[..., None])
    den = p_sel.sum((-2, -1)) + p_own.sum(-1)
    o = (jnp.einsum('qhgnk,qhgnkd->qhgd', p_sel, v_sel.astype(jnp.float32))
         + jnp.einsum('qhgk,hkd->qhgd', p_own, v_own.astype(jnp.float32)))
    return o / den[..., None]


def own_attend(l_own, v_own):
    return jnp.einsum('qhgk,hkd->qhgd', jax.nn.softmax(l_own, axis=-1), v_own.astype(jnp.float32))


def chunk_mix(v, ws, bs):
    l = v.shape[2]
    w = jnp.tril(ws[:, :l, :l])
    return jnp.einsum('gij,ncjge->ncige', w, v) + bs[:, :l].T[None, None, :, :, None]


def moba_prompt(q, k, v):
    n, s = q.shape[:2]
    nb = -(-s // D_BLOCK)
    pad = ((0, 0), (0, nb * D_BLOCK - s), (0, 0), (0, 0))
    kb = jnp.pad(k, pad).reshape(n, nb, D_BLOCK, D_KV, HEAD_DIM).transpose(0, 1, 3, 2, 4)
    vb = jnp.pad(v, pad).reshape(n, nb, D_BLOCK, D_KV, HEAD_DIM).transpose(0, 1, 3, 2, 4)
    own = jnp.arange(s) // D_BLOCK
    n_sel = min(D_TOPK, nb - 1)
    sel, valid = None, None
    if n_sel > 0:
        kmean = kb.astype(jnp.float32).mean(3)
        scores = jnp.einsum('nshgd,nbhd->nshgb', q, kmean)
        past = jnp.arange(nb)[None, :] < own[:, None]
        scores = jnp.where(past[None, :, None, None, :], scores, NEG)
        sel = lax.top_k(scores, n_sel)[1]
        valid = jnp.arange(n_sel)[None, :] < own[:, None]
    n_qb = s // MOBA_Q_BLOCK
    kvh = jnp.arange(D_KV)[None, :, None, None]

    def one_block(idx):
        bi = idx // n_qb
        start = (idx % n_qb) * MOBA_Q_BLOCK
        qc = lax.dynamic_slice_in_dim(q[bi], start, MOBA_Q_BLOCK, 0)
        tpos = start + jnp.arange(MOBA_Q_BLOCK)
        ob = start // D_BLOCK
        k_own, v_own = kb[bi, ob], vb[bi, ob]
        l_own = jnp.einsum('qhgd,hkd->qhgk', qc, k_own).astype(jnp.float32)
        vis = ob * D_BLOCK + jnp.arange(D_BLOCK)[None, :] <= tpos[:, None]
        l_own = jnp.where(vis[:, None, None, :], l_own, NEG)
        if n_sel == 0:
            return own_attend(l_own, v_own)
        sc = lax.dynamic_slice_in_dim(sel[bi], start, MOBA_Q_BLOCK, 0)
        vc = lax.dynamic_slice_in_dim(valid, start, MOBA_Q_BLOCK, 0)
        kg = kb[bi][sc, kvh]
        vg = vb[bi][sc, kvh]
        l_sel = jnp.einsum('qhgd,qhgnkd->qhgnk', qc, kg).astype(jnp.float32)
        l_sel = jnp.where(vc[:, None, None, :, None], l_sel, NEG)
        return two_part_attend(l_sel, vg, l_own, v_own)

    out = lax.map(one_block, jnp.arange(n * n_qb))
    return out.reshape(n, s, D_WIDTH)


def moba_sample(q, k, v, pool, page_table):
    n, t = q.shape[:2]
    ppb = D_BLOCK // PAGE_SIZE
    n_blk = PAST_LEN // D_BLOCK
    own_p0 = n_blk * ppb
    n_own_pages = PAST_LEN // PAGE_SIZE - own_p0
    n_sel = min(D_TOPK, n_blk)
    own_kv = pool[page_table[:, own_p0:own_p0 + n_own_pages]]
    own_kv = own_kv.transpose(0, 2, 3, 1, 4, 5).reshape(n, 2, D_KV, n_own_pages * PAGE_SIZE, HEAD_DIM)
    new_kv = jnp.stack([k, v], axis=1).transpose(0, 1, 3, 2, 4)
    own_kv = jnp.concatenate([own_kv, new_kv], axis=3)
    lpo = n_own_pages * PAGE_SIZE
    vis = jnp.arange(lpo + t)[None, :] <= lpo + jnp.arange(t)[:, None]
    if n_sel > 0:
        kpast = pool[page_table[:, :own_p0], 0]
        kmean = kpast.astype(jnp.float32).reshape(n, n_blk, ppb, D_KV, PAGE_SIZE, HEAD_DIM).mean((2, 4))
        sel = lax.top_k(jnp.einsum('nthgd,nbhd->nthgb', q, kmean), n_sel)[1]
    else:
        sel = jnp.zeros((n, t, D_KV, D_HEADS // D_KV, 0), jnp.int32)
    kvh = jnp.arange(D_KV)[None, :, None, None, None]

    def one_seq(args):
        qs, kvo, sel_s, pt = args
        l_own = jnp.einsum('thgd,hkd->thgk', qs, kvo[0]).astype(jnp.float32)
        l_own = jnp.where(vis[:, None, None, :], l_own, NEG)
        if n_sel == 0:
            return own_attend(l_own, kvo[1])
        pages = pt[sel_s[..., None] * ppb + jnp.arange(ppb)]
        kg = pool[pages, 0, kvh].reshape(*sel_s.shape, D_BLOCK, HEAD_DIM)
        vg = pool[pages, 1, kvh].reshape(*sel_s.shape, D_BLOCK, HEAD_DIM)
        l_sel = jnp.einsum('thgd,thgnkd->thgnk', qs, kg).astype(jnp.float32)
        return two_part_attend(l_sel, vg, l_own, kvo[1])

    out = lax.map(one_seq, (q, own_kv, sel, page_table))
    return out.reshape(n, t, D_WIDTH)


def mixer_output(x, branches, w_out, g_post):
    n, t = x.shape[:2]
    mix = jnp.concatenate([o.reshape(n, t, -1).astype(x.dtype) * jax.nn.silu(g) for o, g in branches], axis=-1)
    return x + rmsnorm(mix @ w_out, g_post)


def even_project(x, w_in, g_pre):
    n, t = x.shape[:2]
    aq, ak, av, ag, bq, bk, bv, bg = split_cols(rmsnorm(x, g_pre) @ w_in, EVEN_SPLITS)
    sa = (n, t, A_NG, A_HEADS, HEAD_DIM)
    sq = (n, t, B_KV, B_HEADS // B_KV, HEAD_DIM)
    sk = (n, t, B_KV, HEAD_DIM)
    return (aq.reshape(sa) * ATTN_SCALE, ak.reshape(sa), av.reshape(sa), ag,
            bq.reshape(sq) * ATTN_SCALE, bk.reshape(sk), bv.reshape(sk), bg)


def odd_project(x, w_in, g_pre, ln_g, ln_b):
    n, t = x.shape[:2]
    cu, cv, cg, dq, dk, dv, dg = split_cols(rmsnorm(x, g_pre) @ w_in, ODD_SPLITS)
    sk = (n, t, D_KV, HEAD_DIM)
    return (cu, layernorm(cv, ln_g, ln_b), cg,
            dq.reshape(n, t, D_KV, D_HEADS // D_KV, HEAD_DIM) * ATTN_SCALE, dk.reshape(sk), dv.reshape(sk), dg)


def even_prompt(x, w_in, w_out, g_pre, g_post, sink):
    aq, ak, av, ag, bq, bk, bv, bg = even_project(x, w_in, g_pre)
    n, s = x.shape[:2]
    stats, rows = [], []
    for gi, (win, dil) in enumerate(A_GROUPS):
        o, m, d = band_attn_stats(to_strided(aq[:, :, gi, :, None], dil), to_strided(ak[:, :, gi], dil),
                                  to_strided(av[:, :, gi], dil), win // dil)
        stats.append((from_strided(o[:, :, :, 0], dil, n), from_strided(m[..., 0], dil, n),
                      from_strided(d[..., 0], dil, n)))
        rows.append(jnp.stack([ak[:, :, gi], av[:, :, gi]], axis=2)[:, s - min(win, s):])
    a_out = combine_by_denominator(stats)
    o, m, d = band_attn_stats(bq, bk, bv, B_WINDOW)
    b_out = apply_sink(o, m, d, sink.reshape(B_KV, -1))
    rows.append(jnp.stack([bk, bv], axis=2)[:, s - min(B_WINDOW, s):])
    return mixer_output(x, ((a_out, ag), (b_out, bg)), w_out, g_post), rows


def even_sample(x, bufs_a, buf_b, w_in, w_out, g_pre, g_post, sink):
    aq, ak, av, ag, bq, bk, bv, bg = even_project(x, w_in, g_pre)
    t = x.shape[1]
    stats, new_bufs = [], []
    for gi, (win, dil) in enumerate(A_GROUPS):
        kv = jnp.concatenate([bufs_a[gi], jnp.stack([ak[:, :, gi], av[:, :, gi]], axis=2)], axis=1)
        stats.append(dilated_sample_stats(aq[:, :, gi], kv, bufs_a[gi].shape[1], dil, win // dil))
        new_bufs.append(kv[:, t:])
    a_out = combine_by_denominator(stats)
    kvb = jnp.concatenate([buf_b, jnp.stack([bk, bv], axis=2)], axis=1)
    o, m, d = window_sample_stats(bq, kvb, buf_b.shape[1], B_WINDOW)
    b_out = apply_sink(o, m, d, sink.reshape(B_KV, -1))
    new_bufs.append(kvb[:, t:])
    return mixer_output(x, ((a_out, ag), (b_out, bg)), w_out, g_post), new_bufs


def odd_prompt(x, w_in, w_out, g_pre, g_post, ln_g, ln_b, ws, bs):
    cu, cv, cg, dq, dk, dv, dg = odd_project(x, w_in, g_pre, ln_g, ln_b)
    n, s = x.shape[:2]
    c_out = cu * chunk_mix(cv.reshape(n, s // C_CHUNK, C_CHUNK, C_GROUPS, C_GDIM), ws, bs).reshape(n, s, C_WIDTH)
    d_out = moba_prompt(dq, dk, dv)
    y = mixer_output(x, ((c_out, cg), (d_out, dg)), w_out, g_post)
    kv_pages = jnp.stack([dk, dv], axis=2).reshape(n, s // PAGE_SIZE, PAGE_SIZE, 2, D_KV, HEAD_DIM).transpose(0, 1, 3, 4, 2, 5)
    return y, kv_pages


def odd_sample(x, pool, page_table, w_in, w_out, g_pre, g_post, ln_g, ln_b, ws, bs):
    cu, cv, cg, dq, dk, dv, dg = odd_project(x, w_in, g_pre, ln_g, ln_b)
    n, t = x.shape[:2]
    c_out = cu * chunk_mix(cv.reshape(n, 1, t, C_GROUPS, C_GDIM), ws, bs).reshape(n, t, C_WIDTH)
    d_out = moba_sample(dq, dk, dv, pool, page_table)
    y = mixer_output(x, ((c_out, cg), (d_out, dg)), w_out, g_post)
    kv_rows = jnp.stack([dk, dv], axis=1).transpose(0, 1, 3, 2, 4)
    return y, kv_rows, cv


def setup_inputs(seed: int = 0) -> dict:
    key = jax.random.key(seed)
    ks = jax.random.split(key, 22)
    f32 = jnp.float32
    nrm = jax.random.normal
    n_pages = PAST_LEN // PAGE_SIZE
    n_pool = (DEC_BATCH * n_pages * 5) // 4

    def gain(k, shape):
        return 1.0 + 0.05 * nrm(k, shape, f32)

    def abuf(k, win, heads):
        return nrm(k, (N_EVEN, DEC_BATCH, min(win, PAST_LEN), 2, heads, HEAD_DIM), f32)

    return {
        'x_prompt': nrm(ks[0], (BATCH, SEQ, D_MODEL), f32),
        'x_sample': nrm(ks[1], (DEC_BATCH, DEC_SEQ, D_MODEL), f32),
        'cache_a1': abuf(ks[2], A_GROUPS[0][0], A_HEADS),
        'cache_a2': abuf(ks[3], A_GROUPS[1][0], A_HEADS),
        'cache_a3': abuf(ks[4], A_GROUPS[2][0], A_HEADS),
        'cache_b': abuf(ks[5], B_WINDOW, B_KV),
        'cache_d': nrm(ks[6], (N_ODD, n_pool, 2, D_KV, PAGE_SIZE, HEAD_DIM), f32),
        'page_table': jax.random.permutation(ks[7], n_pool)[:DEC_BATCH * n_pages].reshape(DEC_BATCH, n_pages).astype(jnp.int32),
        'norm_pre_even': gain(ks[8], (N_EVEN, D_MODEL)),
        'norm_post_even': gain(ks[9], (N_EVEN, D_MODEL)),
        'w_in_even': nrm(ks[10], (N_EVEN, D_MODEL, IN_EVEN), f32) * D_MODEL ** -0.5,
        'w_out_even': nrm(ks[11], (N_EVEN, OUT_EVEN, D_MODEL), f32) * OUT_EVEN ** -0.5,
        'sink_b': 0.5 * nrm(ks[12], (N_EVEN, B_HEADS), f32),
        'norm_pre_odd': gain(ks[13], (N_ODD, D_MODEL)),
        'norm_post_odd': gain(ks[14], (N_ODD, D_MODEL)),
        'w_in_odd': nrm(ks[15], (N_ODD, D_MODEL, IN_ODD), f32) * D_MODEL ** -0.5,
        'w_out_odd': nrm(ks[16], (N_ODD, OUT_ODD, D_MODEL), f32) * OUT_ODD ** -0.5,
        'c_ln_g': gain(ks[17], (N_ODD, C_WIDTH)),
        'c_ln_b': 0.02 * nrm(ks[18], (N_ODD, C_WIDTH), f32),
        'c_ws': nrm(ks[19], (N_ODD, C_GROUPS, C_CHUNK, C_CHUNK), f32) * C_CHUNK ** -0.5,
        'c_bs': 1.0 + 0.1 * nrm(ks[20], (N_ODD, C_GROUPS, C_CHUNK), f32),
    }


def reference(x_prompt, x_sample, cache_a1, cache_a2, cache_a3, cache_b, cache_d, page_table,
              norm_pre_even, norm_post_even, w_in_even, w_out_even, sink_b,
              norm_pre_odd, norm_post_odd, w_in_odd, w_out_odd, c_ln_g, c_ln_b, c_ws, c_bs):
    yp, ys = x_prompt, x_sample
    a_p, a_s = [[], [], []], [[], [], []]
    b_p, b_s, d_p, d_s, c_s = [], [], [], [], []
    for layer in range(DEPTH):
        li = layer // 2
        if layer % 2 == 0:
            yp, rows = even_prompt(yp, w_in_even[li], w_out_even[li], norm_pre_even[li], norm_post_even[li], sink_b[li])
            ys, bufs = even_sample(ys, (cache_a1[li], cache_a2[li], cache_a3[li]), cache_b[li], w_in_even[li],
                                   w_out_even[li], norm_pre_even[li], norm_post_even[li], sink_b[li])
            for gi in range(A_NG):
                a_p[gi].append(rows[gi])
                a_s[gi].append(bufs[gi])
            b_p.append(rows[A_NG])
            b_s.append(bufs[A_NG])
        else:
            yp, kvp = odd_prompt(yp, w_in_odd[li], w_out_odd[li], norm_pre_odd[li], norm_post_odd[li],
                                 c_ln_g[li], c_ln_b[li], c_ws[li], c_bs[li])
            ys, kvs, vrows = odd_sample(ys, cache_d[li], page_table, w_in_odd[li], w_out_odd[li], norm_pre_odd[li],
                                        norm_post_odd[li], c_ln_g[li], c_ln_b[li], c_ws[li], c_bs[li])
            d_p.append(kvp)
            d_s.append(kvs)
            c_s.append(vrows)
    return (yp, ys, jnp.stack(a_p[0]), jnp.stack(a_s[0]), jnp.stack(a_p[1]), jnp.stack(a_s[1]),
            jnp.stack(a_p[2]), jnp.stack(a_s[2]), jnp.stack(b_p), jnp.stack(b_s),
            jnp.stack(d_p), jnp.stack(d_s), jnp.stack(c_s))
```

```python
import functools

import jax
import jax.numpy as jnp
from jax import lax
from jax.experimental import pallas as pl
from jax.experimental.pallas import tpu as pltpu

F32 = jnp.float32
BF16 = jnp.bfloat16

D_MODEL = 2048
DEPTH = 2
DEC_SEQ = 8
PAST_LEN = 8192
PAGE_SIZE = 128
HEAD_DIM = 128
ATTN_SCALE = HEAD_DIM ** -0.5
A_GROUPS = ((128, 1), (512, 4), (2048, 16))
A_NG = len(A_GROUPS)
A_HEADS = 4
A_QKV = A_NG * A_HEADS * HEAD_DIM
A_WIDTH = A_HEADS * HEAD_DIM
B_HEADS = 8
B_KV = 2
B_WINDOW = 128
B_WIDTH = B_HEADS * HEAD_DIM
C_CHUNK = 128
C_GROUPS = 8
C_GDIM = 128
C_WIDTH = C_GROUPS * C_GDIM
D_HEADS = 8
D_KV = 2
D_BLOCK = 256
D_TOPK = 3
D_WIDTH = D_HEADS * HEAD_DIM
EPS = 1e-6
NEG = -1e30
BELOW_NEG = -3e38

E_AQ, E_AK, E_AV = 0, A_QKV, 2 * A_QKV
E_AG = 3 * A_QKV
E_BQ = E_AG + A_WIDTH
E_BK = E_BQ + B_WIDTH
E_BV = E_BK + B_KV * HEAD_DIM
E_BG = E_BV + B_KV * HEAD_DIM
IN_EVEN = E_BG + B_WIDTH
O_CU, O_CV, O_CG = 0, C_WIDTH, 2 * C_WIDTH
O_DQ = 3 * C_WIDTH
O_DK = O_DQ + D_WIDTH
O_DV = O_DK + D_KV * HEAD_DIM
O_DG = O_DV + D_KV * HEAD_DIM
IN_ODD = O_DG + D_WIDTH

BAND = 128
GATE_W = 512
VMEM_LIMIT = 48 * 1024 * 1024


def _cparams(*sem):
    return pltpu.CompilerParams(dimension_semantics=sem, vmem_limit_bytes=VMEM_LIMIT)


def _dot_nt(a, b, precision=None):
    return lax.dot_general(a, b, (((1,), (1,)), ((), ())), preferred_element_type=F32, precision=precision)


def _dot(a, b):
    return jnp.dot(a, b, preferred_element_type=F32)


def _iota(shape, dim):
    return lax.broadcasted_iota(jnp.int32, shape, dim)


def _div(x, d):
    assert d & (d - 1) == 0
    return x >> (d.bit_length() - 1)


def _mod(x, d):
    assert d & (d - 1) == 0
    return x & (d - 1)


def _norm_proj_kernel(x_ref, g_ref, w_ref, o_ref, xn_ref):
    @pl.when(pl.program_id(1) == 0)
    def _():
        x = x_ref[...]
        r = lax.rsqrt(jnp.mean(x * x, axis=-1, keepdims=True) + EPS)
        xn_ref[...] = (x * r * g_ref[...]).astype(BF16)

    o_ref[...] = _dot(xn_ref[...], w_ref[...])


def norm_proj(x, g, w, tm, tn=512):
    m, d = x.shape
    n = w.shape[1]
    return pl.pallas_call(
        _norm_proj_kernel,
        grid=(m // tm, n // tn),
        in_specs=[pl.BlockSpec((tm, d), lambda i, j: (i, 0)),
                  pl.BlockSpec((1, d), lambda i, j: (0, 0)),
                  pl.BlockSpec((d, tn), lambda i, j: (0, j))],
        out_specs=pl.BlockSpec((tm, tn), lambda i, j: (i, j)),
        out_shape=jax.ShapeDtypeStruct((m, n), F32),
        scratch_shapes=[pltpu.VMEM((tm, d), BF16)],
        compiler_params=_cparams("parallel", "arbitrary"),
        name="norm_proj",
    )(x, g.reshape(1, d), w)


def _band_block(q, kc, vc, kp, vp, prev_floor=0):
    r = q.shape[0]
    i = _mod(_iota((r, BAND), 0), BAND)
    j = _iota((r, BAND), 1)
    sc = jnp.where(j <= i, _dot_nt(q, kc), NEG)
    m = sc.max(-1, keepdims=True)
    if kp is not None:
        sp = jnp.where(j >= i + prev_floor, _dot_nt(q, kp), NEG)
        m = jnp.maximum(m, sp.max(-1, keepdims=True))
    pc = jnp.exp(sc - m)
    l = pc.sum(-1, keepdims=True)
    o = _dot(pc.astype(BF16), vc)
    if kp is not None:
        pp = jnp.exp(sp - m)
        l = l + pp.sum(-1, keepdims=True)
        o = o + _dot(pp.astype(BF16), vp)
    return o, m, l


def _rows(ref, start, stride):
    if stride == 1:
        return ref[pl.ds(start, BAND), :]
    return ref[pl.ds(start, BAND, stride=stride), :]


def _attn_a_prompt_kernel(q1, k1, v1, q2, k2, v2, q3, k3, v3, o_ref, o1_s, m1_s, l1_s, o2_s, m2_s, l2_s):
    seq = q1.shape[0]
    wide = (BAND, HEAD_DIM)

    def load(qr, kr, vr, start, stride, prev_start):
        q = (_rows(qr, start, stride) * ATTN_SCALE).astype(BF16)
        kc = _rows(kr, start, stride).astype(BF16)
        vc = _rows(vr, start, stride).astype(BF16)
        if prev_start is None:
            return q, kc, vc, None, None
        return q, kc, vc, _rows(kr, prev_start, stride).astype(BF16), _rows(vr, prev_start, stride).astype(BF16)

    def g1_body(b, carry):
        start = pl.multiple_of(b * BAND, BAND)
        prev = pl.multiple_of(jnp.maximum(b - 1, 0) * BAND, BAND)
        o, m, l = _band_block(*load(q1, k1, v1, start, 1, prev), prev_floor=jnp.where(b > 0, 0, BAND))
        o1_s[pl.ds(start, BAND), :] = o
        m1_s[pl.ds(start, BAND), :] = jnp.broadcast_to(m, wide)
        l1_s[pl.ds(start, BAND), :] = jnp.broadcast_to(l, wide)
        return carry

    lax.fori_loop(0, seq // BAND, g1_body, 0)

    d2 = A_GROUPS[1][1]
    sub2 = seq // d2
    for r in range(d2):
        for b in range(sub2 // BAND):
            start = r + b * BAND * d2
            prev = None if b == 0 else start - BAND * d2
            o, m, l = _band_block(*load(q2, k2, v2, start, d2, prev))
            o2_s[r, b * BAND:(b + 1) * BAND, :] = o
            m2_s[r, b * BAND:(b + 1) * BAND, :] = jnp.broadcast_to(m, wide)
            l2_s[r, b * BAND:(b + 1) * BAND, :] = jnp.broadcast_to(l, wide)

    d3 = A_GROUPS[2][1]
    ratio = d3 // d2
    for r in range(d3):
        o3, m3, l3 = _band_block(*load(q3, k3, v3, r, d3, None))
        m3 = jnp.broadcast_to(m3, wide)
        l3 = jnp.broadcast_to(l3, wide)
        o1 = o1_s[pl.ds(r, BAND, stride=d3), :]
        m1 = m1_s[pl.ds(r, BAND, stride=d3), :]
        l1 = l1_s[pl.ds(r, BAND, stride=d3), :]
        r2, off2 = r % d2, r // d2
        o2 = o2_s[r2, pl.ds(off2, BAND, stride=ratio), :]
        m2 = m2_s[r2, pl.ds(off2, BAND, stride=ratio), :]
        l2 = l2_s[r2, pl.ds(off2, BAND, stride=ratio), :]
        mmax = jnp.maximum(jnp.maximum(m1, m2), m3)
        e1, e2, e3 = jnp.exp(m1 - mmax), jnp.exp(m2 - mmax), jnp.exp(m3 - mmax)
        num = o1 * e1 + o2 * e2 + o3 * e3
        den = l1 * e1 + l2 * e2 + l3 * e3
        o_ref[pl.ds(r, BAND, stride=d3), :] = num / den


def attn_a_prompt(h3):
    n, seq, _ = h3.shape
    assert seq == A_GROUPS[2][0] and all(w // d == BAND for w, d in A_GROUPS)

    def spec(off, gi):
        blk = off // HEAD_DIM + gi * A_HEADS
        return pl.BlockSpec((None, seq, HEAD_DIM), lambda b, hh: (b, 0, blk + hh))

    in_specs = []
    for gi in range(A_NG):
        in_specs += [spec(E_AQ, gi), spec(E_AK, gi), spec(E_AV, gi)]
    d2 = A_GROUPS[1][1]
    stat = pltpu.VMEM((seq, HEAD_DIM), F32)
    stat2 = pltpu.VMEM((d2, seq // d2, HEAD_DIM), F32)
    return pl.pallas_call(
        _attn_a_prompt_kernel,
        grid=(n, A_HEADS),
        in_specs=in_specs,
        out_specs=pl.BlockSpec((None, seq, HEAD_DIM), lambda b, hh: (b, 0, hh)),
        out_shape=jax.ShapeDtypeStruct((n, seq, A_WIDTH), F32),
        scratch_shapes=[stat, stat, stat, stat2, stat2, stat2],
        compiler_params=_cparams("parallel", "parallel"),
        name="attn_a_prompt",
    )(*([h3] * 9))


def _attn_b_prompt_kernel(sink_ref, q_ref, k_ref, v_ref, o_ref):
    seq = q_ref.shape[0]
    g = B_HEADS // B_KV
    kvh = pl.program_id(1)
    sink = jnp.concatenate([jnp.full((BAND, 1), sink_ref[kvh * g + gi], F32) for gi in range(g)], axis=0)

    def body(b, carry):
        start = pl.multiple_of(b * BAND, BAND)
        prev = pl.multiple_of(jnp.maximum(b - 1, 0) * BAND, BAND)
        qb = q_ref[pl.ds(start, BAND), :]
        q = jnp.concatenate([qb[:, gi * HEAD_DIM:(gi + 1) * HEAD_DIM] for gi in range(g)], axis=0)
        q = (q * ATTN_SCALE).astype(BF16)
        kc = k_ref[pl.ds(start, BAND), :].astype(BF16)
        vc = v_ref[pl.ds(start, BAND), :].astype(BF16)
        kp = k_ref[pl.ds(prev, BAND), :].astype(BF16)
        vp = v_ref[pl.ds(prev, BAND), :].astype(BF16)
        o, m, l = _band_block(q, kc, vc, kp, vp, prev_floor=jnp.where(b > 0, 0, BAND))
        m2 = jnp.maximum(m, sink)
        a = jnp.exp(m - m2)
        out = o * (a / (l * a + jnp.exp(sink - m2)))
        for gi in range(g):
            o_ref[pl.ds(start, BAND), gi * HEAD_DIM:(gi + 1) * HEAD_DIM] = out[gi * BAND:(gi + 1) * BAND]
        return carry

    lax.fori_loop(0, seq // BAND, body, 0)


def attn_b_prompt(h3, sink):
    n, seq, _ = h3.shape
    gw = (B_HEADS // B_KV) * HEAD_DIM
    return pl.pallas_call(
        _attn_b_prompt_kernel,
        grid=(n, B_KV),
        in_specs=[pl.BlockSpec(memory_space=pltpu.SMEM),
                  pl.BlockSpec((None, seq, gw), lambda b, hh: (b, 0, E_BQ // gw + hh)),
                  pl.BlockSpec((None, seq, HEAD_DIM), lambda b, hh: (b, 0, E_BK // HEAD_DIM + hh)),
                  pl.BlockSpec((None, seq, HEAD_DIM), lambda b, hh: (b, 0, E_BV // HEAD_DIM + hh))],
        out_specs=pl.BlockSpec((None, seq, gw), lambda b, hh: (b, 0, hh)),
        out_shape=jax.ShapeDtypeStruct((n, seq, B_WIDTH), F32),
        compiler_params=_cparams("parallel", "parallel"),
        name="attn_b_prompt",
    )(sink, h3, h3, h3)


def _kv_rows_kernel(k_ref, v_ref, o_ref, *, heads):
    for kv, src in enumerate((k_ref, v_ref)):
        for hh in range(heads):
            o_ref[pl.ds(kv * heads + hh, BAND, stride=2 * heads), :] = src[:, hh * HEAD_DIM:(hh + 1) * HEAD_DIM]


def kv_rows(h3, k_off, v_off, heads, window):
    n, seq, _ = h3.shape
    w = min(window, seq)
    hw = heads * HEAD_DIM
    first = (seq - w) // BAND
    rows = BAND * 2 * heads
    out = pl.pallas_call(
        functools.partial(_kv_rows_kernel, heads=heads),
        grid=(n, w // BAND),
        in_specs=[pl.BlockSpec((None, BAND, hw), lambda b, c: (b, first + c, k_off // hw)),
                  pl.BlockSpec((None, BAND, hw), lambda b, c: (b, first + c, v_off // hw))],
        out_specs=pl.BlockSpec((None, rows, HEAD_DIM), lambda b, c: (b, c, 0)),
        out_shape=jax.ShapeDtypeStruct((n, w * 2 * heads, HEAD_DIM), F32),
        compiler_params=_cparams("parallel", "parallel"),
        name="kv_rows",
    )(h3, h3)
    return out.reshape(n, w, 2, heads, HEAD_DIM)


DEC_CHUNK = 2048


def _decode_kernel(*refs, heads, q_heads, dil, use_sink):
    if use_sink:
        sink_ref, q_ref, kn_ref, vn_ref, nf_ref, c_ref, nc_ref, out_ref, bias_s, biasn_s, l_s = refs
    else:
        q_ref, kn_ref, vn_ref, nf_ref, c_ref, nc_ref, out_ref, m_ref, l_ref, bias_s, biasn_s, l_s = refs
    slots = 2 * heads
    rows = c_ref.shape[0]
    lb = rows // slots
    nq = q_ref.shape[0]
    t_new = kn_ref.shape[0] // heads
    per_kv = q_heads // heads
    chunk = min(DEC_CHUNK, rows)

    @pl.when(pl.program_id(0) == 0)
    def _():
        row = _iota((nq, rows), 0)
        col = _iota((nq, rows), 1)
        t = _div(row, q_heads)
        kvh = _div(_mod(row, q_heads), per_kv)
        pos = _div(col, slots)
        ok = (_mod(col, slots) == kvh) & (pos >= t) & (_mod(lb + t - pos, dil) == 0)
        bias_s[...] = jnp.where(ok, 0.0, NEG)
        rown = _iota((nq, t_new * heads), 0)
        coln = _iota((nq, t_new * heads), 1)
        tn = _div(rown, q_heads)
        kvhn = _div(_mod(rown, q_heads), per_kv)
        tk = _div(coln, heads)
        okn = (_mod(coln, heads) == kvhn) & (tk <= tn) & (_mod(tn - tk, dil) == 0)
        biasn_s[...] = jnp.where(okn, 0.0, NEG)

    q = (q_ref[...] * ATTN_SCALE).astype(BF16)
    sn = _dot_nt(q, kn_ref[...].astype(BF16)) + biasn_s[...]
    m = sn.max(-1, keepdims=True)
    for c in range(rows // chunk):
        sl = slice(c * chunk, (c + 1) * chunk)
        s = _dot_nt(q, c_ref[sl, :].astype(BF16)) + bias_s[:, sl]
        l_s[:, sl] = s
        m = jnp.maximum(m, s.max(-1, keepdims=True))
    pn = jnp.exp(sn - m)
    den = pn.sum(-1, keepdims=True)
    o = _dot(pn.astype(BF16), vn_ref[...].astype(BF16))
    for c in range(rows // chunk):
        sl = slice(c * chunk, (c + 1) * chunk)
        p = jnp.exp(l_s[:, sl] - m)
        den = den + p.sum(-1, keepdims=True)
        o = o + _dot(pltpu.roll(p, heads, axis=1).astype(BF16), c_ref[sl, :].astype(BF16))

    if use_sink:
        sink = sink_ref[...]
        m2 = jnp.maximum(m, sink)
        a = jnp.exp(m - m2)
        out_ref[...] = o * (a / (den * a + jnp.exp(sink - m2)))
    else:
        out_ref[...] = o
        m_ref[...] = jnp.broadcast_to(m, m_ref.shape)
        l_ref[...] = jnp.broadcast_to(den, l_ref.shape)

    fresh = nf_ref.shape[0]
    nc_ref[pl.ds(0, rows - fresh), :] = c_ref[pl.ds(fresh, rows - fresh), :]
    nc_ref[pl.ds(rows - fresh, fresh), :] = nf_ref[...]


def decode_attn(cache, q, k_new, v_new, heads, q_heads, dil, sink=None):
    n, lb = cache.shape[:2]
    slots = 2 * heads
    rows = lb * slots
    nq = q.shape[1]
    tk = k_new.shape[1]
    t_new = tk // heads
    flat = cache.reshape(n, rows, HEAD_DIM)
    new_flat = jnp.stack([k_new.reshape(n, t_new, heads, HEAD_DIM), v_new.reshape(n, t_new, heads, HEAD_DIM)],
                         axis=2).reshape(n, t_new * slots, HEAD_DIM)
    use_sink = sink is not None

    def per_seq(r):
        return pl.BlockSpec((None, r, HEAD_DIM), lambda b: (b, 0, 0))

    in_specs = [per_seq(nq), per_seq(tk), per_seq(tk), per_seq(t_new * slots), per_seq(rows)]
    args = [q, k_new, v_new, new_flat, flat]
    stat = jax.ShapeDtypeStruct((n, nq, HEAD_DIM), F32)
    out_shape = [jax.ShapeDtypeStruct((n, rows, HEAD_DIM), F32), stat]
    out_specs = [per_seq(rows), per_seq(nq)]
    if use_sink:
        in_specs = [pl.BlockSpec((nq, HEAD_DIM), lambda b: (0, 0))] + in_specs
        args = [sink] + args
    else:
        out_shape += [stat, stat]
        out_specs += [per_seq(nq), per_seq(nq)]
    outs = pl.pallas_call(
        functools.partial(_decode_kernel, heads=heads, q_heads=q_heads, dil=dil, use_sink=use_sink),
        grid=(n,),
        in_specs=in_specs,
        out_specs=out_specs,
        out_shape=out_shape,
        scratch_shapes=[pltpu.VMEM((nq, rows), F32), pltpu.VMEM((nq, tk), F32), pltpu.VMEM((nq, rows), F32)],
        compiler_params=_cparams("arbitrary"),
        name="decode_attn",
    )(*args)
    return (outs[0].reshape(cache.shape),) + tuple(outs[1:])


def _merge_kernel(*refs):
    o_ref = refs[-1]
    stats = [refs[3 * g:3 * g + 3] for g in range(A_NG)]
    mmax = functools.reduce(jnp.maximum, [m[...] for _, m, _ in stats])
    num = 0.0
    den = 0.0
    for o, m, l in stats:
        e = jnp.exp(m[...] - mmax)
        num = num + o[...] * e
        den = den + l[...] * e
    o_ref[...] = num / den


def merge_groups(stats):
    flat = [a.reshape(-1, HEAD_DIM) for st in stats for a in st]
    out = pl.pallas_call(_merge_kernel, out_shape=jax.ShapeDtypeStruct(flat[0].shape, F32),
                         compiler_params=pltpu.CompilerParams(vmem_limit_bytes=VMEM_LIMIT),
                         name="merge_groups")(*flat)
    return out


def _out_proj_kernel(*refs, pieces):
    x_ref = refs[0]
    o_refs = refs[1:1 + pieces]
    g_refs = refs[1 + pieces:1 + 2 * pieces]
    w_ref, gp_ref, y_ref = refs[1 + 2 * pieces:]
    acc = None
    for p in range(pieces):
        g = g_refs[p][...]
        mix = (o_refs[p][...] * (g * (1.0 / (1.0 + jnp.exp(-g))))).astype(BF16)
        d = _dot(mix, w_ref[p * GATE_W:(p + 1) * GATE_W, :])
        acc = d if acc is None else acc + d
    r = lax.rsqrt(jnp.mean(acc * acc, axis=-1, keepdims=True) + EPS)
    y_ref[...] = x_ref[...] + acc * r * gp_ref[...]


def out_proj(x, outs, h, gate_offs, w_out, g_post, tm=256):
    m, d = x.shape
    tm = min(tm, m)
    o_args, o_specs = [], []
    for arr in outs:
        for c in range(arr.shape[1] // GATE_W):
            o_args.append(arr)
            o_specs.append(pl.BlockSpec((tm, GATE_W), lambda i, c=c: (i, c)))
    g_specs = [pl.BlockSpec((tm, GATE_W), lambda i, c=off // GATE_W: (i, c)) for off in gate_offs]
    pieces = len(o_args)
    assert pieces == len(gate_offs) and pieces * GATE_W == w_out.shape[0]
    return pl.pallas_call(
        functools.partial(_out_proj_kernel, pieces=pieces),
        grid=(m // tm,),
        in_specs=[pl.BlockSpec((tm, d), lambda i: (i, 0))] + o_specs + g_specs + [
            pl.BlockSpec(w_out.shape, lambda i: (0, 0)), pl.BlockSpec((1, d), lambda i: (0, 0))],
        out_specs=pl.BlockSpec((tm, d), lambda i: (i, 0)),
        out_shape=jax.ShapeDtypeStruct((m, d), F32),
        compiler_params=_cparams("parallel"),
        name="out_proj",
    )(x, *o_args, *([h] * pieces), w_out, g_post.reshape(1, d))


def _layernorm(x, g, b):
    xc = x - jnp.mean(x, axis=-1, keepdims=True)
    y = xc * lax.rsqrt(jnp.mean(xc * xc, axis=-1, keepdims=True) + EPS)
    return y * g + b


def _cmix_prompt_kernel(cu_ref, cv_ref, lg_ref, lb_ref, ws_ref, bs_ref, o_ref):
    v = _layernorm(cv_ref[...], lg_ref[...], lb_ref[...]).astype(BF16)
    tril = _iota((C_CHUNK, C_CHUNK), 1) <= _iota((C_CHUNK, C_CHUNK), 0)
    for g in range(C_GROUPS):
        cols = slice(g * C_GDIM, (g + 1) * C_GDIM)
        w = jnp.where(tril, ws_ref[g], 0.0).astype(BF16)
        for c in range(cu_ref.shape[0] // C_CHUNK):
            rows = slice(c * C_CHUNK, (c + 1) * C_CHUNK)
            o_ref[rows, cols] = cu_ref[rows, cols] * (_dot(w, v[rows, cols]) + bs_ref[:, cols])


def cmix_prompt(h, ln_g, ln_b, ws, bs, tm=512):
    m = h.shape[0]
    bs_rows = jnp.repeat(bs.T, C_GDIM, axis=1)
    vec = pl.BlockSpec((1, C_WIDTH), lambda i: (0, 0))
    return pl.pallas_call(
        _cmix_prompt_kernel,
        grid=(m // tm,),
        in_specs=[pl.BlockSpec((tm, C_WIDTH), lambda i: (i, O_CU // C_WIDTH)),
                  pl.BlockSpec((tm, C_WIDTH), lambda i: (i, O_CV // C_WIDTH)),
                  vec, vec,
                  pl.BlockSpec(ws.shape, lambda i: (0, 0, 0)),
                  pl.BlockSpec(bs_rows.shape, lambda i: (0, 0))],
        out_specs=pl.BlockSpec((tm, C_WIDTH), lambda i: (i, 0)),
        out_shape=jax.ShapeDtypeStruct((m, C_WIDTH), F32),
        compiler_params=_cparams("parallel"),
        name="cmix_prompt",
    )(h, h, ln_g.reshape(1, -1), ln_b.reshape(1, -1), ws, bs_rows)


def _cmix_sample_kernel(cu_ref, cv_ref, lg_ref, lb_ref, w_ref, bs_ref, o_ref, v_ref):
    v = _layernorm(cv_ref[...], lg_ref[...], lb_ref[...])
    v_ref[...] = v
    t = v.shape[1]
    i = _iota((t, C_WIDTH), 0)
    acc = jnp.broadcast_to(bs_ref[...][None], v.shape)
    for j in range(t):
        wj = jnp.where(i >= j, w_ref[j], 0.0)
        acc = acc + wj[None] * v[:, j:j + 1, :]
    o_ref[...] = cu_ref[...] * acc


def cmix_sample(h3, ln_g, ln_b, ws, bs):
    n, t, _ = h3.shape
    w_cols = jnp.repeat(ws[:, :t, :t].transpose(2, 1, 0), C_GDIM, axis=2)
    bs_rows = jnp.repeat(bs[:, :t].T, C_GDIM, axis=1)
    vec = pl.BlockSpec((1, 1, C_WIDTH), lambda i: (0, 0, 0))
    shape = jax.ShapeDtypeStruct((n, t, C_WIDTH), F32)
    blk = pl.BlockSpec((n, t, C_WIDTH), lambda i: (0, 0, 0))
    return pl.pallas_call(
        _cmix_sample_kernel,
        grid=(1,),
        in_specs=[pl.BlockSpec((n, t, C_WIDTH), lambda i: (0, 0, O_CU // C_WIDTH)),
                  pl.BlockSpec((n, t, C_WIDTH), lambda i: (0, 0, O_CV // C_WIDTH)),
                  vec, vec,
                  pl.BlockSpec(w_cols.shape, lambda i: (0, 0, 0)),
                  pl.BlockSpec(bs_rows.shape, lambda i: (0, 0))],
        out_specs=[blk, blk],
        out_shape=[shape, shape],
        compiler_params=_cparams("arbitrary"),
        name="cmix_sample",
    )(h3, h3, ln_g.reshape(1, 1, -1), ln_b.reshape(1, 1, -1), w_cols, bs_rows)


def _top_blocks(scores, allowed):
    bidx = _iota(scores.shape, 1)
    cur = jnp.where(allowed, scores, NEG)
    sel = jnp.zeros(scores.shape, F32)
    for _ in range(D_TOPK):
        mx = cur.max(-1, keepdims=True)
        first = jnp.where(cur == mx, bidx, scores.shape[1]).min(-1, keepdims=True)
        pick = bidx == first
        sel = jnp.where(jnp.logical_and(pick, allowed), 1.0, sel)
        cur = jnp.where(pick, BELOW_NEG, cur)
    return sel


def _moba_prompt_kernel(q_ref, k_ref, v_ref, o_ref, km_s):
    seq = k_ref.shape[0]
    nb = seq // D_BLOCK
    g = D_HEADS // D_KV
    qb = pl.program_id(2)

    @pl.when(qb == 0)
    def _():
        for b in range(nb):
            km_s[b:b + 1, :] = jnp.mean(k_ref[b * D_BLOCK:(b + 1) * D_BLOCK, :], axis=0, keepdims=True)

    own = pl.multiple_of(qb * D_BLOCK, D_BLOCK)
    k_own = k_ref[pl.ds(own, D_BLOCK), :].astype(BF16)
    v_own = v_ref[pl.ds(own, D_BLOCK), :].astype(BF16)
    causal = _iota((D_BLOCK, D_BLOCK), 1) <= _iota((D_BLOCK, D_BLOCK), 0)
    bidx = _iota((D_BLOCK, nb), 1)
    for gi in range(g):
        cols = slice(gi * HEAD_DIM, (gi + 1) * HEAD_DIM)
        qf = q_ref[:, cols] * ATTN_SCALE
        q = qf.astype(BF16)
        sel = _top_blocks(_dot_nt(qf, km_s[...], precision=lax.Precision.HIGHEST), bidx < qb)
        s = jnp.where(causal, _dot_nt(q, k_own), NEG)
        m = s.max(-1, keepdims=True)
        p = jnp.exp(s - m)
        l = p.sum(-1, keepdims=True)
        acc = _dot(p.astype(BF16), v_own)

        def body(j, carry):
            m, l, acc = carry
            start = pl.multiple_of(j * D_BLOCK, D_BLOCK)
            kj = k_ref[pl.ds(start, D_BLOCK), :].astype(BF16)
            vj = v_ref[pl.ds(start, D_BLOCK), :].astype(BF16)
            chosen = jnp.sum(jnp.where(bidx == j, sel, 0.0), axis=-1, keepdims=True) > 0.5
            s = jnp.where(chosen, _dot_nt(q, kj), NEG)
            m_new = jnp.maximum(m, s.max(-1, keepdims=True))
            alpha = jnp.exp(m - m_new)
            p = jnp.exp(s - m_new)
            return m_new, alpha * l + p.sum(-1, keepdims=True), alpha * acc + _dot(p.astype(BF16), vj)

        m, l, acc = lax.fori_loop(0, qb, body, (m, l, acc))
        o_ref[:, cols] = acc / l


def moba_prompt(h3):
    n, seq, _ = h3.shape
    gw = (D_HEADS // D_KV) * HEAD_DIM
    return pl.pallas_call(
        _moba_prompt_kernel,
        grid=(n, D_KV, seq // D_BLOCK),
        in_specs=[pl.BlockSpec((None, D_BLOCK, gw), lambda b, hh, i: (b, i, O_DQ // gw + hh)),
                  pl.BlockSpec((None, seq, HEAD_DIM), lambda b, hh, i: (b, 0, O_DK // HEAD_DIM + hh)),
                  pl.BlockSpec((None, seq, HEAD_DIM), lambda b, hh, i: (b, 0, O_DV // HEAD_DIM + hh))],
        out_specs=pl.BlockSpec((None, D_BLOCK, gw), lambda b, hh, i: (b, i, hh)),
        out_shape=jax.ShapeDtypeStruct((n, seq, D_WIDTH), F32),
        scratch_shapes=[pltpu.VMEM((seq // D_BLOCK, HEAD_DIM), F32)],
        compiler_params=_cparams("parallel", "parallel", "arbitrary"),
        name="moba_prompt",
    )(h3, h3, h3)


def _kv_pages_kernel(k_ref, v_ref, o_ref):
    for kv, src in enumerate((k_ref, v_ref)):
        for hh in range(D_KV):
            o_ref[kv, hh] = src[:, hh * HEAD_DIM:(hh + 1) * HEAD_DIM]


def kv_pages(h3):
    n, seq, _ = h3.shape
    kw = D_KV * HEAD_DIM
    return pl.pallas_call(
        _kv_pages_kernel,
        grid=(n, seq // PAGE_SIZE),
        in_specs=[pl.BlockSpec((None, PAGE_SIZE, kw), lambda b, p: (b, p, O_DK // kw)),
                  pl.BlockSpec((None, PAGE_SIZE, kw), lambda b, p: (b, p, O_DV // kw))],
        out_specs=pl.BlockSpec((None, None, 2, D_KV, PAGE_SIZE, HEAD_DIM), lambda b, p: (b, p, 0, 0, 0, 0)),
        out_shape=jax.ShapeDtypeStruct((n, seq // PAGE_SIZE, 2, D_KV, PAGE_SIZE, HEAD_DIM), F32),
        compiler_params=_cparams("parallel", "parallel"),
        name="kv_pages",
    )(h3, h3)


MOBA_PAGES_PER_STEP = 16


def _moba_decode_kernel(pt_ref, q_ref, kn_ref, vn_ref, *refs):
    pps = MOBA_PAGES_PER_STEP
    pages = refs[:pps]
    o_ref, lg_s, km_s, p_s, acc_s, m_s, l_s = refs[pps:]
    del pt_ref
    n_pages = lg_s.shape[1]
    k_steps = n_pages // pps
    ppb = D_BLOCK // PAGE_SIZE
    n_blk = n_pages // ppb
    bps = pps // ppb
    g = D_HEADS // D_KV
    nq = q_ref.shape[1]
    step = pl.program_id(1)

    def own_logits(hk):
        q = (q_ref[hk] * ATTN_SCALE).astype(BF16)
        s = _dot_nt(q, kn_ref[hk].astype(BF16))
        vis = _iota(s.shape, 1) <= _div(_iota(s.shape, 0), g)
        return jnp.where(vis, s, NEG)

    @pl.when(step < k_steps)
    def _():
        for hk in range(D_KV):
            q = (q_ref[hk] * ATTN_SCALE).astype(BF16)
            ks = [pages[i][hk] for i in range(pps)]
            s = _dot_nt(q, jnp.concatenate([k.astype(BF16) for k in ks], axis=0))
            for i in range(pps):
                lg_s[hk, step * pps + i] = s[:, i * PAGE_SIZE:(i + 1) * PAGE_SIZE]
            sums = [k.sum(axis=0, keepdims=True) for k in ks]
            means = [functools.reduce(lambda a, b: a + b, sums[b * ppb:(b + 1) * ppb]) * (1.0 / D_BLOCK)
                     for b in range(bps)]
            km_s[hk, pl.ds(pl.multiple_of(step * bps, bps), bps), :] = jnp.concatenate(means, axis=0)

    @pl.when(step == k_steps - 1)
    def _():
        for hk in range(D_KV):
            qf = q_ref[hk] * ATTN_SCALE
            scores = _dot_nt(qf, km_s[hk], precision=lax.Precision.HIGHEST)
            sel = _top_blocks(scores, _iota(scores.shape, 1) >= 0)
            chosen = [sel[:, b:b + 1] > 0.5 for b in range(n_blk)]
            mx = jnp.full((nq, PAGE_SIZE), NEG, F32)
            for pg in range(n_pages):
                mx = jnp.maximum(mx, jnp.where(chosen[pg // ppb], lg_s[hk, pg], NEG))
            m = jnp.maximum(mx.max(-1, keepdims=True), own_logits(hk).max(-1, keepdims=True))
            tot = jnp.zeros((nq, PAGE_SIZE), F32)
            for pg in range(n_pages):
                p = jnp.exp(jnp.where(chosen[pg // ppb], lg_s[hk, pg], NEG) - m)
                tot = tot + p
                p_s[hk, pg] = p.astype(BF16)
            m_s[hk] = jnp.broadcast_to(m, (nq, PAGE_SIZE))
            l_s[hk] = jnp.broadcast_to(tot.sum(-1, keepdims=True), (nq, PAGE_SIZE))

    @pl.when(step >= k_steps)
    def _():
        for hk in range(D_KV):
            base = (step - k_steps) * pps
            p = jnp.concatenate([p_s[hk, base + i] for i in range(pps)], axis=1)
            v = jnp.concatenate([pages[i][hk].astype(BF16) for i in range(pps)], axis=0)
            pv = _dot(p, v)

            @pl.when(step == k_steps)
            def _():
                acc_s[hk] = pv

            @pl.when(step > k_steps)
            def _():
                acc_s[hk] = acc_s[hk] + pv

    @pl.when(step == 2 * k_steps - 1)
    def _():
        for hk in range(D_KV):
            po = jnp.exp(own_logits(hk) - m_s[hk][:, :1])
            den = l_s[hk][:, :1] + po.sum(-1, keepdims=True)
            o_ref[hk] = (acc_s[hk] + _dot(po.astype(BF16), vn_ref[hk].astype(BF16))) / den


def moba_decode(pool, page_table, q, k_new, v_new):
    n, n_pages = page_table.shape
    assert (n_pages * PAGE_SIZE) % D_BLOCK == 0 and n_pages % MOBA_PAGES_PER_STEP == 0
    pps = MOBA_PAGES_PER_STEP
    k_steps = n_pages // pps
    nq, t = q.shape[2], k_new.shape[2]

    def page_spec(i):
        def index(b, s, pt):
            return (pt[b, (s % k_steps) * pps + i], s // k_steps, 0, 0, 0)
        return pl.BlockSpec((None, None, D_KV, PAGE_SIZE, HEAD_DIM), index)

    def per_seq(r):
        return pl.BlockSpec((None, D_KV, r, HEAD_DIM), lambda b, s, pt: (b, 0, 0, 0))

    grid_spec = pltpu.PrefetchScalarGridSpec(
        num_scalar_prefetch=1,
        grid=(n, 2 * k_steps),
        in_specs=[per_seq(nq), per_seq(t), per_seq(t)] + [page_spec(i) for i in range(pps)],
        out_specs=per_seq(nq),
        scratch_shapes=[pltpu.VMEM((D_KV, n_pages, nq, PAGE_SIZE), F32),
                        pltpu.VMEM((D_KV, n_pages * PAGE_SIZE // D_BLOCK, HEAD_DIM), F32),
                        pltpu.VMEM((D_KV, n_pages, nq, PAGE_SIZE), BF16),
                        pltpu.VMEM((D_KV, nq, HEAD_DIM), F32),
                        pltpu.VMEM((D_KV, nq, PAGE_SIZE), F32),
                        pltpu.VMEM((D_KV, nq, PAGE_SIZE), F32)])
    return pl.pallas_call(
        _moba_decode_kernel,
        grid_spec=grid_spec,
        out_shape=jax.ShapeDtypeStruct((n, D_KV, nq, HEAD_DIM), F32),
        compiler_params=_cparams("arbitrary", "arbitrary"),
        name="moba_decode",
    )(page_table, q, k_new, v_new, *([pool] * pps))


def even_layer_prompt(x, w_in, w_out, g_pre, g_post, sink):
    n, s, d = x.shape
    x2 = x.reshape(n * s, d)
    h = norm_proj(x2, g_pre, w_in, tm=1024)
    h3 = h.reshape(n, s, IN_EVEN)
    a_out = attn_a_prompt(h3).reshape(n * s, A_WIDTH)
    b_out = attn_b_prompt(h3, sink).reshape(n * s, B_WIDTH)
    y = out_proj(x2, [a_out, b_out], h, [E_AG, E_BG, E_BG + GATE_W], w_out, g_post)
    rows = [kv_rows(h3, E_AK + gi * A_WIDTH, E_AV + gi * A_WIDTH, A_HEADS, win) for gi, (win, _) in enumerate(A_GROUPS)]
    rows.append(kv_rows(h3, E_BK, E_BV, B_KV, B_WINDOW))
    return y.reshape(n, s, d), rows


def even_layer_sample(x, bufs_a, buf_b, w_in, w_out, g_pre, g_post, sink):
    n, t, d = x.shape
    x2 = x.reshape(n * t, d)
    h = norm_proj(x2, g_pre, w_in, tm=n * t)
    stats, new_bufs = [], []
    for gi, (_, dil) in enumerate(A_GROUPS):
        cols = slice(gi * A_WIDTH, (gi + 1) * A_WIDTH)
        q = h[:, E_AQ:E_AK][:, cols].reshape(n, t * A_HEADS, HEAD_DIM)
        k = h[:, E_AK:E_AV][:, cols].reshape(n, t * A_HEADS, HEAD_DIM)
        v = h[:, E_AV:E_AG][:, cols].reshape(n, t * A_HEADS, HEAD_DIM)
        buf, o, m, l = decode_attn(bufs_a[gi], q, k, v, A_HEADS, A_HEADS, dil)
        stats.append((o, m, l))
        new_bufs.append(buf)
    a_out = merge_groups(stats).reshape(n * t, A_WIDTH)
    q = h[:, E_BQ:E_BK].reshape(n, t * B_HEADS, HEAD_DIM)
    k = h[:, E_BK:E_BV].reshape(n, t * B_KV, HEAD_DIM)
    v = h[:, E_BV:E_BG].reshape(n, t * B_KV, HEAD_DIM)
    sink_rows = jnp.broadcast_to(jnp.tile(sink, t)[:, None], (t * B_HEADS, HEAD_DIM))
    buf, b_out = decode_attn(buf_b, q, k, v, B_KV, B_HEADS, 1, sink=sink_rows)
    new_bufs.append(buf)
    y = out_proj(x2, [a_out, b_out.reshape(n * t, B_WIDTH)], h, [E_AG, E_BG, E_BG + GATE_W], w_out, g_post)
    return y.reshape(n, t, d), new_bufs


def odd_layer_prompt(x, w_in, w_out, g_pre, g_post, ln_g, ln_b, ws, bs):
    n, s, d = x.shape
    x2 = x.reshape(n * s, d)
    h = norm_proj(x2, g_pre, w_in, tm=1024)
    h3 = h.reshape(n, s, IN_ODD)
    c_out = cmix_prompt(h, ln_g, ln_b, ws, bs)
    d_out = moba_prompt(h3).reshape(n * s, D_WIDTH)
    gates = [O_CG, O_CG + GATE_W, O_DG, O_DG + GATE_W]
    y = out_proj(x2, [c_out, d_out], h, gates, w_out, g_post)
    return y.reshape(n, s, d), kv_pages(h3)


def odd_layer_sample(x, pool, page_table, w_in, w_out, g_pre, g_post, ln_g, ln_b, ws, bs):
    n, t, d = x.shape
    g = D_HEADS // D_KV
    x2 = x.reshape(n * t, d)
    h = norm_proj(x2, g_pre, w_in, tm=n * t)
    c_out, v_rows = cmix_sample(h.reshape(n, t, IN_ODD), ln_g, ln_b, ws, bs)
    q = h[:, O_DQ:O_DK].reshape(n, t, D_KV, g, HEAD_DIM).transpose(0, 2, 1, 3, 4).reshape(n, D_KV, t * g, HEAD_DIM)
    k = h[:, O_DK:O_DV].reshape(n, t, D_KV, HEAD_DIM).transpose(0, 2, 1, 3)
    v = h[:, O_DV:O_DG].reshape(n, t, D_KV, HEAD_DIM).transpose(0, 2, 1, 3)
    o = moba_decode(pool, page_table, q, k, v)
    d_out = o.reshape(n, D_KV, t, g, HEAD_DIM).transpose(0, 2, 1, 3, 4).reshape(n * t, D_WIDTH)
    gates = [O_CG, O_CG + GATE_W, O_DG, O_DG + GATE_W]
    y = out_proj(x2, [c_out.reshape(n * t, C_WIDTH), d_out], h, gates, w_out, g_post)
    return y.reshape(n, t, d), jnp.stack([k, v], axis=1), v_rows


def kernel(x_prompt, x_sample, cache_a1, cache_a2, cache_a3, cache_b, cache_d, page_table, norm_pre_even, norm_post_even, w_in_even, w_out_even, sink_b, norm_pre_odd, norm_post_odd, w_in_odd, w_out_odd, c_ln_g, c_ln_b, c_ws, c_bs):
    yp, ys = x_prompt, x_sample
    caches_a = (cache_a1, cache_a2, cache_a3)
    a_p, a_s = [[] for _ in range(A_NG)], [[] for _ in range(A_NG)]
    b_p, b_s, d_p, d_s, c_s = [], [], [], [], []
    for layer in range(DEPTH):
        li = layer // 2
        if layer % 2 == 0:
            w_in, w_out = w_in_even[li].astype(BF16), w_out_even[li].astype(BF16)
            yp, rows = even_layer_prompt(yp, w_in, w_out, norm_pre_even[li], norm_post_even[li], sink_b[li])
            ys, bufs = even_layer_sample(ys, [c[li] for c in caches_a], cache_b[li], w_in, w_out,
                                         norm_pre_even[li], norm_post_even[li], sink_b[li])
            for gi in range(A_NG):
                a_p[gi].append(rows[gi])
                a_s[gi].append(bufs[gi])
            b_p.append(rows[A_NG])
            b_s.append(bufs[A_NG])
        else:
            w_in, w_out = w_in_odd[li].astype(BF16), w_out_odd[li].astype(BF16)
            yp, kvp = odd_layer_prompt(yp, w_in, w_out, norm_pre_odd[li], norm_post_odd[li],
                                       c_ln_g[li], c_ln_b[li], c_ws[li], c_bs[li])
            ys, kvs, vrows = odd_layer_sample(ys, cache_d[li], page_table, w_in, w_out, norm_pre_odd[li],
                                              norm_post_odd[li], c_ln_g[li], c_ln_b[li], c_ws[li], c_bs[li])
            d_p.append(kvp)
            d_s.append(kvs)
            c_s.append(vrows)
    return (yp, ys, jnp.stack(a_p[0]), jnp.stack(a_s[0]), jnp.stack(a_p[1]), jnp.stack(a_s[1]),
            jnp.stack(a_p[2]), jnp.stack(a_s[2]), jnp.stack(b_p), jnp.stack(b_s),
            jnp.stack(d_p), jnp.stack(d_s), jnp.stack(c_s))
```

```python
import functools

import jax
import jax.numpy as jnp
from jax import lax
from jax.experimental import pallas as pl
from jax.experimental.pallas import tpu as pltpu

F32 = jnp.float32
BF16 = jnp.bfloat16

D_MODEL = 2048
DEPTH = 2
DEC_SEQ = 8
PAST_LEN = 8192
PAGE_SIZE = 128
HEAD_DIM = 128
ATTN_SCALE = HEAD_DIM ** -0.5
A_GROUPS = ((128, 1), (512, 4), (2048, 16))
A_NG = len(A_GROUPS)
A_HEADS = 4
A_QKV = A_NG * A_HEADS * HEAD_DIM
A_WIDTH = A_HEADS * HEAD_DIM
B_HEADS = 8
B_KV = 2
B_WINDOW = 128
B_WIDTH = B_HEADS * HEAD_DIM
C_CHUNK = 128
C_GROUPS = 8
C_GDIM = 128
C_WIDTH = C_GROUPS * C_GDIM
D_HEADS = 8
D_KV = 2
D_BLOCK = 256
D_TOPK = 3
D_WIDTH = D_HEADS * HEAD_DIM
EPS = 1e-6
NEG = -1e30
BELOW_NEG = -3e38

E_AQ, E_AK, E_AV = 0, A_QKV, 2 * A_QKV
E_AG = 3 * A_QKV
E_BQ = E_AG + A_WIDTH
E_BK = E_BQ + B_WIDTH
E_BV = E_BK + B_KV * HEAD_DIM
E_BG = E_BV + B_KV * HEAD_DIM
IN_EVEN = E_BG + B_WIDTH
O_CU, O_CV, O_CG = 0, C_WIDTH, 2 * C_WIDTH
O_DQ = 3 * C_WIDTH
O_DK = O_DQ + D_WIDTH
O_DV = O_DK + D_KV * HEAD_DIM
O_DG = O_DV + D_KV * HEAD_DIM
IN_ODD = O_DG + D_WIDTH

BAND = 128
GATE_W = 512
VMEM_LIMIT = 48 * 1024 * 1024


def _cparams(*sem):
    return pltpu.CompilerParams(dimension_semantics=sem, vmem_limit_bytes=VMEM_LIMIT)


def _dot_nt(a, b, precision=None):
    return lax.dot_general(a, b, (((1,), (1,)), ((), ())), preferred_element_type=F32, precision=precision)


def _dot(a, b):
    return jnp.dot(a, b, preferred_element_type=F32)


def _iota(shape, dim):
    return lax.broadcasted_iota(jnp.int32, shape, dim)


def _div(x, d):
    assert d & (d - 1) == 0
    return x >> (d.bit_length() - 1)


def _mod(x, d):
    assert d & (d - 1) == 0
    return x & (d - 1)


def _norm_proj_kernel(x_ref, g_ref, w_ref, o_ref, xn_ref):
    @pl.when(pl.program_id(1) == 0)
    def _():
        x = x_ref[...]
        r = lax.rsqrt(jnp.mean(x * x, axis=-1, keepdims=True) + EPS)
        xn_ref[...] = (x * r * g_ref[...]).astype(BF16)

    o_ref[...] = _dot(xn_ref[...], w_ref[...])


def norm_proj(x, g, w, tm, tn=512):
    m, d = x.shape
    n = w.shape[1]
    return pl.pallas_call(
        _norm_proj_kernel,
        grid=(m // tm, n // tn),
        in_specs=[pl.BlockSpec((tm, d), lambda i, j: (i, 0)),
                  pl.BlockSpec((1, d), lambda i, j: (0, 0)),
                  pl.BlockSpec((d, tn), lambda i, j: (0, j))],
        out_specs=pl.BlockSpec((tm, tn), lambda i, j: (i, j)),
        out_shape=jax.ShapeDtypeStruct((m, n), F32),
        scratch_shapes=[pltpu.VMEM((tm, d), BF16)],
        compiler_params=_cparams("parallel", "arbitrary"),
        name="norm_proj",
    )(x, g.reshape(1, d), w)


def _band_block(q, kc, vc, kp, vp, prev_floor=0):
    r = q.shape[0]
    i = _mod(_iota((r, BAND), 0), BAND)
    j = _iota((r, BAND), 1)
    sc = jnp.where(j <= i, _dot_nt(q, kc), NEG)
    m = sc.max(-1, keepdims=True)
    if kp is not None:
        sp = jnp.where(j >= i + prev_floor, _dot_nt(q, kp), NEG)
        m = jnp.maximum(m, sp.max(-1, keepdims=True))
    pc = jnp.exp(sc - m)
    l = pc.sum(-1, keepdims=True)
    o = _dot(pc.astype(BF16), vc)
    if kp is not None:
        pp = jnp.exp(sp - m)
        l = l + pp.sum(-1, keepdims=True)
        o = o + _dot(pp.astype(BF16), vp)
    return o, m, l


def _rows(ref, start, stride):
    if stride == 1:
        return ref[pl.ds(start, BAND), :]
    return ref[pl.ds(start, BAND, stride=stride), :]


BG_CHUNK_SEQS = 16


def _job_copies(src_refs, new_refs, dst_refs, sem):
    copies = []
    for k, (src, new, dst) in enumerate(zip(src_refs, new_refs, dst_refs)):
        n, rows, fresh = src.shape[0], src.shape[1], new.shape[1]
        for c0 in range(0, n, BG_CHUNK_SEQS):
            cn = min(BG_CHUNK_SEQS, n - c0)
            copies.append(pltpu.make_async_copy(src.at[pl.ds(c0, cn), pl.ds(fresh, rows - fresh), :],
                                                dst.at[pl.ds(c0, cn), pl.ds(0, rows - fresh), :], sem.at[k]))
        copies.append(pltpu.make_async_copy(new, dst.at[:, pl.ds(rows - fresh, fresh), :], sem.at[k]))
    return copies


def hosted_call(body, jobs, *, grid, in_specs, out_specs, out_shape, scratch_shapes, args, name):
    nj = len(jobs)
    n_in, n_out = len(in_specs), len(out_specs)
    ndim = len(grid)

    def kernel(*refs):
        srcs, news = refs[n_in:n_in + nj], refs[n_in + nj:n_in + 2 * nj]
        p = n_in + 2 * nj
        main_out, dsts = refs[p:p + n_out], refs[p + n_out:p + n_out + nj]
        scratch, sem = refs[p + n_out + nj:-1], refs[-1]
        first = functools.reduce(jnp.logical_and, [pl.program_id(d) == 0 for d in range(ndim)])
        last = functools.reduce(jnp.logical_and, [pl.program_id(d) == pl.num_programs(d) - 1 for d in range(ndim)])

        @pl.when(first)
        def _():
            for c in _job_copies(srcs, news, dsts, sem):
                c.start()

        body(*refs[:n_in], *main_out, *scratch)

        @pl.when(last)
        def _():
            for c in _job_copies(srcs, news, dsts, sem):
                c.wait()

    hbm = pl.BlockSpec(memory_space=pl.ANY)
    outs = pl.pallas_call(
        kernel if nj else body,
        grid=grid,
        in_specs=list(in_specs) + [hbm] * (2 * nj),
        out_specs=list(out_specs) + [hbm] * nj,
        out_shape=list(out_shape) + [jax.ShapeDtypeStruct(src.shape, F32) for src, _ in jobs],
        scratch_shapes=list(scratch_shapes) + ([pltpu.SemaphoreType.DMA((nj,))] if nj else []),
        compiler_params=_cparams(*(["arbitrary"] * ndim)),
        name=name,
    )(*args, *[src for src, _ in jobs], *[new for _, new in jobs])
    return list(outs[:n_out]), list(outs[n_out:])


def _attn_a_prompt_kernel(q1, k1, v1, q2, k2, v2, q3, k3, v3, o_ref, o1_s, m1_s, l1_s, o2_s, m2_s, l2_s):
    seq = q1.shape[0]
    wide = (BAND, HEAD_DIM)

    def load(qr, kr, vr, start, stride, prev_start):
        q = (_rows(qr, start, stride) * ATTN_SCALE).astype(BF16)
        kc = _rows(kr, start, stride).astype(BF16)
        vc = _rows(vr, start, stride).astype(BF16)
        if prev_start is None:
            return q, kc, vc, None, None
        return q, kc, vc, _rows(kr, prev_start, stride).astype(BF16), _rows(vr, prev_start, stride).astype(BF16)

    def g1_body(b, carry):
        start = pl.multiple_of(b * BAND, BAND)
        prev = pl.multiple_of(jnp.maximum(b - 1, 0) * BAND, BAND)
        o, m, l = _band_block(*load(q1, k1, v1, start, 1, prev), prev_floor=jnp.where(b > 0, 0, BAND))
        o1_s[pl.ds(start, BAND), :] = o
        m1_s[pl.ds(start, BAND), :] = jnp.broadcast_to(m, wide)
        l1_s[pl.ds(start, BAND), :] = jnp.broadcast_to(l, wide)
        return carry

    lax.fori_loop(0, seq // BAND, g1_body, 0)

    d2 = A_GROUPS[1][1]
    sub2 = seq // d2
    for r in range(d2):
        for b in range(sub2 // BAND):
            start = r + b * BAND * d2
            prev = None if b == 0 else start - BAND * d2
            o, m, l = _band_block(*load(q2, k2, v2, start, d2, prev))
            o2_s[r, b * BAND:(b + 1) * BAND, :] = o
            m2_s[r, b * BAND:(b + 1) * BAND, :] = jnp.broadcast_to(m, wide)
            l2_s[r, b * BAND:(b + 1) * BAND, :] = jnp.broadcast_to(l, wide)

    d3 = A_GROUPS[2][1]
    ratio = d3 // d2
    for r in range(d3):
        o3, m3, l3 = _band_block(*load(q3, k3, v3, r, d3, None))
        m3 = jnp.broadcast_to(m3, wide)
        l3 = jnp.broadcast_to(l3, wide)
        o1 = o1_s[pl.ds(r, BAND, stride=d3), :]
        m1 = m1_s[pl.ds(r, BAND, stride=d3), :]
        l1 = l1_s[pl.ds(r, BAND, stride=d3), :]
        r2, off2 = r % d2, r // d2
        o2 = o2_s[r2, pl.ds(off2, BAND, stride=ratio), :]
        m2 = m2_s[r2, pl.ds(off2, BAND, stride=ratio), :]
        l2 = l2_s[r2, pl.ds(off2, BAND, stride=ratio), :]
        mmax = jnp.maximum(jnp.maximum(m1, m2), m3)
        e1, e2, e3 = jnp.exp(m1 - mmax), jnp.exp(m2 - mmax), jnp.exp(m3 - mmax)
        num = o1 * e1 + o2 * e2 + o3 * e3
        den = l1 * e1 + l2 * e2 + l3 * e3
        o_ref[pl.ds(r, BAND, stride=d3), :] = num / den


def attn_a_prompt(h3, jobs=()):
    n, seq, _ = h3.shape
    assert seq == A_GROUPS[2][0] and all(w // d == BAND for w, d in A_GROUPS)

    def spec(off, gi):
        blk = off // HEAD_DIM + gi * A_HEADS
        return pl.BlockSpec((None, seq, HEAD_DIM), lambda b, hh: (b, 0, blk + hh))

    in_specs = []
    for gi in range(A_NG):
        in_specs += [spec(E_AQ, gi), spec(E_AK, gi), spec(E_AV, gi)]
    d2 = A_GROUPS[1][1]
    stat = pltpu.VMEM((seq, HEAD_DIM), F32)
    stat2 = pltpu.VMEM((d2, seq // d2, HEAD_DIM), F32)
    (out,), updated = hosted_call(
        _attn_a_prompt_kernel, jobs,
        grid=(n, A_HEADS),
        in_specs=in_specs,
        out_specs=[pl.BlockSpec((None, seq, HEAD_DIM), lambda b, hh: (b, 0, hh))],
        out_shape=[jax.ShapeDtypeStruct((n, seq, A_WIDTH), F32)],
        scratch_shapes=[stat, stat, stat, stat2, stat2, stat2],
        args=[h3] * 9,
        name="attn_a_prompt")
    return out, updated


def _attn_b_prompt_kernel(sink_ref, q_ref, k_ref, v_ref, o_ref):
    seq = q_ref.shape[0]
    g = B_HEADS // B_KV
    kvh = pl.program_id(1)
    sink = jnp.concatenate([jnp.full((BAND, 1), sink_ref[kvh * g + gi], F32) for gi in range(g)], axis=0)

    def body(b, carry):
        start = pl.multiple_of(b * BAND, BAND)
        prev = pl.multiple_of(jnp.maximum(b - 1, 0) * BAND, BAND)
        qb = q_ref[pl.ds(start, BAND), :]
        q = jnp.concatenate([qb[:, gi * HEAD_DIM:(gi + 1) * HEAD_DIM] for gi in range(g)], axis=0)
        q = (q * ATTN_SCALE).astype(BF16)
        kc = k_ref[pl.ds(start, BAND), :].astype(BF16)
        vc = v_ref[pl.ds(start, BAND), :].astype(BF16)
        kp = k_ref[pl.ds(prev, BAND), :].astype(BF16)
        vp = v_ref[pl.ds(prev, BAND), :].astype(BF16)
        o, m, l = _band_block(q, kc, vc, kp, vp, prev_floor=jnp.where(b > 0, 0, BAND))
        m2 = jnp.maximum(m, sink)
        a = jnp.exp(m - m2)
        out = o * (a / (l * a + jnp.exp(sink - m2)))
        for gi in range(g):
            o_ref[pl.ds(start, BAND), gi * HEAD_DIM:(gi + 1) * HEAD_DIM] = out[gi * BAND:(gi + 1) * BAND]
        return carry

    lax.fori_loop(0, seq // BAND, body, 0)


def attn_b_prompt(h3, sink, jobs=()):
    n, seq, _ = h3.shape
    gw = (B_HEADS // B_KV) * HEAD_DIM
    (out,), updated = hosted_call(
        _attn_b_prompt_kernel, jobs,
        grid=(n, B_KV),
        in_specs=[pl.BlockSpec(memory_space=pltpu.SMEM),
                  pl.BlockSpec((None, seq, gw), lambda b, hh: (b, 0, E_BQ // gw + hh)),
                  pl.BlockSpec((None, seq, HEAD_DIM), lambda b, hh: (b, 0, E_BK // HEAD_DIM + hh)),
                  pl.BlockSpec((None, seq, HEAD_DIM), lambda b, hh: (b, 0, E_BV // HEAD_DIM + hh))],
        out_specs=[pl.BlockSpec((None, seq, gw), lambda b, hh: (b, 0, hh))],
        out_shape=[jax.ShapeDtypeStruct((n, seq, B_WIDTH), F32)],
        scratch_shapes=[],
        args=[sink, h3, h3, h3],
        name="attn_b_prompt")
    return out, updated


KV_ROWS_BLOCK = 512


def _kv_rows_kernel(k_ref, v_ref, o_ref, *, heads):
    npos = k_ref.shape[0]
    for kv, src in enumerate((k_ref, v_ref)):
        for hh in range(heads):
            o_ref[pl.ds(kv * heads + hh, npos, stride=2 * heads), :] = src[:, hh * HEAD_DIM:(hh + 1) * HEAD_DIM]


def kv_rows(h3, k_off, v_off, heads, window):
    n, seq, _ = h3.shape
    w = min(window, seq)
    hw = heads * HEAD_DIM
    npos = min(w, KV_ROWS_BLOCK)
    first = (seq - w) // npos
    rows = npos * 2 * heads
    out = pl.pallas_call(
        functools.partial(_kv_rows_kernel, heads=heads),
        grid=(n, w // npos),
        in_specs=[pl.BlockSpec((None, npos, hw), lambda b, c: (b, first + c, k_off // hw)),
                  pl.BlockSpec((None, npos, hw), lambda b, c: (b, first + c, v_off // hw))],
        out_specs=pl.BlockSpec((None, rows, HEAD_DIM), lambda b, c: (b, c, 0)),
        out_shape=jax.ShapeDtypeStruct((n, w * 2 * heads, HEAD_DIM), F32),
        compiler_params=_cparams("parallel", "parallel"),
        name="kv_rows",
    )(h3, h3)
    return out.reshape(n, w, 2, heads, HEAD_DIM)


DEC_CHUNK = 2048


def _decode_kernel(*refs, heads, q_heads, dil, use_sink):
    if use_sink:
        sink_ref, q_ref, kn_ref, vn_ref, c_ref, out_ref, bias_s, biasn_s, l_s = refs
    else:
        q_ref, kn_ref, vn_ref, c_ref, out_ref, m_ref, l_ref, bias_s, biasn_s, l_s = refs
    slots = 2 * heads
    seqs, rows = c_ref.shape[:2]
    lb = rows // slots
    nq = q_ref.shape[1]
    t_new = kn_ref.shape[1] // heads
    per_kv = q_heads // heads
    chunk = min(DEC_CHUNK, rows)

    @pl.when(pl.program_id(0) == 0)
    def _():
        row = _iota((nq, rows), 0)
        col = _iota((nq, rows), 1)
        t = _div(row, q_heads)
        kvh = _div(_mod(row, q_heads), per_kv)
        pos = _div(col, slots)
        ok = (_mod(col, slots) == kvh) & (pos >= t) & (_mod(lb + t - pos, dil) == 0)
        bias_s[...] = jnp.where(ok, 0.0, NEG)
        rown = _iota((nq, t_new * heads), 0)
        coln = _iota((nq, t_new * heads), 1)
        tn = _div(rown, q_heads)
        kvhn = _div(_mod(rown, q_heads), per_kv)
        tk = _div(coln, heads)
        okn = (_mod(coln, heads) == kvhn) & (tk <= tn) & (_mod(tn - tk, dil) == 0)
        biasn_s[...] = jnp.where(okn, 0.0, NEG)

    def one_seq(b, carry):
        q = (q_ref[b] * ATTN_SCALE).astype(BF16)
        sn = _dot_nt(q, kn_ref[b].astype(BF16)) + biasn_s[...]
        m = sn.max(-1, keepdims=True)
        for c in range(rows // chunk):
            sl = slice(c * chunk, (c + 1) * chunk)
            s = _dot_nt(q, c_ref[b, sl, :].astype(BF16)) + bias_s[:, sl]
            l_s[:, sl] = s
            m = jnp.maximum(m, s.max(-1, keepdims=True))
        pn = jnp.exp(sn - m)
        den = pn.sum(-1, keepdims=True)
        o = _dot(pn.astype(BF16), vn_ref[b].astype(BF16))
        for c in range(rows // chunk):
            sl = slice(c * chunk, (c + 1) * chunk)
            p = jnp.exp(l_s[:, sl] - m)
            den = den + p.sum(-1, keepdims=True)
            o = o + _dot(pltpu.roll(p, heads, axis=1).astype(BF16), c_ref[b, sl, :].astype(BF16))

        if use_sink:
            sink = sink_ref[...]
            m2 = jnp.maximum(m, sink)
            a = jnp.exp(m - m2)
            out_ref[b] = o * (a / (den * a + jnp.exp(sink - m2)))
        else:
            out_ref[b] = o
            m_ref[b] = jnp.broadcast_to(m, (nq, HEAD_DIM))
            l_ref[b] = jnp.broadcast_to(den, (nq, HEAD_DIM))
        return carry

    if seqs == 1:
        one_seq(0, 0)
    else:
        lax.fori_loop(0, seqs, one_seq, 0)


DEC_BLOCK_BYTES = 8 * 1024 * 1024


def new_cache_rows(k_new, v_new, heads):
    n = k_new.shape[0]
    k4 = k_new.reshape(n, -1, 1, heads, HEAD_DIM)
    v4 = v_new.reshape(n, -1, 1, heads, HEAD_DIM)
    return jnp.concatenate([k4, v4], axis=2).reshape(n, -1, HEAD_DIM)


def decode_attn(flat, q, k_new, v_new, heads, q_heads, dil, sink=None):
    n, rows = flat.shape[:2]
    seqs = 1
    while seqs * 2 * rows * HEAD_DIM * 4 <= DEC_BLOCK_BYTES and n % (seqs * 2) == 0:
        seqs *= 2
    nq = q.shape[1]
    tk = k_new.shape[1]
    use_sink = sink is not None

    def per_seq(r):
        return pl.BlockSpec((seqs, r, HEAD_DIM), lambda b: (b, 0, 0))

    in_specs = [per_seq(nq), per_seq(tk), per_seq(tk), per_seq(rows)]
    args = [q, k_new, v_new, flat]
    stat = jax.ShapeDtypeStruct((n, nq, HEAD_DIM), F32)
    out_shape = [stat]
    out_specs = [per_seq(nq)]
    if use_sink:
        in_specs = [pl.BlockSpec((nq, HEAD_DIM), lambda b: (0, 0))] + in_specs
        args = [sink] + args
    else:
        out_shape += [stat, stat]
        out_specs += [per_seq(nq), per_seq(nq)]
    outs = pl.pallas_call(
        functools.partial(_decode_kernel, heads=heads, q_heads=q_heads, dil=dil, use_sink=use_sink),
        grid=(n // seqs,),
        in_specs=in_specs,
        out_specs=out_specs,
        out_shape=out_shape,
        scratch_shapes=[pltpu.VMEM((nq, rows), F32), pltpu.VMEM((nq, tk), F32), pltpu.VMEM((nq, rows), F32)],
        compiler_params=_cparams("arbitrary"),
        name="decode_attn",
    )(*args)
    return tuple(outs)


def _merge_kernel(*refs):
    o_ref = refs[-1]
    stats = [refs[3 * g:3 * g + 3] for g in range(A_NG)]
    mmax = functools.reduce(jnp.maximum, [m[...] for _, m, _ in stats])
    num = 0.0
    den = 0.0
    for o, m, l in stats:
        e = jnp.exp(m[...] - mmax)
        num = num + o[...] * e
        den = den + l[...] * e
    o_ref[...] = num / den


def merge_groups(stats):
    flat = [a.reshape(-1, HEAD_DIM) for st in stats for a in st]
    out = pl.pallas_call(_merge_kernel, out_shape=jax.ShapeDtypeStruct(flat[0].shape, F32),
                         compiler_params=pltpu.CompilerParams(vmem_limit_bytes=VMEM_LIMIT),
                         name="merge_groups")(*flat)
    return out


def _out_proj_kernel(*refs, pieces):
    x_ref = refs[0]
    o_refs = refs[1:1 + pieces]
    g_refs = refs[1 + pieces:1 + 2 * pieces]
    w_ref, gp_ref, y_ref = refs[1 + 2 * pieces:]
    acc = None
    for p in range(pieces):
        g = g_refs[p][...]
        mix = (o_refs[p][...] * (g * (1.0 / (1.0 + jnp.exp(-g))))).astype(BF16)
        d = _dot(mix, w_ref[p * GATE_W:(p + 1) * GATE_W, :])
        acc = d if acc is None else acc + d
    r = lax.rsqrt(jnp.mean(acc * acc, axis=-1, keepdims=True) + EPS)
    y_ref[...] = x_ref[...] + acc * r * gp_ref[...]


def out_proj(x, outs, h, gate_offs, w_out, g_post, tm=512):
    m, d = x.shape
    tm = min(tm, m)
    o_args, o_specs = [], []
    for arr in outs:
        for c in range(arr.shape[1] // GATE_W):
            o_args.append(arr)
            o_specs.append(pl.BlockSpec((tm, GATE_W), lambda i, c=c: (i, c)))
    g_specs = [pl.BlockSpec((tm, GATE_W), lambda i, c=off // GATE_W: (i, c)) for off in gate_offs]
    pieces = len(o_args)
    assert pieces == len(gate_offs) and pieces * GATE_W == w_out.shape[0]
    return pl.pallas_call(
        functools.partial(_out_proj_kernel, pieces=pieces),
        grid=(m // tm,),
        in_specs=[pl.BlockSpec((tm, d), lambda i: (i, 0))] + o_specs + g_specs + [
            pl.BlockSpec(w_out.shape, lambda i: (0, 0), pipeline_mode=pl.Buffered(1)),
            pl.BlockSpec((1, d), lambda i: (0, 0))],
        out_specs=pl.BlockSpec((tm, d), lambda i: (i, 0)),
        out_shape=jax.ShapeDtypeStruct((m, d), F32),
        compiler_params=_cparams("parallel"),
        name="out_proj",
    )(x, *o_args, *([h] * pieces), w_out, g_post.reshape(1, d))


def _layernorm(x, g, b):
    xc = x - jnp.mean(x, axis=-1, keepdims=True)
    y = xc * lax.rsqrt(jnp.mean(xc * xc, axis=-1, keepdims=True) + EPS)
    return y * g + b


def _cmix_prompt_kernel(cu_ref, cv_ref, lg_ref, lb_ref, ws_ref, bs_ref, o_ref):
    v = _layernorm(cv_ref[...], lg_ref[...], lb_ref[...]).astype(BF16)
    tril = _iota((C_CHUNK, C_CHUNK), 1) <= _iota((C_CHUNK, C_CHUNK), 0)
    for g in range(C_GROUPS):
        cols = slice(g * C_GDIM, (g + 1) * C_GDIM)
        w = jnp.where(tril, ws_ref[g], 0.0).astype(BF16)
        for c in range(cu_ref.shape[0] // C_CHUNK):
            rows = slice(c * C_CHUNK, (c + 1) * C_CHUNK)
            o_ref[rows, cols] = cu_ref[rows, cols] * (_dot(w, v[rows, cols]) + bs_ref[:, cols])


def cmix_prompt(h, ln_g, ln_b, ws, bs, tm=512):
    m = h.shape[0]
    bs_rows = jnp.repeat(bs.T, C_GDIM, axis=1)
    vec = pl.BlockSpec((1, C_WIDTH), lambda i: (0, 0))
    return pl.pallas_call(
        _cmix_prompt_kernel,
        grid=(m // tm,),
        in_specs=[pl.BlockSpec((tm, C_WIDTH), lambda i: (i, O_CU // C_WIDTH)),
                  pl.BlockSpec((tm, C_WIDTH), lambda i: (i, O_CV // C_WIDTH)),
                  vec, vec,
                  pl.BlockSpec(ws.shape, lambda i: (0, 0, 0)),
                  pl.BlockSpec(bs_rows.shape, lambda i: (0, 0))],
        out_specs=pl.BlockSpec((tm, C_WIDTH), lambda i: (i, 0)),
        out_shape=jax.ShapeDtypeStruct((m, C_WIDTH), F32),
        compiler_params=_cparams("parallel"),
        name="cmix_prompt",
    )(h, h, ln_g.reshape(1, -1), ln_b.reshape(1, -1), ws, bs_rows)


def _cmix_sample_kernel(cu_ref, cv_ref, lg_ref, lb_ref, w_ref, bs_ref, o_ref, v_ref):
    v = _layernorm(cv_ref[...], lg_ref[...], lb_ref[...])
    v_ref[...] = v
    t = v.shape[1]
    i = _iota((t, C_WIDTH), 0)
    acc = jnp.broadcast_to(bs_ref[...][None], v.shape)
    for j in range(t):
        wj = jnp.where(i >= j, w_ref[j], 0.0)
        acc = acc + wj[None] * v[:, j:j + 1, :]
    o_ref[...] = cu_ref[...] * acc


def cmix_sample(h3, ln_g, ln_b, ws, bs):
    n, t, _ = h3.shape
    w_cols = jnp.repeat(ws[:, :t, :t].transpose(2, 1, 0), C_GDIM, axis=2)
    bs_rows = jnp.repeat(bs[:, :t].T, C_GDIM, axis=1)
    vec = pl.BlockSpec((1, 1, C_WIDTH), lambda i: (0, 0, 0))
    shape = jax.ShapeDtypeStruct((n, t, C_WIDTH), F32)
    blk = pl.BlockSpec((n, t, C_WIDTH), lambda i: (0, 0, 0))
    return pl.pallas_call(
        _cmix_sample_kernel,
        grid=(1,),
        in_specs=[pl.BlockSpec((n, t, C_WIDTH), lambda i: (0, 0, O_CU // C_WIDTH)),
                  pl.BlockSpec((n, t, C_WIDTH), lambda i: (0, 0, O_CV // C_WIDTH)),
                  vec, vec,
                  pl.BlockSpec(w_cols.shape, lambda i: (0, 0, 0)),
                  pl.BlockSpec(bs_rows.shape, lambda i: (0, 0))],
        out_specs=[blk, blk],
        out_shape=[shape, shape],
        compiler_params=_cparams("arbitrary"),
        name="cmix_sample",
    )(h3, h3, ln_g.reshape(1, 1, -1), ln_b.reshape(1, 1, -1), w_cols, bs_rows)


def _top_blocks(scores, allowed, axis):
    nb = scores.shape[axis]
    bidx = _iota(scores.shape, axis).astype(F32)
    cur = jnp.where(allowed, scores, NEG)
    sel = jnp.zeros(scores.shape, F32)
    for _ in range(D_TOPK):
        mx = cur.max(axis, keepdims=True)
        first = jnp.where(cur == mx, bidx, float(nb)).min(axis, keepdims=True)
        pick = bidx == first
        sel = jnp.where(pick, jnp.where(allowed, 1.0, sel), sel)
        cur = jnp.where(pick, BELOW_NEG, cur)
    return sel


def _moba_prompt_kernel(q_ref, k_ref, v_ref, o_ref, km_s, kb_s, vt_s, bias_s):
    nb = kb_s.shape[0]
    g = D_HEADS // D_KV
    nq = g * D_BLOCK
    qb = pl.program_id(2)

    @pl.when(qb == 0)
    def _():
        for b in range(nb):
            rows = slice(b * D_BLOCK, (b + 1) * D_BLOCK)
            kblk = k_ref[rows, :]
            km_s[b:b + 1, :] = jnp.mean(kblk, axis=0, keepdims=True)
            kb_s[b] = kblk.astype(BF16)
            vt_s[b] = v_ref[rows, :].T.astype(BF16)

    qf = jnp.concatenate([q_ref[:, gi * HEAD_DIM:(gi + 1) * HEAD_DIM] for gi in range(g)], axis=0) * ATTN_SCALE
    q = qf.astype(BF16)
    scores = _dot_nt(km_s[...], qf, precision=lax.Precision.HIGHEST)
    sel = _top_blocks(scores, _iota((nb, nq), 0) < qb, axis=0)
    bias_s[...] = jnp.where(sel > 0.5, 0.0, NEG)

    causal = _iota((D_BLOCK, nq), 0) <= _mod(_iota((D_BLOCK, nq), 1), D_BLOCK)
    s = jnp.where(causal, _dot_nt(kb_s[qb], q), NEG)
    m = s.max(0, keepdims=True)
    p = jnp.exp(s - m)
    l = p.sum(0, keepdims=True)
    acc = _dot(vt_s[qb], p.astype(BF16))

    def body(j, carry):
        m, l, acc = carry
        s = _dot_nt(kb_s[j], q) + bias_s[pl.ds(j, 1), :]
        m_new = jnp.maximum(m, s.max(0, keepdims=True))
        alpha = jnp.exp(m - m_new)
        p = jnp.exp(s - m_new)
        return m_new, alpha * l + p.sum(0, keepdims=True), alpha * acc + _dot(vt_s[j], p.astype(BF16))

    m, l, acc = lax.fori_loop(0, qb, body, (m, l, acc))
    out = acc / l
    for gi in range(g):
        o_ref[:, gi * HEAD_DIM:(gi + 1) * HEAD_DIM] = out[:, gi * D_BLOCK:(gi + 1) * D_BLOCK].T


def moba_prompt(h3):
    n, seq, _ = h3.shape
    g = D_HEADS // D_KV
    gw = g * HEAD_DIM
    nb = seq // D_BLOCK
    return pl.pallas_call(
        _moba_prompt_kernel,
        grid=(n, D_KV, nb),
        in_specs=[pl.BlockSpec((None, D_BLOCK, gw), lambda b, hh, i: (b, i, O_DQ // gw + hh)),
                  pl.BlockSpec((None, seq, HEAD_DIM), lambda b, hh, i: (b, 0, O_DK // HEAD_DIM + hh)),
                  pl.BlockSpec((None, seq, HEAD_DIM), lambda b, hh, i: (b, 0, O_DV // HEAD_DIM + hh))],
        out_specs=pl.BlockSpec((None, D_BLOCK, gw), lambda b, hh, i: (b, i, hh)),
        out_shape=jax.ShapeDtypeStruct((n, seq, D_WIDTH), F32),
        scratch_shapes=[pltpu.VMEM((nb, HEAD_DIM), F32),
                        pltpu.VMEM((nb, D_BLOCK, HEAD_DIM), BF16),
                        pltpu.VMEM((nb, HEAD_DIM, D_BLOCK), BF16),
                        pltpu.VMEM((nb, g * D_BLOCK), F32)],
        compiler_params=_cparams("parallel", "parallel", "arbitrary"),
        name="moba_prompt",
    )(h3, h3, h3)


KV_PAGES_BLOCK = 4


def _kv_pages_kernel(k_ref, v_ref, o_ref):
    for pg in range(o_ref.shape[0]):
        rows = slice(pg * PAGE_SIZE, (pg + 1) * PAGE_SIZE)
        for kv, src in enumerate((k_ref, v_ref)):
            for hh in range(D_KV):
                o_ref[pg, kv, hh] = src[rows, hh * HEAD_DIM:(hh + 1) * HEAD_DIM]


def kv_pages(h3):
    n, seq, _ = h3.shape
    kw = D_KV * HEAD_DIM
    pb = KV_PAGES_BLOCK
    return pl.pallas_call(
        _kv_pages_kernel,
        grid=(n, seq // (pb * PAGE_SIZE)),
        in_specs=[pl.BlockSpec((None, pb * PAGE_SIZE, kw), lambda b, p: (b, p, O_DK // kw)),
                  pl.BlockSpec((None, pb * PAGE_SIZE, kw), lambda b, p: (b, p, O_DV // kw))],
        out_specs=pl.BlockSpec((None, pb, 2, D_KV, PAGE_SIZE, HEAD_DIM), lambda b, p: (b, p, 0, 0, 0, 0)),
        out_shape=jax.ShapeDtypeStruct((n, seq // PAGE_SIZE, 2, D_KV, PAGE_SIZE, HEAD_DIM), F32),
        compiler_params=_cparams("parallel", "parallel"),
        name="kv_pages",
    )(h3, h3)


MOBA_CHUNK_PAGES = 16
MOBA_VMEM_LIMIT = 52 * 1024 * 1024


def _moba_decode_kernel(pt_ref, q_ref, kn_ref, vn_ref, pool_ref, o_ref, buf, sem, lg_s, km_s, p_s, bias_s, e_s):
    seq = pl.program_id(0)
    n_seq = pl.num_programs(0)
    n_pages = buf.shape[1]
    cp = MOBA_CHUNK_PAGES
    ppb = D_BLOCK // PAGE_SIZE
    n_blk = n_pages // ppb
    g = D_HEADS // D_KV
    nq = q_ref.shape[1]
    lane_blk = (nq, PAGE_SIZE)

    def page_copies(s, slot):
        return [pltpu.make_async_copy(pool_ref.at[pt_ref[s, pg]], buf.at[slot, pg], sem.at[slot])
                for pg in range(n_pages)]

    @pl.when(seq == 0)
    def _():
        for c in page_copies(0, 0):
            c.start()
        e_s[...] = jnp.where(_div(_iota(e_s.shape, 1), PAGE_SIZE) == _iota(e_s.shape, 0), 1.0, 0.0).astype(BF16)

    @pl.when(seq + 1 < n_seq)
    def _():
        for c in page_copies(seq + 1, (seq + 1) % 2):
            c.start()

    slot = seq % 2
    for c in page_copies(seq, slot):
        c.wait()

    for hk in range(D_KV):
        qf = q_ref[hk] * ATTN_SCALE
        q = qf.astype(BF16)
        for c in range(n_pages // cp):
            ks = [buf[slot, c * cp + i, 0, hk] for i in range(cp)]
            lg_s[hk, :, c * cp * PAGE_SIZE:(c + 1) * cp * PAGE_SIZE] = _dot_nt(
                q, jnp.concatenate([k.astype(BF16) for k in ks], axis=0))
            sums = [k.sum(axis=0, keepdims=True) for k in ks]
            means = [functools.reduce(lambda a, b: a + b, sums[b * ppb:(b + 1) * ppb]) * (1.0 / D_BLOCK)
                     for b in range(cp // ppb)]
            km_s[hk, c * (cp // ppb):(c + 1) * (cp // ppb), :] = jnp.concatenate(means, axis=0)

        scores = _dot_nt(qf, km_s[hk], precision=lax.Precision.HIGHEST)
        sel = _top_blocks(scores, _iota(scores.shape, 1) >= 0, axis=1)
        bias_s[...] = jnp.where(_dot(sel.astype(BF16), e_s[...]) > 0.5, 0.0, NEG)

        own = _dot_nt(q, kn_ref[hk].astype(BF16))
        own = jnp.where(_iota(own.shape, 1) <= _div(_iota(own.shape, 0), g), own, NEG)

        def masked(pg):
            b = pg // ppb
            return lg_s[hk, :, pg * PAGE_SIZE:(pg + 1) * PAGE_SIZE] + bias_s[:, b * PAGE_SIZE:(b + 1) * PAGE_SIZE]

        mx = masked(0)
        for pg in range(1, n_pages):
            mx = jnp.maximum(mx, masked(pg))
        m = jnp.maximum(mx.max(-1, keepdims=True), own.max(-1, keepdims=True))
        m_rep = jnp.broadcast_to(m, lane_blk)
        tot = jnp.zeros(lane_blk, F32)
        for pg in range(n_pages):
            p = jnp.exp(masked(pg) - m_rep)
            tot = tot + p
            p_s[hk, :, pg * PAGE_SIZE:(pg + 1) * PAGE_SIZE] = p.astype(BF16)
        po = jnp.exp(own - m)
        den = tot.sum(-1, keepdims=True) + po.sum(-1, keepdims=True)

        acc = _dot(po.astype(BF16), vn_ref[hk].astype(BF16))
        for c in range(n_pages // cp):
            v = jnp.concatenate([buf[slot, c * cp + i, 1, hk].astype(BF16) for i in range(cp)], axis=0)
            acc = acc + _dot(p_s[hk, :, c * cp * PAGE_SIZE:(c + 1) * cp * PAGE_SIZE], v)
        o_ref[hk] = acc / den


def moba_decode(pool, page_table, q, k_new, v_new):
    n, n_pages = page_table.shape
    assert (n_pages * PAGE_SIZE) % D_BLOCK == 0 and n_pages % MOBA_CHUNK_PAGES == 0
    nq, t = q.shape[2], k_new.shape[2]
    n_blk = n_pages * PAGE_SIZE // D_BLOCK
    keys = n_pages * PAGE_SIZE

    def per_seq(r):
        return pl.BlockSpec((None, D_KV, r, HEAD_DIM), lambda b, pt: (b, 0, 0, 0))

    grid_spec = pltpu.PrefetchScalarGridSpec(
        num_scalar_prefetch=1,
        grid=(n,),
        in_specs=[per_seq(nq), per_seq(t), per_seq(t), pl.BlockSpec(memory_space=pl.ANY)],
        out_specs=per_seq(nq),
        scratch_shapes=[pltpu.VMEM((2, n_pages) + pool.shape[1:], F32),
                        pltpu.SemaphoreType.DMA((2,)),
                        pltpu.VMEM((D_KV, nq, keys), F32),
                        pltpu.VMEM((D_KV, n_blk, HEAD_DIM), F32),
                        pltpu.VMEM((D_KV, nq, keys), BF16),
                        pltpu.VMEM((nq, n_blk * PAGE_SIZE), F32),
                        pltpu.VMEM((n_blk, n_blk * PAGE_SIZE), BF16)])
    return pl.pallas_call(
        _moba_decode_kernel,
        grid_spec=grid_spec,
        out_shape=jax.ShapeDtypeStruct((n, D_KV, nq, HEAD_DIM), F32),
        compiler_params=pltpu.CompilerParams(dimension_semantics=("arbitrary",), vmem_limit_bytes=MOBA_VMEM_LIMIT),
        name="moba_decode",
    )(page_table, q, k_new, v_new, pool)


def even_layer_prompt(x, w_in, w_out, g_pre, g_post, sink, jobs_a=(), jobs_b=()):
    n, s, d = x.shape
    x2 = x.reshape(n * s, d)
    h = norm_proj(x2, g_pre, w_in, tm=1024)
    h3 = h.reshape(n, s, IN_EVEN)
    a_out, upd_a = attn_a_prompt(h3, jobs_a)
    b_out, upd_b = attn_b_prompt(h3, sink, jobs_b)
    y = out_proj(x2, [a_out.reshape(n * s, A_WIDTH), b_out.reshape(n * s, B_WIDTH)], h,
                 [E_AG, E_BG, E_BG + GATE_W], w_out, g_post)
    rows = [kv_rows(h3, E_AK + gi * A_WIDTH, E_AV + gi * A_WIDTH, A_HEADS, win) for gi, (win, _) in enumerate(A_GROUPS)]
    rows.append(kv_rows(h3, E_BK, E_BV, B_KV, B_WINDOW))
    return y.reshape(n, s, d), rows, upd_a, upd_b


def even_layer_sample(x, flats, w_in, w_out, g_pre, g_post, sink):
    n, t, d = x.shape
    x2 = x.reshape(n * t, d)
    h = norm_proj(x2, g_pre, w_in, tm=n * t)
    stats, fresh = [], []
    for gi, (_, dil) in enumerate(A_GROUPS):
        cols = slice(gi * A_WIDTH, (gi + 1) * A_WIDTH)
        q = h[:, E_AQ:E_AK][:, cols].reshape(n, t * A_HEADS, HEAD_DIM)
        k = h[:, E_AK:E_AV][:, cols].reshape(n, t * A_HEADS, HEAD_DIM)
        v = h[:, E_AV:E_AG][:, cols].reshape(n, t * A_HEADS, HEAD_DIM)
        stats.append(decode_attn(flats[gi], q, k, v, A_HEADS, A_HEADS, dil))
        fresh.append(new_cache_rows(k, v, A_HEADS))
    a_out = merge_groups(stats).reshape(n * t, A_WIDTH)
    q = h[:, E_BQ:E_BK].reshape(n, t * B_HEADS, HEAD_DIM)
    k = h[:, E_BK:E_BV].reshape(n, t * B_KV, HEAD_DIM)
    v = h[:, E_BV:E_BG].reshape(n, t * B_KV, HEAD_DIM)
    sink_rows = jnp.broadcast_to(jnp.tile(sink, t)[:, None], (t * B_HEADS, HEAD_DIM))
    (b_out,) = decode_attn(flats[A_NG], q, k, v, B_KV, B_HEADS, 1, sink=sink_rows)
    fresh.append(new_cache_rows(k, v, B_KV))
    y = out_proj(x2, [a_out, b_out.reshape(n * t, B_WIDTH)], h, [E_AG, E_BG, E_BG + GATE_W], w_out, g_post)
    return y.reshape(n, t, d), fresh


def odd_layer_prompt(x, w_in, w_out, g_pre, g_post, ln_g, ln_b, ws, bs):
    n, s, d = x.shape
    x2 = x.reshape(n * s, d)
    h = norm_proj(x2, g_pre, w_in, tm=1024)
    h3 = h.reshape(n, s, IN_ODD)
    c_out = cmix_prompt(h, ln_g, ln_b, ws, bs)
    d_out = moba_prompt(h3).reshape(n * s, D_WIDTH)
    gates = [O_CG, O_CG + GATE_W, O_DG, O_DG + GATE_W]
    y = out_proj(x2, [c_out, d_out], h, gates, w_out, g_post)
    return y.reshape(n, s, d), kv_pages(h3)


def odd_layer_sample(x, pool, page_table, w_in, w_out, g_pre, g_post, ln_g, ln_b, ws, bs):
    n, t, d = x.shape
    g = D_HEADS // D_KV
    x2 = x.reshape(n * t, d)
    h = norm_proj(x2, g_pre, w_in, tm=n * t)
    c_out, v_rows = cmix_sample(h.reshape(n, t, IN_ODD), ln_g, ln_b, ws, bs)
    q = h[:, O_DQ:O_DK].reshape(n, t, D_KV, g, HEAD_DIM).transpose(0, 2, 1, 3, 4).reshape(n, D_KV, t * g, HEAD_DIM)
    k = h[:, O_DK:O_DV].reshape(n, t, D_KV, HEAD_DIM).transpose(0, 2, 1, 3)
    v = h[:, O_DV:O_DG].reshape(n, t, D_KV, HEAD_DIM).transpose(0, 2, 1, 3)
    o = moba_decode(pool, page_table, q, k, v)
    d_out = o.reshape(n, D_KV, t, g, HEAD_DIM).transpose(0, 2, 1, 3, 4).reshape(n * t, D_WIDTH)
    gates = [O_CG, O_CG + GATE_W, O_DG, O_DG + GATE_W]
    y = out_proj(x2, [c_out.reshape(n * t, C_WIDTH), d_out], h, gates, w_out, g_post)
    return y.reshape(n, t, d), jnp.stack([k, v], axis=1), v_rows


def kernel(x_prompt, x_sample, cache_a1, cache_a2, cache_a3, cache_b, cache_d, page_table, norm_pre_even, norm_post_even, w_in_even, w_out_even, sink_b, norm_pre_odd, norm_post_odd, w_in_odd, w_out_odd, c_ln_g, c_ln_b, c_ws, c_bs):
    yp, ys = x_prompt, x_sample
    caches = (cache_a1, cache_a2, cache_a3, cache_b)
    big = max(range(len(caches)), key=lambda c: caches[c].size)
    a_p, a_s = [[] for _ in range(A_NG)], [[] for _ in range(A_NG)]
    b_p, b_s, d_p, d_s, c_s = [], [], [], [], []
    for layer in range(DEPTH):
        li = layer // 2
        if layer % 2 == 0:
            w_in, w_out = w_in_even[li].astype(BF16), w_out_even[li].astype(BF16)
            n_seq = ys.shape[0]
            flats = [c[li].reshape(n_seq, -1, HEAD_DIM) for c in caches]
            ys, fresh = even_layer_sample(ys, flats, w_in, w_out, norm_pre_even[li], norm_post_even[li], sink_b[li])
            small = [c for c in range(len(caches)) if c != big]
            yp, rows, upd_a, upd_b = even_layer_prompt(
                yp, w_in, w_out, norm_pre_even[li], norm_post_even[li], sink_b[li],
                [(flats[big], fresh[big])], [(flats[c], fresh[c]) for c in small])
            bufs = dict(zip([big] + small, upd_a + upd_b))
            for gi in range(A_NG):
                a_p[gi].append(rows[gi])
                a_s[gi].append(bufs[gi].reshape(caches[gi][li].shape))
            b_p.append(rows[A_NG])
            b_s.append(bufs[A_NG].reshape(cache_b[li].shape))
        else:
            w_in, w_out = w_in_odd[li].astype(BF16), w_out_odd[li].astype(BF16)
            yp, kvp = odd_layer_prompt(yp, w_in, w_out, norm_pre_odd[li], norm_post_odd[li],
                                       c_ln_g[li], c_ln_b[li], c_ws[li], c_bs[li])
            ys, kvs, vrows = odd_layer_sample(ys, cache_d[li], page_table, w_in, w_out, norm_pre_odd[li],
                                              norm_post_odd[li], c_ln_g[li], c_ln_b[li], c_ws[li], c_bs[li])
            d_p.append(kvp)
            d_s.append(kvs)
            c_s.append(vrows)
    return (yp, ys, jnp.stack(a_p[0]), jnp.stack(a_s[0]), jnp.stack(a_p[1]), jnp.stack(a_s[1]),
            jnp.stack(a_p[2]), jnp.stack(a_s[2]), jnp.stack(b_p), jnp.stack(b_s),
            jnp.stack(d_p), jnp.stack(d_s), jnp.stack(c_s))
```

```python
import functools

import jax
import jax.numpy as jnp
from jax import lax
from jax.experimental import pallas as pl
from jax.experimental.pallas import tpu as pltpu

F32 = jnp.float32
BF16 = jnp.bfloat16

D_MODEL = 2048
DEPTH = 2
DEC_SEQ = 8
PAST_LEN = 8192
PAGE_SIZE = 128
HEAD_DIM = 128
ATTN_SCALE = HEAD_DIM ** -0.5
A_GROUPS = ((128, 1), (512, 4), (2048, 16))
A_NG = len(A_GROUPS)
A_HEADS = 4
A_QKV = A_NG * A_HEADS * HEAD_DIM
A_WIDTH = A_HEADS * HEAD_DIM
B_HEADS = 8
B_KV = 2
B_WINDOW = 128
B_WIDTH = B_HEADS * HEAD_DIM
C_CHUNK = 128
C_GROUPS = 8
C_GDIM = 128
C_WIDTH = C_GROUPS * C_GDIM
D_HEADS = 8
D_KV = 2
D_BLOCK = 256
D_TOPK = 3
D_WIDTH = D_HEADS * HEAD_DIM
EPS = 1e-6
NEG = -1e30
BELOW_NEG = -3e38

E_AQ, E_AK, E_AV = 0, A_QKV, 2 * A_QKV
E_AG = 3 * A_QKV
E_BQ = E_AG + A_WIDTH
E_BK = E_BQ + B_WIDTH
E_BV = E_BK + B_KV * HEAD_DIM
E_BG = E_BV + B_KV * HEAD_DIM
IN_EVEN = E_BG + B_WIDTH
O_CU, O_CV, O_CG = 0, C_WIDTH, 2 * C_WIDTH
O_DQ = 3 * C_WIDTH
O_DK = O_DQ + D_WIDTH
O_DV = O_DK + D_KV * HEAD_DIM
O_DG = O_DV + D_KV * HEAD_DIM
IN_ODD = O_DG + D_WIDTH

BAND = 128
GATE_W = 512
VMEM_LIMIT = 48 * 1024 * 1024


def _cparams(*sem):
    return pltpu.CompilerParams(dimension_semantics=sem, vmem_limit_bytes=VMEM_LIMIT)


def _dot_nt(a, b, precision=None):
    return lax.dot_general(a, b, (((1,), (1,)), ((), ())), preferred_element_type=F32, precision=precision)


def _dot(a, b):
    return jnp.dot(a, b, preferred_element_type=F32)


def _iota(shape, dim):
    return lax.broadcasted_iota(jnp.int32, shape, dim)


def _div(x, d):
    assert d & (d - 1) == 0
    return x >> (d.bit_length() - 1)


def _mod(x, d):
    assert d & (d - 1) == 0
    return x & (d - 1)


def _norm_proj_kernel(x_ref, g_ref, w_ref, o_ref, xn_ref):
    @pl.when(pl.program_id(1) == 0)
    def _():
        x = x_ref[...]
        r = lax.rsqrt(jnp.mean(x * x, axis=-1, keepdims=True) + EPS)
        xn_ref[...] = (x * r * g_ref[...]).astype(BF16)

    o_ref[...] = _dot(xn_ref[...], w_ref[...])


def norm_proj(x, g, w, tm, tn=512):
    m, d = x.shape
    n = w.shape[1]
    return pl.pallas_call(
        _norm_proj_kernel,
        grid=(m // tm, n // tn),
        in_specs=[pl.BlockSpec((tm, d), lambda i, j: (i, 0)),
                  pl.BlockSpec((1, d), lambda i, j: (0, 0)),
                  pl.BlockSpec((d, tn), lambda i, j: (0, j))],
        out_specs=pl.BlockSpec((tm, tn), lambda i, j: (i, j)),
        out_shape=jax.ShapeDtypeStruct((m, n), F32),
        scratch_shapes=[pltpu.VMEM((tm, d), BF16)],
        compiler_params=_cparams("parallel", "arbitrary"),
        name="norm_proj",
    )(x, g.reshape(1, d), w)


def _band_block(q, kc, vc, kp, vp, prev_floor=0):
    r = q.shape[0]
    i = _mod(_iota((r, BAND), 0), BAND)
    j = _iota((r, BAND), 1)
    sc = jnp.where(j <= i, _dot_nt(q, kc), NEG)
    m = sc.max(-1, keepdims=True)
    if kp is not None:
        sp = jnp.where(j >= i + prev_floor, _dot_nt(q, kp), NEG)
        m = jnp.maximum(m, sp.max(-1, keepdims=True))
    pc = jnp.exp(sc - m)
    l = pc.sum(-1, keepdims=True)
    o = _dot(pc.astype(BF16), vc)
    if kp is not None:
        pp = jnp.exp(sp - m)
        l = l + pp.sum(-1, keepdims=True)
        o = o + _dot(pp.astype(BF16), vp)
    return o, m, l


def _rows(ref, start, stride):
    if stride == 1:
        return ref[pl.ds(start, BAND), :]
    return ref[pl.ds(start, BAND, stride=stride), :]


def _attn_a_prompt_kernel(q1, k1, v1, q2, k2, v2, q3, k3, v3, o_ref, o1_s, m1_s, l1_s, o2_s, m2_s, l2_s):
    seq = q1.shape[0]
    wide = (BAND, HEAD_DIM)

    def load(qr, kr, vr, start, stride, prev_start):
        q = (_rows(qr, start, stride) * ATTN_SCALE).astype(BF16)
        kc = _rows(kr, start, stride).astype(BF16)
        vc = _rows(vr, start, stride).astype(BF16)
        if prev_start is None:
            return q, kc, vc, None, None
        return q, kc, vc, _rows(kr, prev_start, stride).astype(BF16), _rows(vr, prev_start, stride).astype(BF16)

    def g1_body(b, carry):
        start = pl.multiple_of(b * BAND, BAND)
        prev = pl.multiple_of(jnp.maximum(b - 1, 0) * BAND, BAND)
        o, m, l = _band_block(*load(q1, k1, v1, start, 1, prev), prev_floor=jnp.where(b > 0, 0, BAND))
        o1_s[pl.ds(start, BAND), :] = o
        m1_s[pl.ds(start, BAND), :] = jnp.broadcast_to(m, wide)
        l1_s[pl.ds(start, BAND), :] = jnp.broadcast_to(l, wide)
        return carry

    lax.fori_loop(0, seq // BAND, g1_body, 0)

    d2 = A_GROUPS[1][1]
    sub2 = seq // d2
    for r in range(d2):
        for b in range(sub2 // BAND):
            start = r + b * BAND * d2
            prev = None if b == 0 else start - BAND * d2
            o, m, l = _band_block(*load(q2, k2, v2, start, d2, prev))
            o2_s[r, b * BAND:(b + 1) * BAND, :] = o
            m2_s[r, b * BAND:(b + 1) * BAND, :] = jnp.broadcast_to(m, wide)
            l2_s[r, b * BAND:(b + 1) * BAND, :] = jnp.broadcast_to(l, wide)

    d3 = A_GROUPS[2][1]
    ratio = d3 // d2
    for r in range(d3):
        o3, m3, l3 = _band_block(*load(q3, k3, v3, r, d3, None))
        m3 = jnp.broadcast_to(m3, wide)
        l3 = jnp.broadcast_to(l3, wide)
        o1 = o1_s[pl.ds(r, BAND, stride=d3), :]
        m1 = m1_s[pl.ds(r, BAND, stride=d3), :]
        l1 = l1_s[pl.ds(r, BAND, stride=d3), :]
        r2, off2 = r % d2, r // d2
        o2 = o2_s[r2, pl.ds(off2, BAND, stride=ratio), :]
        m2 = m2_s[r2, pl.ds(off2, BAND, stride=ratio), :]
        l2 = l2_s[r2, pl.ds(off2, BAND, stride=ratio), :]
        mmax = jnp.maximum(jnp.maximum(m1, m2), m3)
        e1, e2, e3 = jnp.exp(m1 - mmax), jnp.exp(m2 - mmax), jnp.exp(m3 - mmax)
        num = o1 * e1 + o2 * e2 + o3 * e3
        den = l1 * e1 + l2 * e2 + l3 * e3
        o_ref[pl.ds(r, BAND, stride=d3), :] = num / den


def attn_a_prompt(h3):
    n, seq, _ = h3.shape
    assert seq == A_GROUPS[2][0] and all(w // d == BAND for w, d in A_GROUPS)

    def spec(off, gi):
        blk = off // HEAD_DIM + gi * A_HEADS
        return pl.BlockSpec((None, seq, HEAD_DIM), lambda b, hh: (b, 0, blk + hh))

    in_specs = []
    for gi in range(A_NG):
        in_specs += [spec(E_AQ, gi), spec(E_AK, gi), spec(E_AV, gi)]
    d2 = A_GROUPS[1][1]
    stat = pltpu.VMEM((seq, HEAD_DIM), F32)
    stat2 = pltpu.VMEM((d2, seq // d2, HEAD_DIM), F32)
    return pl.pallas_call(
        _attn_a_prompt_kernel,
        grid=(n, A_HEADS),
        in_specs=in_specs,
        out_specs=pl.BlockSpec((None, seq, HEAD_DIM), lambda b, hh: (b, 0, hh)),
        out_shape=jax.ShapeDtypeStruct((n, seq, A_WIDTH), F32),
        scratch_shapes=[stat, stat, stat, stat2, stat2, stat2],
        compiler_params=_cparams("parallel", "parallel"),
        name="attn_a_prompt",
    )(*([h3] * 9))


def _attn_b_prompt_kernel(sink_ref, q_ref, k_ref, v_ref, o_ref):
    seq = q_ref.shape[0]
    g = B_HEADS // B_KV
    kvh = pl.program_id(1)
    sink = jnp.concatenate([jnp.full((BAND, 1), sink_ref[kvh * g + gi], F32) for gi in range(g)], axis=0)

    def body(b, carry):
        start = pl.multiple_of(b * BAND, BAND)
        prev = pl.multiple_of(jnp.maximum(b - 1, 0) * BAND, BAND)
        qb = q_ref[pl.ds(start, BAND), :]
        q = jnp.concatenate([qb[:, gi * HEAD_DIM:(gi + 1) * HEAD_DIM] for gi in range(g)], axis=0)
        q = (q * ATTN_SCALE).astype(BF16)
        kc = k_ref[pl.ds(start, BAND), :].astype(BF16)
        vc = v_ref[pl.ds(start, BAND), :].astype(BF16)
        kp = k_ref[pl.ds(prev, BAND), :].astype(BF16)
        vp = v_ref[pl.ds(prev, BAND), :].astype(BF16)
        o, m, l = _band_block(q, kc, vc, kp, vp, prev_floor=jnp.where(b > 0, 0, BAND))
        m2 = jnp.maximum(m, sink)
        a = jnp.exp(m - m2)
        out = o * (a / (l * a + jnp.exp(sink - m2)))
        for gi in range(g):
            o_ref[pl.ds(start, BAND), gi * HEAD_DIM:(gi + 1) * HEAD_DIM] = out[gi * BAND:(gi + 1) * BAND]
        return carry

    lax.fori_loop(0, seq // BAND, body, 0)


def attn_b_prompt(h3, sink):
    n, seq, _ = h3.shape
    gw = (B_HEADS // B_KV) * HEAD_DIM
    return pl.pallas_call(
        _attn_b_prompt_kernel,
        grid=(n, B_KV),
        in_specs=[pl.BlockSpec(memory_space=pltpu.SMEM),
                  pl.BlockSpec((None, seq, gw), lambda b, hh: (b, 0, E_BQ // gw + hh)),
                  pl.BlockSpec((None, seq, HEAD_DIM), lambda b, hh: (b, 0, E_BK // HEAD_DIM + hh)),
                  pl.BlockSpec((None, seq, HEAD_DIM), lambda b, hh: (b, 0, E_BV // HEAD_DIM + hh))],
        out_specs=pl.BlockSpec((None, seq, gw), lambda b, hh: (b, 0, hh)),
        out_shape=jax.ShapeDtypeStruct((n, seq, B_WIDTH), F32),
        compiler_params=_cparams("parallel", "parallel"),
        name="attn_b_prompt",
    )(sink, h3, h3, h3)


KV_ROWS_BLOCK = 512


def _kv_rows_kernel(k_ref, v_ref, o_ref, *, heads):
    npos = k_ref.shape[0]
    for kv, src in enumerate((k_ref, v_ref)):
        for hh in range(heads):
            o_ref[pl.ds(kv * heads + hh, npos, stride=2 * heads), :] = src[:, hh * HEAD_DIM:(hh + 1) * HEAD_DIM]


def kv_rows(h3, k_off, v_off, heads, window):
    n, seq, _ = h3.shape
    w = min(window, seq)
    hw = heads * HEAD_DIM
    npos = min(w, KV_ROWS_BLOCK)
    first = (seq - w) // npos
    rows = npos * 2 * heads
    out = pl.pallas_call(
        functools.partial(_kv_rows_kernel, heads=heads),
        grid=(n, w // npos),
        in_specs=[pl.BlockSpec((None, npos, hw), lambda b, c: (b, first + c, k_off // hw)),
                  pl.BlockSpec((None, npos, hw), lambda b, c: (b, first + c, v_off // hw))],
        out_specs=pl.BlockSpec((None, rows, HEAD_DIM), lambda b, c: (b, c, 0)),
        out_shape=jax.ShapeDtypeStruct((n, w * 2 * heads, HEAD_DIM), F32),
        compiler_params=_cparams("parallel", "parallel"),
        name="kv_rows",
    )(h3, h3)
    return out.reshape(n, w, 2, heads, HEAD_DIM)


DEC_CHUNK = 2048


def _decode_kernel(*refs, heads, q_heads, dil, use_sink):
    if use_sink:
        sink_ref, q_ref, kn_ref, vn_ref, nf_ref, c_ref, nc_ref, out_ref, bias_s, biasn_s, l_s = refs
    else:
        q_ref, kn_ref, vn_ref, nf_ref, c_ref, nc_ref, out_ref, m_ref, l_ref, bias_s, biasn_s, l_s = refs
    slots = 2 * heads
    seqs, rows = c_ref.shape[:2]
    lb = rows // slots
    nq = q_ref.shape[1]
    t_new = kn_ref.shape[1] // heads
    per_kv = q_heads // heads
    chunk = min(DEC_CHUNK, rows)
    fresh = nf_ref.shape[1]

    @pl.when(pl.program_id(0) == 0)
    def _():
        row = _iota((nq, rows), 0)
        col = _iota((nq, rows), 1)
        t = _div(row, q_heads)
        kvh = _div(_mod(row, q_heads), per_kv)
        pos = _div(col, slots)
        ok = (_mod(col, slots) == kvh) & (pos >= t) & (_mod(lb + t - pos, dil) == 0)
        bias_s[...] = jnp.where(ok, 0.0, NEG)
        rown = _iota((nq, t_new * heads), 0)
        coln = _iota((nq, t_new * heads), 1)
        tn = _div(rown, q_heads)
        kvhn = _div(_mod(rown, q_heads), per_kv)
        tk = _div(coln, heads)
        okn = (_mod(coln, heads) == kvhn) & (tk <= tn) & (_mod(tn - tk, dil) == 0)
        biasn_s[...] = jnp.where(okn, 0.0, NEG)

    def one_seq(b, carry):
        q = (q_ref[b] * ATTN_SCALE).astype(BF16)
        sn = _dot_nt(q, kn_ref[b].astype(BF16)) + biasn_s[...]
        m = sn.max(-1, keepdims=True)
        for c in range(rows // chunk):
            sl = slice(c * chunk, (c + 1) * chunk)
            s = _dot_nt(q, c_ref[b, sl, :].astype(BF16)) + bias_s[:, sl]
            l_s[:, sl] = s
            m = jnp.maximum(m, s.max(-1, keepdims=True))
        pn = jnp.exp(sn - m)
        den = pn.sum(-1, keepdims=True)
        o = _dot(pn.astype(BF16), vn_ref[b].astype(BF16))
        for c in range(rows // chunk):
            sl = slice(c * chunk, (c + 1) * chunk)
            p = jnp.exp(l_s[:, sl] - m)
            den = den + p.sum(-1, keepdims=True)
            o = o + _dot(pltpu.roll(p, heads, axis=1).astype(BF16), c_ref[b, sl, :].astype(BF16))

        if use_sink:
            sink = sink_ref[...]
            m2 = jnp.maximum(m, sink)
            a = jnp.exp(m - m2)
            out_ref[b] = o * (a / (den * a + jnp.exp(sink - m2)))
        else:
            out_ref[b] = o
            m_ref[b] = jnp.broadcast_to(m, (nq, HEAD_DIM))
            l_ref[b] = jnp.broadcast_to(den, (nq, HEAD_DIM))

        nc_ref[b, pl.ds(0, rows - fresh), :] = c_ref[b, pl.ds(fresh, rows - fresh), :]
        nc_ref[b, pl.ds(rows - fresh, fresh), :] = nf_ref[b]
        return carry

    if seqs == 1:
        one_seq(0, 0)
    else:
        lax.fori_loop(0, seqs, one_seq, 0)


DEC_BLOCK_BYTES = 4 * 1024 * 1024


def decode_attn(cache, q, k_new, v_new, heads, q_heads, dil, sink=None):
    n, lb = cache.shape[:2]
    rows = lb * 2 * heads
    flat = cache.reshape(n, rows, HEAD_DIM)
    k4 = k_new.reshape(n, -1, 1, heads, HEAD_DIM)
    v4 = v_new.reshape(n, -1, 1, heads, HEAD_DIM)
    new_flat = jnp.concatenate([k4, v4], axis=2).reshape(n, -1, HEAD_DIM)
    seqs = 1
    while seqs * 2 * rows * HEAD_DIM * 4 <= DEC_BLOCK_BYTES and n % (seqs * 2) == 0:
        seqs *= 2
    nq = q.shape[1]
    tk = k_new.shape[1]
    use_sink = sink is not None

    def per_seq(r):
        return pl.BlockSpec((seqs, r, HEAD_DIM), lambda b: (b, 0, 0))

    in_specs = [per_seq(nq), per_seq(tk), per_seq(tk), per_seq(new_flat.shape[1]), per_seq(rows)]
    args = [q, k_new, v_new, new_flat, flat]
    stat = jax.ShapeDtypeStruct((n, nq, HEAD_DIM), F32)
    out_shape = [jax.ShapeDtypeStruct(flat.shape, F32), stat]
    out_specs = [per_seq(rows), per_seq(nq)]
    if use_sink:
        in_specs = [pl.BlockSpec((nq, HEAD_DIM), lambda b: (0, 0))] + in_specs
        args = [sink] + args
    else:
        out_shape += [stat, stat]
        out_specs += [per_seq(nq), per_seq(nq)]
    outs = pl.pallas_call(
        functools.partial(_decode_kernel, heads=heads, q_heads=q_heads, dil=dil, use_sink=use_sink),
        grid=(n // seqs,),
        in_specs=in_specs,
        out_specs=out_specs,
        out_shape=out_shape,
        scratch_shapes=[pltpu.VMEM((nq, rows), F32), pltpu.VMEM((nq, tk), F32), pltpu.VMEM((nq, rows), F32)],
        compiler_params=_cparams("arbitrary"),
        name="decode_attn",
    )(*args)
    return (outs[0].reshape(cache.shape),) + tuple(outs[1:])


def _merge_kernel(*refs):
    o_ref = refs[-1]
    stats = [refs[3 * g:3 * g + 3] for g in range(A_NG)]
    mmax = functools.reduce(jnp.maximum, [m[...] for _, m, _ in stats])
    num = 0.0
    den = 0.0
    for o, m, l in stats:
        e = jnp.exp(m[...] - mmax)
        num = num + o[...] * e
        den = den + l[...] * e
    o_ref[...] = num / den


def merge_groups(stats):
    flat = [a.reshape(-1, HEAD_DIM) for st in stats for a in st]
    out = pl.pallas_call(_merge_kernel, out_shape=jax.ShapeDtypeStruct(flat[0].shape, F32),
                         compiler_params=pltpu.CompilerParams(vmem_limit_bytes=VMEM_LIMIT),
                         name="merge_groups")(*flat)
    return out


def _out_proj_kernel(*refs, pieces):
    x_ref = refs[0]
    o_refs = refs[1:1 + pieces]
    g_refs = refs[1 + pieces:1 + 2 * pieces]
    w_ref, gp_ref, y_ref = refs[1 + 2 * pieces:]
    acc = None
    for p in range(pieces):
        g = g_refs[p][...]
        mix = (o_refs[p][...] * (g * (1.0 / (1.0 + jnp.exp(-g))))).astype(BF16)
        d = _dot(mix, w_ref[p * GATE_W:(p + 1) * GATE_W, :])
        acc = d if acc is None else acc + d
    r = lax.rsqrt(jnp.mean(acc * acc, axis=-1, keepdims=True) + EPS)
    y_ref[...] = x_ref[...] + acc * r * gp_ref[...]


def out_proj(x, outs, h, gate_offs, w_out, g_post, tm=512):
    m, d = x.shape
    tm = min(tm, m)
    o_args, o_specs = [], []
    for arr in outs:
        for c in range(arr.shape[1] // GATE_W):
            o_args.append(arr)
            o_specs.append(pl.BlockSpec((tm, GATE_W), lambda i, c=c: (i, c)))
    g_specs = [pl.BlockSpec((tm, GATE_W), lambda i, c=off // GATE_W: (i, c)) for off in gate_offs]
    pieces = len(o_args)
    assert pieces == len(gate_offs) and pieces * GATE_W == w_out.shape[0]
    return pl.pallas_call(
        functools.partial(_out_proj_kernel, pieces=pieces),
        grid=(m // tm,),
        in_specs=[pl.BlockSpec((tm, d), lambda i: (i, 0))] + o_specs + g_specs + [
            pl.BlockSpec(w_out.shape, lambda i: (0, 0), pipeline_mode=pl.Buffered(1)),
            pl.BlockSpec((1, d), lambda i: (0, 0))],
        out_specs=pl.BlockSpec((tm, d), lambda i: (i, 0)),
        out_shape=jax.ShapeDtypeStruct((m, d), F32),
        compiler_params=_cparams("parallel"),
        name="out_proj",
    )(x, *o_args, *([h] * pieces), w_out, g_post.reshape(1, d))


def _layernorm(x, g, b):
    xc = x - jnp.mean(x, axis=-1, keepdims=True)
    y = xc * lax.rsqrt(jnp.mean(xc * xc, axis=-1, keepdims=True) + EPS)
    return y * g + b


def _cmix_prompt_kernel(cu_ref, cv_ref, lg_ref, lb_ref, ws_ref, bs_ref, o_ref):
    v = _layernorm(cv_ref[...], lg_ref[...], lb_ref[...]).astype(BF16)
    tril = _iota((C_CHUNK, C_CHUNK), 1) <= _iota((C_CHUNK, C_CHUNK), 0)
    for g in range(C_GROUPS):
        cols = slice(g * C_GDIM, (g + 1) * C_GDIM)
        w = jnp.where(tril, ws_ref[g], 0.0).astype(BF16)
        for c in range(cu_ref.shape[0] // C_CHUNK):
            rows = slice(c * C_CHUNK, (c + 1) * C_CHUNK)
            o_ref[rows, cols] = cu_ref[rows, cols] * (_dot(w, v[rows, cols]) + bs_ref[:, cols])


def cmix_prompt(h, ln_g, ln_b, ws, bs, tm=512):
    m = h.shape[0]
    bs_rows = jnp.repeat(bs.T, C_GDIM, axis=1)
    vec = pl.BlockSpec((1, C_WIDTH), lambda i: (0, 0))
    return pl.pallas_call(
        _cmix_prompt_kernel,
        grid=(m // tm,),
        in_specs=[pl.BlockSpec((tm, C_WIDTH), lambda i: (i, O_CU // C_WIDTH)),
                  pl.BlockSpec((tm, C_WIDTH), lambda i: (i, O_CV // C_WIDTH)),
                  vec, vec,
                  pl.BlockSpec(ws.shape, lambda i: (0, 0, 0)),
                  pl.BlockSpec(bs_rows.shape, lambda i: (0, 0))],
        out_specs=pl.BlockSpec((tm, C_WIDTH), lambda i: (i, 0)),
        out_shape=jax.ShapeDtypeStruct((m, C_WIDTH), F32),
        compiler_params=_cparams("parallel"),
        name="cmix_prompt",
    )(h, h, ln_g.reshape(1, -1), ln_b.reshape(1, -1), ws, bs_rows)


def _cmix_sample_kernel(cu_ref, cv_ref, lg_ref, lb_ref, w_ref, bs_ref, o_ref, v_ref):
    v = _layernorm(cv_ref[...], lg_ref[...], lb_ref[...])
    v_ref[...] = v
    t = v.shape[1]
    i = _iota((t, C_WIDTH), 0)
    acc = jnp.broadcast_to(bs_ref[...][None], v.shape)
    for j in range(t):
        wj = jnp.where(i >= j, w_ref[j], 0.0)
        acc = acc + wj[None] * v[:, j:j + 1, :]
    o_ref[...] = cu_ref[...] * acc


def cmix_sample(h3, ln_g, ln_b, ws, bs):
    n, t, _ = h3.shape
    w_cols = jnp.repeat(ws[:, :t, :t].transpose(2, 1, 0), C_GDIM, axis=2)
    bs_rows = jnp.repeat(bs[:, :t].T, C_GDIM, axis=1)
    vec = pl.BlockSpec((1, 1, C_WIDTH), lambda i: (0, 0, 0))
    shape = jax.ShapeDtypeStruct((n, t, C_WIDTH), F32)
    blk = pl.BlockSpec((n, t, C_WIDTH), lambda i: (0, 0, 0))
    return pl.pallas_call(
        _cmix_sample_kernel,
        grid=(1,),
        in_specs=[pl.BlockSpec((n, t, C_WIDTH), lambda i: (0, 0, O_CU // C_WIDTH)),
                  pl.BlockSpec((n, t, C_WIDTH), lambda i: (0, 0, O_CV // C_WIDTH)),
                  vec, vec,
                  pl.BlockSpec(w_cols.shape, lambda i: (0, 0, 0)),
                  pl.BlockSpec(bs_rows.shape, lambda i: (0, 0))],
        out_specs=[blk, blk],
        out_shape=[shape, shape],
        compiler_params=_cparams("arbitrary"),
        name="cmix_sample",
    )(h3, h3, ln_g.reshape(1, 1, -1), ln_b.reshape(1, 1, -1), w_cols, bs_rows)


def _top_blocks(scores, allowed, axis):
    nb = scores.shape[axis]
    bidx = _iota(scores.shape, axis).astype(F32)
    cur = jnp.where(allowed, scores, NEG)
    sel = jnp.zeros(scores.shape, F32)
    for _ in range(D_TOPK):
        mx = cur.max(axis, keepdims=True)
        first = jnp.where(cur == mx, bidx, float(nb)).min(axis, keepdims=True)
        pick = bidx == first
        sel = jnp.where(pick, jnp.where(allowed, 1.0, sel), sel)
        cur = jnp.where(pick, BELOW_NEG, cur)
    return sel


def _moba_prompt_kernel(q_ref, k_ref, v_ref, o_ref, km_s, kb_s, vt_s, bias_s):
    nb = kb_s.shape[0]
    g = D_HEADS // D_KV
    nq = g * D_BLOCK
    qb = pl.program_id(2)

    @pl.when(qb == 0)
    def _():
        for b in range(nb):
            rows = slice(b * D_BLOCK, (b + 1) * D_BLOCK)
            kblk = k_ref[rows, :]
            km_s[b:b + 1, :] = jnp.mean(kblk, axis=0, keepdims=True)
            kb_s[b] = kblk.astype(BF16)
            vt_s[b] = v_ref[rows, :].T.astype(BF16)

    qf = jnp.concatenate([q_ref[:, gi * HEAD_DIM:(gi + 1) * HEAD_DIM] for gi in range(g)], axis=0) * ATTN_SCALE
    q = qf.astype(BF16)
    scores = _dot_nt(km_s[...], qf, precision=lax.Precision.HIGHEST)
    sel = _top_blocks(scores, _iota((nb, nq), 0) < qb, axis=0)
    bias_s[...] = jnp.where(sel > 0.5, 0.0, NEG)

    causal = _iota((D_BLOCK, nq), 0) <= _mod(_iota((D_BLOCK, nq), 1), D_BLOCK)
    s = jnp.where(causal, _dot_nt(kb_s[qb], q), NEG)
    m = s.max(0, keepdims=True)
    p = jnp.exp(s - m)
    l = p.sum(0, keepdims=True)
    acc = _dot(vt_s[qb], p.astype(BF16))

    def body(j, carry):
        m, l, acc = carry
        s = _dot_nt(kb_s[j], q) + bias_s[pl.ds(j, 1), :]
        m_new = jnp.maximum(m, s.max(0, keepdims=True))
        alpha = jnp.exp(m - m_new)
        p = jnp.exp(s - m_new)
        return m_new, alpha * l + p.sum(0, keepdims=True), alpha * acc + _dot(vt_s[j], p.astype(BF16))

    m, l, acc = lax.fori_loop(0, qb, body, (m, l, acc))
    out = acc / l
    for gi in range(g):
        o_ref[:, gi * HEAD_DIM:(gi + 1) * HEAD_DIM] = out[:, gi * D_BLOCK:(gi + 1) * D_BLOCK].T


def moba_prompt(h3):
    n, seq, _ = h3.shape
    g = D_HEADS // D_KV
    gw = g * HEAD_DIM
    nb = seq // D_BLOCK
    return pl.pallas_call(
        _moba_prompt_kernel,
        grid=(n, D_KV, nb),
        in_specs=[pl.BlockSpec((None, D_BLOCK, gw), lambda b, hh, i: (b, i, O_DQ // gw + hh)),
                  pl.BlockSpec((None, seq, HEAD_DIM), lambda b, hh, i: (b, 0, O_DK // HEAD_DIM + hh)),
                  pl.BlockSpec((None, seq, HEAD_DIM), lambda b, hh, i: (b, 0, O_DV // HEAD_DIM + hh))],
        out_specs=pl.BlockSpec((None, D_BLOCK, gw), lambda b, hh, i: (b, i, hh)),
        out_shape=jax.ShapeDtypeStruct((n, seq, D_WIDTH), F32),
        scratch_shapes=[pltpu.VMEM((nb, HEAD_DIM), F32),
                        pltpu.VMEM((nb, D_BLOCK, HEAD_DIM), BF16),
                        pltpu.VMEM((nb, HEAD_DIM, D_BLOCK), BF16),
                        pltpu.VMEM((nb, g * D_BLOCK), F32)],
        compiler_params=_cparams("parallel", "parallel", "arbitrary"),
        name="moba_prompt",
    )(h3, h3, h3)


KV_PAGES_BLOCK = 4


def _kv_pages_kernel(k_ref, v_ref, o_ref):
    for pg in range(o_ref.shape[0]):
        rows = slice(pg * PAGE_SIZE, (pg + 1) * PAGE_SIZE)
        for kv, src in enumerate((k_ref, v_ref)):
            for hh in range(D_KV):
                o_ref[pg, kv, hh] = src[rows, hh * HEAD_DIM:(hh + 1) * HEAD_DIM]


def kv_pages(h3):
    n, seq, _ = h3.shape
    kw = D_KV * HEAD_DIM
    pb = KV_PAGES_BLOCK
    return pl.pallas_call(
        _kv_pages_kernel,
        grid=(n, seq // (pb * PAGE_SIZE)),
        in_specs=[pl.BlockSpec((None, pb * PAGE_SIZE, kw), lambda b, p: (b, p, O_DK // kw)),
                  pl.BlockSpec((None, pb * PAGE_SIZE, kw), lambda b, p: (b, p, O_DV // kw))],
        out_specs=pl.BlockSpec((None, pb, 2, D_KV, PAGE_SIZE, HEAD_DIM), lambda b, p: (b, p, 0, 0, 0, 0)),
        out_shape=jax.ShapeDtypeStruct((n, seq // PAGE_SIZE, 2, D_KV, PAGE_SIZE, HEAD_DIM), F32),
        compiler_params=_cparams("parallel", "parallel"),
        name="kv_pages",
    )(h3, h3)


MOBA_CHUNK_PAGES = 16
MOBA_VMEM_LIMIT = 52 * 1024 * 1024


def _moba_decode_kernel(pt_ref, q_ref, kn_ref, vn_ref, pool_ref, o_ref, buf, sem, lg_s, km_s, p_s, bias_s, e_s):
    seq = pl.program_id(0)
    n_seq = pl.num_programs(0)
    n_pages = buf.shape[1]
    cp = MOBA_CHUNK_PAGES
    ppb = D_BLOCK // PAGE_SIZE
    n_blk = n_pages // ppb
    g = D_HEADS // D_KV
    nq = q_ref.shape[1]
    lane_blk = (nq, PAGE_SIZE)

    def page_copies(s, slot):
        return [pltpu.make_async_copy(pool_ref.at[pt_ref[s, pg]], buf.at[slot, pg], sem.at[slot])
                for pg in range(n_pages)]

    @pl.when(seq == 0)
    def _():
        for c in page_copies(0, 0):
            c.start()
        e_s[...] = jnp.where(_div(_iota(e_s.shape, 1), PAGE_SIZE) == _iota(e_s.shape, 0), 1.0, 0.0).astype(BF16)

    @pl.when(seq + 1 < n_seq)
    def _():
        for c in page_copies(seq + 1, (seq + 1) % 2):
            c.start()

    slot = seq % 2
    for c in page_copies(seq, slot):
        c.wait()

    for hk in range(D_KV):
        qf = q_ref[hk] * ATTN_SCALE
        q = qf.astype(BF16)
        for c in range(n_pages // cp):
            ks = [buf[slot, c * cp + i, 0, hk] for i in range(cp)]
            lg_s[hk, :, c * cp * PAGE_SIZE:(c + 1) * cp * PAGE_SIZE] = _dot_nt(
                q, jnp.concatenate([k.astype(BF16) for k in ks], axis=0))
            sums = [k.sum(axis=0, keepdims=True) for k in ks]
            means = [functools.reduce(lambda a, b: a + b, sums[b * ppb:(b + 1) * ppb]) * (1.0 / D_BLOCK)
                     for b in range(cp // ppb)]
            km_s[hk, c * (cp // ppb):(c + 1) * (cp // ppb), :] = jnp.concatenate(means, axis=0)

        scores = _dot_nt(qf, km_s[hk], precision=lax.Precision.HIGHEST)
        sel = _top_blocks(scores, _iota(scores.shape, 1) >= 0, axis=1)
        bias_s[...] = jnp.where(_dot(sel.astype(BF16), e_s[...]) > 0.5, 0.0, NEG)

        own = _dot_nt(q, kn_ref[hk].astype(BF16))
        own = jnp.where(_iota(own.shape, 1) <= _div(_iota(own.shape, 0), g), own, NEG)

        def masked(pg):
            b = pg // ppb
            return lg_s[hk, :, pg * PAGE_SIZE:(pg + 1) * PAGE_SIZE] + bias_s[:, b * PAGE_SIZE:(b + 1) * PAGE_SIZE]

        mx = masked(0)
        for pg in range(1, n_pages):
            mx = jnp.maximum(mx, masked(pg))
        m = jnp.maximum(mx.max(-1, keepdims=True), own.max(-1, keepdims=True))
        m_rep = jnp.broadcast_to(m, lane_blk)
        tot = jnp.zeros(lane_blk, F32)
        for pg in range(n_pages):
            p = jnp.exp(masked(pg) - m_rep)
            tot = tot + p
            p_s[hk, :, pg * PAGE_SIZE:(pg + 1) * PAGE_SIZE] = p.astype(BF16)
        po = jnp.exp(own - m)
        den = tot.sum(-1, keepdims=True) + po.sum(-1, keepdims=True)

        acc = _dot(po.astype(BF16), vn_ref[hk].astype(BF16))
        for c in range(n_pages // cp):
            v = jnp.concatenate([buf[slot, c * cp + i, 1, hk].astype(BF16) for i in range(cp)], axis=0)
            acc = acc + _dot(p_s[hk, :, c * cp * PAGE_SIZE:(c + 1) * cp * PAGE_SIZE], v)
        o_ref[hk] = acc / den


def moba_decode(pool, page_table, q, k_new, v_new):
    n, n_pages = page_table.shape
    assert (n_pages * PAGE_SIZE) % D_BLOCK == 0 and n_pages % MOBA_CHUNK_PAGES == 0
    nq, t = q.shape[2], k_new.shape[2]
    n_blk = n_pages * PAGE_SIZE // D_BLOCK
    keys = n_pages * PAGE_SIZE

    def per_seq(r):
        return pl.BlockSpec((None, D_KV, r, HEAD_DIM), lambda b, pt: (b, 0, 0, 0))

    grid_spec = pltpu.PrefetchScalarGridSpec(
        num_scalar_prefetch=1,
        grid=(n,),
        in_specs=[per_seq(nq), per_seq(t), per_seq(t), pl.BlockSpec(memory_space=pl.ANY)],
        out_specs=per_seq(nq),
        scratch_shapes=[pltpu.VMEM((2, n_pages) + pool.shape[1:], F32),
                        pltpu.SemaphoreType.DMA((2,)),
                        pltpu.VMEM((D_KV, nq, keys), F32),
                        pltpu.VMEM((D_KV, n_blk, HEAD_DIM), F32),
                        pltpu.VMEM((D_KV, nq, keys), BF16),
                        pltpu.VMEM((nq, n_blk * PAGE_SIZE), F32),
                        pltpu.VMEM((n_blk, n_blk * PAGE_SIZE), BF16)])
    return pl.pallas_call(
        _moba_decode_kernel,
        grid_spec=grid_spec,
        out_shape=jax.ShapeDtypeStruct((n, D_KV, nq, HEAD_DIM), F32),
        compiler_params=pltpu.CompilerParams(dimension_semantics=("arbitrary",), vmem_limit_bytes=MOBA_VMEM_LIMIT),
        name="moba_decode",
    )(page_table, q, k_new, v_new, pool)


def even_layer_prompt(x, w_in, w_out, g_pre, g_post, sink):
    n, s, d = x.shape
    x2 = x.reshape(n * s, d)
    h = norm_proj(x2, g_pre, w_in, tm=1024)
    h3 = h.reshape(n, s, IN_EVEN)
    a_out = attn_a_prompt(h3).reshape(n * s, A_WIDTH)
    b_out = attn_b_prompt(h3, sink).reshape(n * s, B_WIDTH)
    y = out_proj(x2, [a_out, b_out], h, [E_AG, E_BG, E_BG + GATE_W], w_out, g_post)
    rows = [kv_rows(h3, E_AK + gi * A_WIDTH, E_AV + gi * A_WIDTH, A_HEADS, win) for gi, (win, _) in enumerate(A_GROUPS)]
    rows.append(kv_rows(h3, E_BK, E_BV, B_KV, B_WINDOW))
    return y.reshape(n, s, d), rows


def even_layer_sample(x, bufs_a, buf_b, w_in, w_out, g_pre, g_post, sink):
    n, t, d = x.shape
    x2 = x.reshape(n * t, d)
    h = norm_proj(x2, g_pre, w_in, tm=n * t)
    stats, new_bufs = [], []
    for gi, (_, dil) in enumerate(A_GROUPS):
        cols = slice(gi * A_WIDTH, (gi + 1) * A_WIDTH)
        q = h[:, E_AQ:E_AK][:, cols].reshape(n, t * A_HEADS, HEAD_DIM)
        k = h[:, E_AK:E_AV][:, cols].reshape(n, t * A_HEADS, HEAD_DIM)
        v = h[:, E_AV:E_AG][:, cols].reshape(n, t * A_HEADS, HEAD_DIM)
        buf, o, m, l = decode_attn(bufs_a[gi], q, k, v, A_HEADS, A_HEADS, dil)
        stats.append((o, m, l))
        new_bufs.append(buf)
    a_out = merge_groups(stats).reshape(n * t, A_WIDTH)
    q = h[:, E_BQ:E_BK].reshape(n, t * B_HEADS, HEAD_DIM)
    k = h[:, E_BK:E_BV].reshape(n, t * B_KV, HEAD_DIM)
    v = h[:, E_BV:E_BG].reshape(n, t * B_KV, HEAD_DIM)
    sink_rows = jnp.broadcast_to(jnp.tile(sink, t)[:, None], (t * B_HEADS, HEAD_DIM))
    buf, b_out = decode_attn(buf_b, q, k, v, B_KV, B_HEADS, 1, sink=sink_rows)
    new_bufs.append(buf)
    y = out_proj(x2, [a_out, b_out.reshape(n * t, B_WIDTH)], h, [E_AG, E_BG, E_BG + GATE_W], w_out, g_post)
    return y.reshape(n, t, d), new_bufs


def odd_layer_prompt(x, w_in, w_out, g_pre, g_post, ln_g, ln_b, ws, bs):
    n, s, d = x.shape
    x2 = x.reshape(n * s, d)
    h = norm_proj(x2, g_pre, w_in, tm=1024)
    h3 = h.reshape(n, s, IN_ODD)
    c_out = cmix_prompt(h, ln_g, ln_b, ws, bs)
    d_out = moba_prompt(h3).reshape(n * s, D_WIDTH)
    gates = [O_CG, O_CG + GATE_W, O_DG, O_DG + GATE_W]
    y = out_proj(x2, [c_out, d_out], h, gates, w_out, g_post)
    return y.reshape(n, s, d), kv_pages(h3)


def odd_layer_sample(x, pool, page_table, w_in, w_out, g_pre, g_post, ln_g, ln_b, ws, bs):
    n, t, d = x.shape
    g = D_HEADS // D_KV
    x2 = x.reshape(n * t, d)
    h = norm_proj(x2, g_pre, w_in, tm=n * t)
    c_out, v_rows = cmix_sample(h.reshape(n, t, IN_ODD), ln_g, ln_b, ws, bs)
    q = h[:, O_DQ:O_DK].reshape(n, t, D_KV, g, HEAD_DIM).transpose(0, 2, 1, 3, 4).reshape(n, D_KV, t * g, HEAD_DIM)
    k = h[:, O_DK:O_DV].reshape(n, t, D_KV, HEAD_DIM).transpose(0, 2, 1, 3)
    v = h[:, O_DV:O_DG].reshape(n, t, D_KV, HEAD_DIM).transpose(0, 2, 1, 3)
    o = moba_decode(pool, page_table, q, k, v)
    d_out = o.reshape(n, D_KV, t, g, HEAD_DIM).transpose(0, 2, 1, 3, 4).reshape(n * t, D_WIDTH)
    gates = [O_CG, O_CG + GATE_W, O_DG, O_DG + GATE_W]
    y = out_proj(x2, [c_out.reshape(n * t, C_WIDTH), d_out], h, gates, w_out, g_post)
    return y.reshape(n, t, d), jnp.stack([k, v], axis=1), v_rows


def kernel(x_prompt, x_sample, cache_a1, cache_a2, cache_a3, cache_b, cache_d, page_table, norm_pre_even, norm_post_even, w_in_even, w_out_even, sink_b, norm_pre_odd, norm_post_odd, w_in_odd, w_out_odd, c_ln_g, c_ln_b, c_ws, c_bs):
    yp, ys = x_prompt, x_sample
    caches_a = (cache_a1, cache_a2, cache_a3)
    a_p, a_s = [[] for _ in range(A_NG)], [[] for _ in range(A_NG)]
    b_p, b_s, d_p, d_s, c_s = [], [], [], [], []
    for layer in range(DEPTH):
        li = layer // 2
        if layer % 2 == 0:
            w_in, w_out = w_in_even[li].astype(BF16), w_out_even[li].astype(BF16)
            yp, rows = even_layer_prompt(yp, w_in, w_out, norm_pre_even[li], norm_post_even[li], sink_b[li])
            ys, bufs = even_layer_sample(ys, [c[li] for c in caches_a], cache_b[li], w_in, w_out,
                                         norm_pre_even[li], norm_post_even[li], sink_b[li])
            for gi in range(A_NG):
                a_p[gi].append(rows[gi])
                a_s[gi].append(bufs[gi])
            b_p.append(rows[A_NG])
            b_s.append(bufs[A_NG])
        else:
            w_in, w_out = w_in_odd[li].astype(BF16), w_out_odd[li].astype(BF16)
            yp, kvp = odd_layer_prompt(yp, w_in, w_out, norm_pre_odd[li], norm_post_odd[li],
                                       c_ln_g[li], c_ln_b[li], c_ws[li], c_bs[li])
            ys, kvs, vrows = odd_layer_sample(ys, cache_d[li], page_table, w_in, w_out, norm_pre_odd[li],
                                              norm_post_odd[li], c_ln_g[li], c_ln_b[li], c_ws[li], c_bs[li])
            d_p.append(kvp)
            d_s.append(kvs)
            c_s.append(vrows)
    return (yp, ys, jnp.stack(a_p[0]), jnp.stack(a_s[0]), jnp.stack(a_p[1]), jnp.stack(a_s[1]),
            jnp.stack(a_p[2]), jnp.stack(a_s[2]), jnp.stack(b_p), jnp.stack(b_s),
            jnp.stack(d_p), jnp.stack(d_s), jnp.stack(c_s))
```

```python
import functools

import jax
import jax.numpy as jnp
from jax import lax
from jax.experimental import pallas as pl
from jax.experimental.pallas import tpu as pltpu

F32 = jnp.float32
BF16 = jnp.bfloat16

D_MODEL = 2048
DEPTH = 2
DEC_SEQ = 8
PAST_LEN = 8192
PAGE_SIZE = 128
HEAD_DIM = 128
ATTN_SCALE = HEAD_DIM ** -0.5
A_GROUPS = ((128, 1), (512, 4), (2048, 16))
A_NG = len(A_GROUPS)
A_HEADS = 4
A_QKV = A_NG * A_HEADS * HEAD_DIM
A_WIDTH = A_HEADS * HEAD_DIM
B_HEADS = 8
B_KV = 2
B_WINDOW = 128
B_WIDTH = B_HEADS * HEAD_DIM
C_CHUNK = 128
C_GROUPS = 8
C_GDIM = 128
C_WIDTH = C_GROUPS * C_GDIM
D_HEADS = 8
D_KV = 2
D_BLOCK = 256
D_TOPK = 3
D_WIDTH = D_HEADS * HEAD_DIM
EPS = 1e-6
NEG = -1e30
BELOW_NEG = -3e38

E_AQ, E_AK, E_AV = 0, A_QKV, 2 * A_QKV
E_AG = 3 * A_QKV
E_BQ = E_AG + A_WIDTH
E_BK = E_BQ + B_WIDTH
E_BV = E_BK + B_KV * HEAD_DIM
E_BG = E_BV + B_KV * HEAD_DIM
IN_EVEN = E_BG + B_WIDTH
O_CU, O_CV, O_CG = 0, C_WIDTH, 2 * C_WIDTH
O_DQ = 3 * C_WIDTH
O_DK = O_DQ + D_WIDTH
O_DV = O_DK + D_KV * HEAD_DIM
O_DG = O_DV + D_KV * HEAD_DIM
IN_ODD = O_DG + D_WIDTH

BAND = 128
GATE_W = 512
VMEM_LIMIT = 48 * 1024 * 1024


def _cparams(*sem):
    return pltpu.CompilerParams(dimension_semantics=sem, vmem_limit_bytes=VMEM_LIMIT)


def _dot_nt(a, b, precision=None):
    return lax.dot_general(a, b, (((1,), (1,)), ((), ())), preferred_element_type=F32, precision=precision)


def _dot(a, b):
    return jnp.dot(a, b, preferred_element_type=F32)


def _iota(shape, dim):
    return lax.broadcasted_iota(jnp.int32, shape, dim)


def _div(x, d):
    assert d & (d - 1) == 0
    return x >> (d.bit_length() - 1)


def _mod(x, d):
    assert d & (d - 1) == 0
    return x & (d - 1)


def _norm_proj_kernel(x_ref, g_ref, w_ref, o_ref, xn_ref):
    @pl.when(pl.program_id(1) == 0)
    def _():
        x = x_ref[...]
        r = lax.rsqrt(jnp.mean(x * x, axis=-1, keepdims=True) + EPS)
        xn_ref[...] = (x * r * g_ref[...]).astype(BF16)

    o_ref[...] = _dot(xn_ref[...], w_ref[...])


def norm_proj(x, g, w, tm, tn=512):
    m, d = x.shape
    n = w.shape[1]
    return pl.pallas_call(
        _norm_proj_kernel,
        grid=(m // tm, n // tn),
        in_specs=[pl.BlockSpec((tm, d), lambda i, j: (i, 0)),
                  pl.BlockSpec((1, d), lambda i, j: (0, 0)),
                  pl.BlockSpec((d, tn), lambda i, j: (0, j))],
        out_specs=pl.BlockSpec((tm, tn), lambda i, j: (i, j)),
        out_shape=jax.ShapeDtypeStruct((m, n), F32),
        scratch_shapes=[pltpu.VMEM((tm, d), BF16)],
        compiler_params=_cparams("parallel", "arbitrary"),
        name="norm_proj",
    )(x, g.reshape(1, d), w)


def _band_block(q, kc, vc, kp, vp, prev_floor=0):
    r = q.shape[0]
    i = _mod(_iota((r, BAND), 0), BAND)
    j = _iota((r, BAND), 1)
    sc = jnp.where(j <= i, _dot_nt(q, kc), NEG)
    m = sc.max(-1, keepdims=True)
    if kp is not None:
        sp = jnp.where(j >= i + prev_floor, _dot_nt(q, kp), NEG)
        m = jnp.maximum(m, sp.max(-1, keepdims=True))
    pc = jnp.exp(sc - m)
    l = pc.sum(-1, keepdims=True)
    o = _dot(pc.astype(BF16), vc)
    if kp is not None:
        pp = jnp.exp(sp - m)
        l = l + pp.sum(-1, keepdims=True)
        o = o + _dot(pp.astype(BF16), vp)
    return o, m, l


def _rows(ref, start, stride):
    if stride == 1:
        return ref[pl.ds(start, BAND), :]
    return ref[pl.ds(start, BAND, stride=stride), :]


def _attn_a_prompt_kernel(q1, k1, v1, q2, k2, v2, q3, k3, v3, o_ref, o1_s, m1_s, l1_s, o2_s, m2_s, l2_s):
    seq = q1.shape[0]
    wide = (BAND, HEAD_DIM)

    def load(qr, kr, vr, start, stride, prev_start):
        q = (_rows(qr, start, stride) * ATTN_SCALE).astype(BF16)
        kc = _rows(kr, start, stride).astype(BF16)
        vc = _rows(vr, start, stride).astype(BF16)
        if prev_start is None:
            return q, kc, vc, None, None
        return q, kc, vc, _rows(kr, prev_start, stride).astype(BF16), _rows(vr, prev_start, stride).astype(BF16)

    for b in range(seq // BAND):
        start = b * BAND
        o, m, l = _band_block(*load(q1, k1, v1, start, 1, None if b == 0 else start - BAND))
        o1_s[start:start + BAND, :] = o
        m1_s[start:start + BAND, :] = jnp.broadcast_to(m, wide)
        l1_s[start:start + BAND, :] = jnp.broadcast_to(l, wide)

    d2 = A_GROUPS[1][1]
    sub2 = seq // d2
    for r in range(d2):
        for b in range(sub2 // BAND):
            start = r + b * BAND * d2
            prev = None if b == 0 else start - BAND * d2
            o, m, l = _band_block(*load(q2, k2, v2, start, d2, prev))
            o2_s[r, b * BAND:(b + 1) * BAND, :] = o
            m2_s[r, b * BAND:(b + 1) * BAND, :] = jnp.broadcast_to(m, wide)
            l2_s[r, b * BAND:(b + 1) * BAND, :] = jnp.broadcast_to(l, wide)

    d3 = A_GROUPS[2][1]
    ratio = d3 // d2
    for r in range(d3):
        o3, m3, l3 = _band_block(*load(q3, k3, v3, r, d3, None))
        m3 = jnp.broadcast_to(m3, wide)
        l3 = jnp.broadcast_to(l3, wide)
        o1 = o1_s[pl.ds(r, BAND, stride=d3), :]
        m1 = m1_s[pl.ds(r, BAND, stride=d3), :]
        l1 = l1_s[pl.ds(r, BAND, stride=d3), :]
        r2, off2 = r % d2, r // d2
        o2 = o2_s[r2, pl.ds(off2, BAND, stride=ratio), :]
        m2 = m2_s[r2, pl.ds(off2, BAND, stride=ratio), :]
        l2 = l2_s[r2, pl.ds(off2, BAND, stride=ratio), :]
        mmax = jnp.maximum(jnp.maximum(m1, m2), m3)
        e1, e2, e3 = jnp.exp(m1 - mmax), jnp.exp(m2 - mmax), jnp.exp(m3 - mmax)
        num = o1 * e1 + o2 * e2 + o3 * e3
        den = l1 * e1 + l2 * e2 + l3 * e3
        o_ref[pl.ds(r, BAND, stride=d3), :] = num / den


def attn_a_prompt(h3):
    n, seq, _ = h3.shape
    assert seq == A_GROUPS[2][0] and all(w // d == BAND for w, d in A_GROUPS)

    def spec(off, gi):
        blk = off // HEAD_DIM + gi * A_HEADS
        return pl.BlockSpec((None, seq, HEAD_DIM), lambda b, hh: (b, 0, blk + hh))

    in_specs = []
    for gi in range(A_NG):
        in_specs += [spec(E_AQ, gi), spec(E_AK, gi), spec(E_AV, gi)]
    d2 = A_GROUPS[1][1]
    stat = pltpu.VMEM((seq, HEAD_DIM), F32)
    stat2 = pltpu.VMEM((d2, seq // d2, HEAD_DIM), F32)
    return pl.pallas_call(
        _attn_a_prompt_kernel,
        grid=(n, A_HEADS),
        in_specs=in_specs,
        out_specs=pl.BlockSpec((None, seq, HEAD_DIM), lambda b, hh: (b, 0, hh)),
        out_shape=jax.ShapeDtypeStruct((n, seq, A_WIDTH), F32),
        scratch_shapes=[stat, stat, stat, stat2, stat2, stat2],
        compiler_params=_cparams("parallel", "parallel"),
        name="attn_a_prompt",
    )(*([h3] * 9))


def _attn_b_prompt_kernel(sink_ref, q_ref, k_ref, v_ref, o_ref, kb_s, vt_s):
    seq = q_ref.shape[0]
    g = B_HEADS // B_KV
    nq = g * BAND
    nblk = seq // BAND
    kvh = pl.program_id(1)
    for b in range(nblk):
        rows = slice(b * BAND, (b + 1) * BAND)
        kb_s[rows, :] = k_ref[rows, :].astype(BF16)
        vt_s[b] = v_ref[rows, :].T.astype(BF16)
    head = _div(_iota((1, nq), 1), BAND)
    sink = jnp.zeros((1, nq), F32)
    for gi in range(g):
        sink = jnp.where(head == gi, sink_ref[kvh * g + gi], sink)
    qi = _mod(_iota((2 * BAND, nq), 1), BAND)
    key = _iota((2 * BAND, nq), 0)
    band = jnp.logical_and(key >= qi, key <= qi + BAND)
    for b in range(nblk):
        rows = slice(b * BAND, (b + 1) * BAND)
        qb = q_ref[rows, :]
        q = jnp.concatenate([qb[:, gi * HEAD_DIM:(gi + 1) * HEAD_DIM] for gi in range(g)], axis=0)
        q = (q * ATTN_SCALE).astype(BF16)
        if b == 0:
            s = jnp.where(band[BAND:], _dot_nt(kb_s[rows, :], q), NEG)
        else:
            s = jnp.where(band, _dot_nt(kb_s[(b - 1) * BAND:(b + 1) * BAND, :], q), NEG)
        m = s.max(0, keepdims=True)
        pf = jnp.exp(s - m)
        l = pf.sum(0, keepdims=True)
        p = pf.astype(BF16)
        if b == 0:
            o = _dot(vt_s[0], p)
        else:
            o = _dot(vt_s[b - 1], p[:BAND]) + _dot(vt_s[b], p[BAND:])
        m2 = jnp.maximum(m, sink)
        a = jnp.exp(m - m2)
        out = o * (a / (l * a + jnp.exp(sink - m2)))
        for gi in range(g):
            o_ref[rows, gi * HEAD_DIM:(gi + 1) * HEAD_DIM] = out[:, gi * BAND:(gi + 1) * BAND].T


def attn_b_prompt(h3, sink):
    n, seq, _ = h3.shape
    gw = (B_HEADS // B_KV) * HEAD_DIM
    return pl.pallas_call(
        _attn_b_prompt_kernel,
        grid=(n, B_KV),
        in_specs=[pl.BlockSpec(memory_space=pltpu.SMEM),
                  pl.BlockSpec((None, seq, gw), lambda b, hh: (b, 0, E_BQ // gw + hh)),
                  pl.BlockSpec((None, seq, HEAD_DIM), lambda b, hh: (b, 0, E_BK // HEAD_DIM + hh)),
                  pl.BlockSpec((None, seq, HEAD_DIM), lambda b, hh: (b, 0, E_BV // HEAD_DIM + hh))],
        out_specs=pl.BlockSpec((None, seq, gw), lambda b, hh: (b, 0, hh)),
        out_shape=jax.ShapeDtypeStruct((n, seq, B_WIDTH), F32),
        scratch_shapes=[pltpu.VMEM((seq, HEAD_DIM), BF16), pltpu.VMEM((seq // BAND, HEAD_DIM, BAND), BF16)],
        compiler_params=_cparams("parallel", "parallel"),
        name="attn_b_prompt",
    )(sink, h3, h3, h3)


KV_ROWS_BLOCK = 512


def _kv_rows_kernel(k_ref, v_ref, o_ref, *, heads):
    npos = k_ref.shape[0]
    for kv, src in enumerate((k_ref, v_ref)):
        for hh in range(heads):
            o_ref[pl.ds(kv * heads + hh, npos, stride=2 * heads), :] = src[:, hh * HEAD_DIM:(hh + 1) * HEAD_DIM]


def kv_rows(h3, k_off, v_off, heads, window):
    n, seq, _ = h3.shape
    w = min(window, seq)
    hw = heads * HEAD_DIM
    npos = min(w, KV_ROWS_BLOCK)
    first = (seq - w) // npos
    rows = npos * 2 * heads
    out = pl.pallas_call(
        functools.partial(_kv_rows_kernel, heads=heads),
        grid=(n, w // npos),
        in_specs=[pl.BlockSpec((None, npos, hw), lambda b, c: (b, first + c, k_off // hw)),
                  pl.BlockSpec((None, npos, hw), lambda b, c: (b, first + c, v_off // hw))],
        out_specs=pl.BlockSpec((None, rows, HEAD_DIM), lambda b, c: (b, c, 0)),
        out_shape=jax.ShapeDtypeStruct((n, w * 2 * heads, HEAD_DIM), F32),
        compiler_params=_cparams("parallel", "parallel"),
        name="kv_rows",
    )(h3, h3)
    return out.reshape(n, w, 2, heads, HEAD_DIM)


DEC_CHUNK = 2048
DEC_UNROLL = 4


def _decode_kernel(*refs, heads, q_heads, dil, use_sink):
    if use_sink:
        sink_ref, q_ref, kn_ref, vn_ref, nf_ref, c_ref, nc_ref, out_ref, bias_s, biasn_s, l_s = refs
    else:
        q_ref, kn_ref, vn_ref, nf_ref, c_ref, nc_ref, out_ref, m_ref, l_ref, bias_s, biasn_s, l_s = refs
    slots = 2 * heads
    seqs, rows = c_ref.shape[:2]
    lb = rows // slots
    nq = q_ref.shape[1]
    t_new = kn_ref.shape[1] // heads
    per_kv = q_heads // heads
    chunk = min(DEC_CHUNK, rows)
    fresh = nf_ref.shape[1]

    @pl.when(pl.program_id(0) == 0)
    def _():
        row = _iota((nq, rows), 0)
        col = _iota((nq, rows), 1)
        t = _div(row, q_heads)
        kvh = _div(_mod(row, q_heads), per_kv)
        pos = _div(col, slots)
        ok = (_mod(col, slots) == kvh) & (pos >= t) & (_mod(lb + t - pos, dil) == 0)
        bias_s[...] = jnp.where(ok, 0.0, NEG)
        rown = _iota((nq, t_new * heads), 0)
        coln = _iota((nq, t_new * heads), 1)
        tn = _div(rown, q_heads)
        kvhn = _div(_mod(rown, q_heads), per_kv)
        tk = _div(coln, heads)
        okn = (_mod(coln, heads) == kvhn) & (tk <= tn) & (_mod(tn - tk, dil) == 0)
        biasn_s[...] = jnp.where(okn, 0.0, NEG)

    def one_seq(b, carry):
        q = (q_ref[b] * ATTN_SCALE).astype(BF16)
        sn = _dot_nt(q, kn_ref[b].astype(BF16)) + biasn_s[...]
        m = sn.max(-1, keepdims=True)
        for c in range(rows // chunk):
            sl = slice(c * chunk, (c + 1) * chunk)
            s = _dot_nt(q, c_ref[b, sl, :].astype(BF16)) + bias_s[:, sl]
            l_s[b, :, sl] = s
            m = jnp.maximum(m, s.max(-1, keepdims=True))
        pn = jnp.exp(sn - m)
        den = pn.sum(-1, keepdims=True)
        o = _dot(pn.astype(BF16), vn_ref[b].astype(BF16))
        for c in range(rows // chunk):
            sl = slice(c * chunk, (c + 1) * chunk)
            p = jnp.exp(l_s[b, :, sl] - m)
            den = den + p.sum(-1, keepdims=True)
            o = o + _dot(pltpu.roll(p, heads, axis=1).astype(BF16), c_ref[b, sl, :].astype(BF16))

        if use_sink:
            sink = sink_ref[...]
            m2 = jnp.maximum(m, sink)
            a = jnp.exp(m - m2)
            out_ref[b] = o * (a / (den * a + jnp.exp(sink - m2)))
        else:
            out_ref[b] = o
            m_ref[b] = jnp.broadcast_to(m, (nq, HEAD_DIM))
            l_ref[b] = jnp.broadcast_to(den, (nq, HEAD_DIM))

        nc_ref[b, pl.ds(0, rows - fresh), :] = c_ref[b, pl.ds(fresh, rows - fresh), :]
        nc_ref[b, pl.ds(rows - fresh, fresh), :] = nf_ref[b]
        return carry

    if seqs == 1:
        one_seq(0, 0)
    else:
        lax.fori_loop(0, seqs, one_seq, 0, unroll=min(seqs, DEC_UNROLL))


DEC_BLOCK_BYTES = 4 * 1024 * 1024


def decode_attn(cache, q, k_new, v_new, heads, q_heads, dil, sink=None):
    n, lb = cache.shape[:2]
    rows = lb * 2 * heads
    flat = cache.reshape(n, rows, HEAD_DIM)
    k4 = k_new.reshape(n, -1, 1, heads, HEAD_DIM)
    v4 = v_new.reshape(n, -1, 1, heads, HEAD_DIM)
    new_flat = jnp.concatenate([k4, v4], axis=2).reshape(n, -1, HEAD_DIM)
    seqs = 1
    while seqs * 2 * rows * HEAD_DIM * 4 <= DEC_BLOCK_BYTES and n % (seqs * 2) == 0:
        seqs *= 2
    nq = q.shape[1]
    tk = k_new.shape[1]
    use_sink = sink is not None

    def per_seq(r):
        return pl.BlockSpec((seqs, r, HEAD_DIM), lambda b: (b, 0, 0))

    in_specs = [per_seq(nq), per_seq(tk), per_seq(tk), per_seq(new_flat.shape[1]), per_seq(rows)]
    args = [q, k_new, v_new, new_flat, flat]
    stat = jax.ShapeDtypeStruct((n, nq, HEAD_DIM), F32)
    out_shape = [jax.ShapeDtypeStruct(flat.shape, F32), stat]
    out_specs = [per_seq(rows), per_seq(nq)]
    if use_sink:
        in_specs = [pl.BlockSpec((nq, HEAD_DIM), lambda b: (0, 0))] + in_specs
        args = [sink] + args
    else:
        out_shape += [stat, stat]
        out_specs += [per_seq(nq), per_seq(nq)]
    outs = pl.pallas_call(
        functools.partial(_decode_kernel, heads=heads, q_heads=q_heads, dil=dil, use_sink=use_sink),
        grid=(n // seqs,),
        in_specs=in_specs,
        out_specs=out_specs,
        out_shape=out_shape,
        scratch_shapes=[pltpu.VMEM((nq, rows), F32), pltpu.VMEM((nq, tk), F32), pltpu.VMEM((seqs, nq, rows), F32)],
        compiler_params=_cparams("arbitrary"),
        name="decode_attn",
    )(*args)
    return (outs[0].reshape(cache.shape),) + tuple(outs[1:])


def _merge_kernel(*refs):
    o_ref = refs[-1]
    stats = [refs[3 * g:3 * g + 3] for g in range(A_NG)]
    mmax = functools.reduce(jnp.maximum, [m[...] for _, m, _ in stats])
    num = 0.0
    den = 0.0
    for o, m, l in stats:
        e = jnp.exp(m[...] - mmax)
        num = num + o[...] * e
        den = den + l[...] * e
    o_ref[...] = num / den


def merge_groups(stats):
    flat = [a.reshape(-1, HEAD_DIM) for st in stats for a in st]
    out = pl.pallas_call(_merge_kernel, out_shape=jax.ShapeDtypeStruct(flat[0].shape, F32),
                         compiler_params=pltpu.CompilerParams(vmem_limit_bytes=VMEM_LIMIT),
                         name="merge_groups")(*flat)
    return out


def _out_proj_kernel(*refs, pieces):
    x_ref = refs[0]
    o_refs = refs[1:1 + pieces]
    g_refs = refs[1 + pieces:1 + 2 * pieces]
    w_ref, gp_ref, y_ref = refs[1 + 2 * pieces:]
    acc = None
    for p in range(pieces):
        g = g_refs[p][...]
        mix = (o_refs[p][...] * (g * (1.0 / (1.0 + jnp.exp(-g))))).astype(BF16)
        d = _dot(mix, w_ref[p * GATE_W:(p + 1) * GATE_W, :])
        acc = d if acc is None else acc + d
    r = lax.rsqrt(jnp.mean(acc * acc, axis=-1, keepdims=True) + EPS)
    y_ref[...] = x_ref[...] + acc * r * gp_ref[...]


def out_proj(x, outs, h, gate_offs, w_out, g_post, tm=512):
    m, d = x.shape
    tm = min(tm, m)
    o_args, o_specs = [], []
    for arr in outs:
        for c in range(arr.shape[1] // GATE_W):
            o_args.append(arr)
            o_specs.append(pl.BlockSpec((tm, GATE_W), lambda i, c=c: (i, c)))
    g_specs = [pl.BlockSpec((tm, GATE_W), lambda i, c=off // GATE_W: (i, c)) for off in gate_offs]
    pieces = len(o_args)
    assert pieces == len(gate_offs) and pieces * GATE_W == w_out.shape[0]
    return pl.pallas_call(
        functools.partial(_out_proj_kernel, pieces=pieces),
        grid=(m // tm,),
        in_specs=[pl.BlockSpec((tm, d), lambda i: (i, 0))] + o_specs + g_specs + [
            pl.BlockSpec(w_out.shape, lambda i: (0, 0), pipeline_mode=pl.Buffered(1)),
            pl.BlockSpec((1, d), lambda i: (0, 0))],
        out_specs=pl.BlockSpec((tm, d), lambda i: (i, 0)),
        out_shape=jax.ShapeDtypeStruct((m, d), F32),
        compiler_params=_cparams("parallel"),
        name="out_proj",
    )(x, *o_args, *([h] * pieces), w_out, g_post.reshape(1, d))


def _layernorm(x, g, b):
    xc = x - jnp.mean(x, axis=-1, keepdims=True)
    y = xc * lax.rsqrt(jnp.mean(xc * xc, axis=-1, keepdims=True) + EPS)
    return y * g + b


def _cmix_prompt_kernel(cu_ref, cv_ref, lg_ref, lb_ref, ws_ref, bs_ref, o_ref):
    v = _layernorm(cv_ref[...], lg_ref[...], lb_ref[...]).astype(BF16)
    tril = _iota((C_CHUNK, C_CHUNK), 1) <= _iota((C_CHUNK, C_CHUNK), 0)
    for g in range(C_GROUPS):
        cols = slice(g * C_GDIM, (g + 1) * C_GDIM)
        w = jnp.where(tril, ws_ref[g], 0.0).astype(BF16)
        for c in range(cu_ref.shape[0] // C_CHUNK):
            rows = slice(c * C_CHUNK, (c + 1) * C_CHUNK)
            o_ref[rows, cols] = cu_ref[rows, cols] * (_dot(w, v[rows, cols]) + bs_ref[:, cols])


def cmix_prompt(h, ln_g, ln_b, ws, bs, tm=512):
    m = h.shape[0]
    bs_rows = jnp.repeat(bs.T, C_GDIM, axis=1)
    vec = pl.BlockSpec((1, C_WIDTH), lambda i: (0, 0))
    return pl.pallas_call(
        _cmix_prompt_kernel,
        grid=(m // tm,),
        in_specs=[pl.BlockSpec((tm, C_WIDTH), lambda i: (i, O_CU // C_WIDTH)),
                  pl.BlockSpec((tm, C_WIDTH), lambda i: (i, O_CV // C_WIDTH)),
                  vec, vec,
                  pl.BlockSpec(ws.shape, lambda i: (0, 0, 0)),
                  pl.BlockSpec(bs_rows.shape, lambda i: (0, 0))],
        out_specs=pl.BlockSpec((tm, C_WIDTH), lambda i: (i, 0)),
        out_shape=jax.ShapeDtypeStruct((m, C_WIDTH), F32),
        compiler_params=_cparams("parallel"),
        name="cmix_prompt",
    )(h, h, ln_g.reshape(1, -1), ln_b.reshape(1, -1), ws, bs_rows)


def _cmix_sample_kernel(cu_ref, cv_ref, lg_ref, lb_ref, w_ref, bs_ref, o_ref, v_ref):
    v = _layernorm(cv_ref[...], lg_ref[...], lb_ref[...])
    v_ref[...] = v
    t = v.shape[1]
    i = _iota((t, C_WIDTH), 0)
    acc = jnp.broadcast_to(bs_ref[...][None], v.shape)
    for j in range(t):
        wj = jnp.where(i >= j, w_ref[j], 0.0)
        acc = acc + wj[None] * v[:, j:j + 1, :]
    o_ref[...] = cu_ref[...] * acc


def cmix_sample(h3, ln_g, ln_b, ws, bs):
    n, t, _ = h3.shape
    w_cols = jnp.repeat(ws[:, :t, :t].transpose(2, 1, 0), C_GDIM, axis=2)
    bs_rows = jnp.repeat(bs[:, :t].T, C_GDIM, axis=1)
    vec = pl.BlockSpec((1, 1, C_WIDTH), lambda i: (0, 0, 0))
    shape = jax.ShapeDtypeStruct((n, t, C_WIDTH), F32)
    blk = pl.BlockSpec((n, t, C_WIDTH), lambda i: (0, 0, 0))
    return pl.pallas_call(
        _cmix_sample_kernel,
        grid=(1,),
        in_specs=[pl.BlockSpec((n, t, C_WIDTH), lambda i: (0, 0, O_CU // C_WIDTH)),
                  pl.BlockSpec((n, t, C_WIDTH), lambda i: (0, 0, O_CV // C_WIDTH)),
                  vec, vec,
                  pl.BlockSpec(w_cols.shape, lambda i: (0, 0, 0)),
                  pl.BlockSpec(bs_rows.shape, lambda i: (0, 0))],
        out_specs=[blk, blk],
        out_shape=[shape, shape],
        compiler_params=_cparams("arbitrary"),
        name="cmix_sample",
    )(h3, h3, ln_g.reshape(1, 1, -1), ln_b.reshape(1, 1, -1), w_cols, bs_rows)


def _top_blocks(scores, allowed, axis):
    nb = scores.shape[axis]
    bidx = _iota(scores.shape, axis).astype(F32)
    cur = jnp.where(allowed, scores, NEG)
    sel = jnp.zeros(scores.shape, F32)
    for _ in range(D_TOPK):
        mx = cur.max(axis, keepdims=True)
        first = jnp.where(cur == mx, bidx, float(nb)).min(axis, keepdims=True)
        pick = bidx == first
        sel = jnp.where(pick, jnp.where(allowed, 1.0, sel), sel)
        cur = jnp.where(pick, BELOW_NEG, cur)
    return sel


MOBA_LANE_HEADS = 4


def _moba_prompt_kernel(q_ref, k_ref, v_ref, o_ref, km_s, kb_s, vt_s, bias_s):
    nb = kb_s.shape[0]
    g = D_HEADS // D_KV
    nq = g * D_BLOCK
    qb = pl.program_id(2)

    @pl.when(qb == 0)
    def _():
        for b in range(nb):
            rows = slice(b * D_BLOCK, (b + 1) * D_BLOCK)
            kblk = k_ref[rows, :]
            km_s[b:b + 1, :] = jnp.mean(kblk, axis=0, keepdims=True)
            kb_s[b] = kblk.astype(BF16)
            vt_s[b] = v_ref[rows, :].T.astype(BF16)

    qf = jnp.concatenate([q_ref[:, gi * HEAD_DIM:(gi + 1) * HEAD_DIM] for gi in range(g)], axis=0) * ATTN_SCALE
    q = qf.astype(BF16)
    scores = _dot_nt(km_s[...], qf, precision=lax.Precision.HIGHEST)
    sel = _top_blocks(scores, _iota((nb, nq), 0) < qb, axis=0)
    bias_s[...] = jnp.where(sel > 0.5, 0.0, NEG)

    gl = MOBA_LANE_HEADS
    width = gl * D_BLOCK
    causal = _iota((D_BLOCK, width), 0) <= _mod(_iota((D_BLOCK, width), 1), D_BLOCK)
    for part in range(g // gl):
        lanes = slice(part * width, (part + 1) * width)
        qp = q[part * width:(part + 1) * width]
        s = jnp.where(causal, _dot_nt(kb_s[qb], qp), NEG)
        m = s.max(0, keepdims=True)
        p = jnp.exp(s - m)
        l = p.sum(0, keepdims=True)
        acc = _dot(vt_s[qb], p.astype(BF16))

        def body(j, carry, qp=qp, lanes=lanes):
            m, l, acc = carry
            s = _dot_nt(kb_s[j], qp) + bias_s[pl.ds(j, 1), lanes]
            m_new = jnp.maximum(m, s.max(0, keepdims=True))
            alpha = jnp.exp(m - m_new)
            p = jnp.exp(s - m_new)
            return m_new, alpha * l + p.sum(0, keepdims=True), alpha * acc + _dot(vt_s[j], p.astype(BF16))

        m, l, acc = lax.fori_loop(0, qb, body, (m, l, acc))
        out = acc / l
        for gi in range(gl):
            col = (part * gl + gi) * HEAD_DIM
            o_ref[:, col:col + HEAD_DIM] = out[:, gi * D_BLOCK:(gi + 1) * D_BLOCK].T


def moba_prompt(h3):
    n, seq, _ = h3.shape
    g = D_HEADS // D_KV
    gw = g * HEAD_DIM
    nb = seq // D_BLOCK
    return pl.pallas_call(
        _moba_prompt_kernel,
        grid=(n, D_KV, nb),
        in_specs=[pl.BlockSpec((None, D_BLOCK, gw), lambda b, hh, i: (b, i, O_DQ // gw + hh)),
                  pl.BlockSpec((None, seq, HEAD_DIM), lambda b, hh, i: (b, 0, O_DK // HEAD_DIM + hh)),
                  pl.BlockSpec((None, seq, HEAD_DIM), lambda b, hh, i: (b, 0, O_DV // HEAD_DIM + hh))],
        out_specs=pl.BlockSpec((None, D_BLOCK, gw), lambda b, hh, i: (b, i, hh)),
        out_shape=jax.ShapeDtypeStruct((n, seq, D_WIDTH), F32),
        scratch_shapes=[pltpu.VMEM((nb, HEAD_DIM), F32),
                        pltpu.VMEM((nb, D_BLOCK, HEAD_DIM), BF16),
                        pltpu.VMEM((nb, HEAD_DIM, D_BLOCK), BF16),
                        pltpu.VMEM((nb, g * D_BLOCK), F32)],
        compiler_params=_cparams("parallel", "parallel", "arbitrary"),
        name="moba_prompt",
    )(h3, h3, h3)


KV_PAGES_BLOCK = 4


def _kv_pages_kernel(k_ref, v_ref, o_ref):
    for pg in range(o_ref.shape[0]):
        rows = slice(pg * PAGE_SIZE, (pg + 1) * PAGE_SIZE)
        for kv, src in enumerate((k_ref, v_ref)):
            for hh in range(D_KV):
                o_ref[pg, kv, hh] = src[rows, hh * HEAD_DIM:(hh + 1) * HEAD_DIM]


def kv_pages(h3):
    n, seq, _ = h3.shape
    kw = D_KV * HEAD_DIM
    pb = KV_PAGES_BLOCK
    return pl.pallas_call(
        _kv_pages_kernel,
        grid=(n, seq // (pb * PAGE_SIZE)),
        in_specs=[pl.BlockSpec((None, pb * PAGE_SIZE, kw), lambda b, p: (b, p, O_DK // kw)),
                  pl.BlockSpec((None, pb * PAGE_SIZE, kw), lambda b, p: (b, p, O_DV // kw))],
        out_specs=pl.BlockSpec((None, pb, 2, D_KV, PAGE_SIZE, HEAD_DIM), lambda b, p: (b, p, 0, 0, 0, 0)),
        out_shape=jax.ShapeDtypeStruct((n, seq // PAGE_SIZE, 2, D_KV, PAGE_SIZE, HEAD_DIM), F32),
        compiler_params=_cparams("parallel", "parallel"),
        name="kv_pages",
    )(h3, h3)


MOBA_CHUNK_PAGES = 16
MOBA_VMEM_LIMIT = 52 * 1024 * 1024


def _moba_decode_kernel(pt_ref, q_ref, kn_ref, vn_ref, pool_ref, o_ref, buf, sem, lg_s, km_s, p_s, bias_s, e_s):
    seq = pl.program_id(0)
    n_seq = pl.num_programs(0)
    n_pages = buf.shape[1]
    cp = MOBA_CHUNK_PAGES
    ppb = D_BLOCK // PAGE_SIZE
    n_blk = n_pages // ppb
    g = D_HEADS // D_KV
    nq = q_ref.shape[1]
    lane_blk = (nq, PAGE_SIZE)

    def page_copies(s, slot):
        return [pltpu.make_async_copy(pool_ref.at[pt_ref[s, pg]], buf.at[slot, pg], sem.at[slot])
                for pg in range(n_pages)]

    @pl.when(seq == 0)
    def _():
        for c in page_copies(0, 0):
            c.start()
        e_s[...] = jnp.where(_div(_iota(e_s.shape, 1), PAGE_SIZE) == _iota(e_s.shape, 0), 1.0, 0.0).astype(BF16)

    @pl.when(seq + 1 < n_seq)
    def _():
        for c in page_copies(seq + 1, (seq + 1) % 2):
            c.start()

    slot = seq % 2
    for c in page_copies(seq, slot):
        c.wait()

    for hk in range(D_KV):
        qf = q_ref[hk] * ATTN_SCALE
        q = qf.astype(BF16)
        for c in range(n_pages // cp):
            ks = [buf[slot, c * cp + i, 0, hk] for i in range(cp)]
            lg_s[hk, :, c * cp * PAGE_SIZE:(c + 1) * cp * PAGE_SIZE] = _dot_nt(
                q, jnp.concatenate([k.astype(BF16) for k in ks], axis=0))
            sums = [k.sum(axis=0, keepdims=True) for k in ks]
            means = [functools.reduce(lambda a, b: a + b, sums[b * ppb:(b + 1) * ppb]) * (1.0 / D_BLOCK)
                     for b in range(cp // ppb)]
            km_s[hk, c * (cp // ppb):(c + 1) * (cp // ppb), :] = jnp.concatenate(means, axis=0)

        scores = _dot_nt(qf, km_s[hk], precision=lax.Precision.HIGHEST)
        sel = _top_blocks(scores, _iota(scores.shape, 1) >= 0, axis=1)
        bias_s[...] = jnp.where(_dot(sel.astype(BF16), e_s[...]) > 0.5, 0.0, NEG)

        own = _dot_nt(q, kn_ref[hk].astype(BF16))
        own = jnp.where(_iota(own.shape, 1) <= _div(_iota(own.shape, 0), g), own, NEG)

        def masked(pg):
            b = pg // ppb
            return lg_s[hk, :, pg * PAGE_SIZE:(pg + 1) * PAGE_SIZE] + bias_s[:, b * PAGE_SIZE:(b + 1) * PAGE_SIZE]

        mx = masked(0)
        for pg in range(1, n_pages):
            mx = jnp.maximum(mx, masked(pg))
        m = jnp.maximum(mx.max(-1, keepdims=True), own.max(-1, keepdims=True))
        m_rep = jnp.broadcast_to(m, lane_blk)
        tot = jnp.zeros(lane_blk, F32)
        for pg in range(n_pages):
            p = jnp.exp(masked(pg) - m_rep)
            tot = tot + p
            p_s[hk, :, pg * PAGE_SIZE:(pg + 1) * PAGE_SIZE] = p.astype(BF16)
        po = jnp.exp(own - m)
        den = tot.sum(-1, keepdims=True) + po.sum(-1, keepdims=True)

        acc = _dot(po.astype(BF16), vn_ref[hk].astype(BF16))
        for c in range(n_pages // cp):
            v = jnp.concatenate([buf[slot, c * cp + i, 1, hk].astype(BF16) for i in range(cp)], axis=0)
            acc = acc + _dot(p_s[hk, :, c * cp * PAGE_SIZE:(c + 1) * cp * PAGE_SIZE], v)
        o_ref[hk] = acc / den


def moba_decode(pool, page_table, q, k_new, v_new):
    n, n_pages = page_table.shape
    assert (n_pages * PAGE_SIZE) % D_BLOCK == 0 and n_pages % MOBA_CHUNK_PAGES == 0
    nq, t = q.shape[2], k_new.shape[2]
    n_blk = n_pages * PAGE_SIZE // D_BLOCK
    keys = n_pages * PAGE_SIZE

    def per_seq(r):
        return pl.BlockSpec((None, D_KV, r, HEAD_DIM), lambda b, pt: (b, 0, 0, 0))

    grid_spec = pltpu.PrefetchScalarGridSpec(
        num_scalar_prefetch=1,
        grid=(n,),
        in_specs=[per_seq(nq), per_seq(t), per_seq(t), pl.BlockSpec(memory_space=pl.ANY)],
        out_specs=per_seq(nq),
        scratch_shapes=[pltpu.VMEM((2, n_pages) + pool.shape[1:], F32),
                        pltpu.SemaphoreType.DMA((2,)),
                        pltpu.VMEM((D_KV, nq, keys), F32),
                        pltpu.VMEM((D_KV, n_blk, HEAD_DIM), F32),
                        pltpu.VMEM((D_KV, nq, keys), BF16),
                        pltpu.VMEM((nq, n_blk * PAGE_SIZE), F32),
                        pltpu.VMEM((n_blk, n_blk * PAGE_SIZE), BF16)])
    return pl.pallas_call(
        _moba_decode_kernel,
        grid_spec=grid_spec,
        out_shape=jax.ShapeDtypeStruct((n, D_KV, nq, HEAD_DIM), F32),
        compiler_params=pltpu.CompilerParams(dimension_semantics=("arbitrary",), vmem_limit_bytes=MOBA_VMEM_LIMIT),
        name="moba_decode",
    )(page_table, q, k_new, v_new, pool)


def even_layer_prompt(x, w_in, w_out, g_pre, g_post, sink):
    n, s, d = x.shape
    x2 = x.reshape(n * s, d)
    h = norm_proj(x2, g_pre, w_in, tm=1024)
    h3 = h.reshape(n, s, IN_EVEN)
    a_out = attn_a_prompt(h3).reshape(n * s, A_WIDTH)
    b_out = attn_b_prompt(h3, sink).reshape(n * s, B_WIDTH)
    y = out_proj(x2, [a_out, b_out], h, [E_AG, E_BG, E_BG + GATE_W], w_out, g_post)
    rows = [kv_rows(h3, E_AK + gi * A_WIDTH, E_AV + gi * A_WIDTH, A_HEADS, win) for gi, (win, _) in enumerate(A_GROUPS)]
    rows.append(kv_rows(h3, E_BK, E_BV, B_KV, B_WINDOW))
    return y.reshape(n, s, d), rows


def even_layer_sample(x, bufs_a, buf_b, w_in, w_out, g_pre, g_post, sink):
    n, t, d = x.shape
    x2 = x.reshape(n * t, d)
    h = norm_proj(x2, g_pre, w_in, tm=n * t)
    stats, new_bufs = [], []
    for gi, (_, dil) in enumerate(A_GROUPS):
        cols = slice(gi * A_WIDTH, (gi + 1) * A_WIDTH)
        q = h[:, E_AQ:E_AK][:, cols].reshape(n, t * A_HEADS, HEAD_DIM)
        k = h[:, E_AK:E_AV][:, cols].reshape(n, t * A_HEADS, HEAD_DIM)
        v = h[:, E_AV:E_AG][:, cols].reshape(n, t * A_HEADS, HEAD_DIM)
        buf, o, m, l = decode_attn(bufs_a[gi], q, k, v, A_HEADS, A_HEADS, dil)
        stats.append((o, m, l))
        new_bufs.append(buf)
    a_out = merge_groups(stats).reshape(n * t, A_WIDTH)
    q = h[:, E_BQ:E_BK].reshape(n, t * B_HEADS, HEAD_DIM)
    k = h[:, E_BK:E_BV].reshape(n, t * B_KV, HEAD_DIM)
    v = h[:, E_BV:E_BG].reshape(n, t * B_KV, HEAD_DIM)
    sink_rows = jnp.broadcast_to(jnp.tile(sink, t)[:, None], (t * B_HEADS, HEAD_DIM))
    buf, b_out = decode_attn(buf_b, q, k, v, B_KV, B_HEADS, 1, sink=sink_rows)
    new_bufs.append(buf)
    y = out_proj(x2, [a_out, b_out.reshape(n * t, B_WIDTH)], h, [E_AG, E_BG, E_BG + GATE_W], w_out, g_post)
    return y.reshape(n, t, d), new_bufs


def odd_layer_prompt(x, w_in, w_out, g_pre, g_post, ln_g, ln_b, ws, bs):
    n, s, d = x.shape
    x2 = x.reshape(n * s, d)
    h = norm_proj(x2, g_pre, w_in, tm=1024)
    h3 = h.reshape(n, s, IN_ODD)
    c_out = cmix_prompt(h, ln_g, ln_b, ws, bs)
    d_out = moba_prompt(h3).reshape(n * s, D_WIDTH)
    gates = [O_CG, O_CG + GATE_W, O_DG, O_DG + GATE_W]
    y = out_proj(x2, [c_out, d_out], h, gates, w_out, g_post)
    return y.reshape(n, s, d), kv_pages(h3)


def odd_layer_sample(x, pool, page_table, w_in, w_out, g_pre, g_post, ln_g, ln_b, ws, bs):
    n, t, d = x.shape
    g = D_HEADS // D_KV
    x2 = x.reshape(n * t, d)
    h = norm_proj(x2, g_pre, w_in, tm=n * t)
    c_out, v_rows = cmix_sample(h.reshape(n, t, IN_ODD), ln_g, ln_b, ws, bs)
    q = h[:, O_DQ:O_DK].reshape(n, t, D_KV, g, HEAD_DIM).transpose(0, 2, 1, 3, 4).reshape(n, D_KV, t * g, HEAD_DIM)
    k = h[:, O_DK:O_DV].reshape(n, t, D_KV, HEAD_DIM).transpose(0, 2, 1, 3)
    v = h[:, O_DV:O_DG].reshape(n, t, D_KV, HEAD_DIM).transpose(0, 2, 1, 3)
    o = moba_decode(pool, page_table, q, k, v)
    d_out = o.reshape(n, D_KV, t, g, HEAD_DIM).transpose(0, 2, 1, 3, 4).reshape(n * t, D_WIDTH)
    gates = [O_CG, O_CG + GATE_W, O_DG, O_DG + GATE_W]
    y = out_proj(x2, [c_out.reshape(n * t, C_WIDTH), d_out], h, gates, w_out, g_post)
    return y.reshape(n, t, d), jnp.stack([k, v], axis=1), v_rows


def kernel(x_prompt, x_sample, cache_a1, cache_a2, cache_a3, cache_b, cache_d, page_table, norm_pre_even, norm_post_even, w_in_even, w_out_even, sink_b, norm_pre_odd, norm_post_odd, w_in_odd, w_out_odd, c_ln_g, c_ln_b, c_ws, c_bs):
    yp, ys = x_prompt, x_sample
    caches_a = (cache_a1, cache_a2, cache_a3)
    a_p, a_s = [[] for _ in range(A_NG)], [[] for _ in range(A_NG)]
    b_p, b_s, d_p, d_s, c_s = [], [], [], [], []
    for layer in range(DEPTH):
        li = layer // 2
        if layer % 2 == 0:
            w_in, w_out = w_in_even[li].astype(BF16), w_out_even[li].astype(BF16)
            yp, rows = even_layer_prompt(yp, w_in, w_out, norm_pre_even[li], norm_post_even[li], sink_b[li])
            ys, bufs = even_layer_sample(ys, [c[li] for c in caches_a], cache_b[li], w_in, w_out,
                                         norm_pre_even[li], norm_post_even[li], sink_b[li])
            for gi in range(A_NG):
                a_p[gi].append(rows[gi])
                a_s[gi].append(bufs[gi])
            b_p.append(rows[A_NG])
            b_s.append(bufs[A_NG])
        else:
            w_in, w_out = w_in_odd[li].astype(BF16), w_out_odd[li].astype(BF16)
            yp, kvp = odd_layer_prompt(yp, w_in, w_out, norm_pre_odd[li], norm_post_odd[li],
                                       c_ln_g[li], c_ln_b[li], c_ws[li], c_bs[li])
            ys, kvs, vrows = odd_layer_sample(ys, cache_d[li], page_table, w_in, w_out, norm_pre_odd[li],
                                              norm_post_odd[li], c_ln_g[li], c_ln_b[li], c_ws[li], c_bs[li])
            d_p.append(kvp)
            d_s.append(kvs)
            c_s.append(vrows)
    return (yp, ys, jnp.stack(a_p[0]), jnp.stack(a_s[0]), jnp.stack(a_p[1]), jnp.stack(a_s[1]),
            jnp.stack(a_p[2]), jnp.stack(a_s[2]), jnp.stack(b_p), jnp.stack(b_s),
            jnp.stack(d_p), jnp.stack(d_s), jnp.stack(c_s))
```

```python
import functools

import jax
import jax.numpy as jnp
from jax import lax
from jax.experimental import pallas as pl
from jax.experimental.pallas import tpu as pltpu

F32 = jnp.float32
BF16 = jnp.bfloat16

D_MODEL = 2048
DEPTH = 2
DEC_SEQ = 8
PAST_LEN = 8192
PAGE_SIZE = 128
HEAD_DIM = 128
ATTN_SCALE = HEAD_DIM ** -0.5
A_GROUPS = ((128, 1), (512, 4), (2048, 16))
A_NG = len(A_GROUPS)
A_HEADS = 4
A_QKV = A_NG * A_HEADS * HEAD_DIM
A_WIDTH = A_HEADS * HEAD_DIM
B_HEADS = 8
B_KV = 2
B_WINDOW = 128
B_WIDTH = B_HEADS * HEAD_DIM
C_CHUNK = 128
C_GROUPS = 8
C_GDIM = 128
C_WIDTH = C_GROUPS * C_GDIM
D_HEADS = 8
D_KV = 2
D_BLOCK = 256
D_TOPK = 3
D_WIDTH = D_HEADS * HEAD_DIM
EPS = 1e-6
NEG = -1e30
BELOW_NEG = -3e38

E_AQ, E_AK, E_AV = 0, A_QKV, 2 * A_QKV
E_AG = 3 * A_QKV
E_BQ = E_AG + A_WIDTH
E_BK = E_BQ + B_WIDTH
E_BV = E_BK + B_KV * HEAD_DIM
E_BG = E_BV + B_KV * HEAD_DIM
IN_EVEN = E_BG + B_WIDTH
O_CU, O_CV, O_CG = 0, C_WIDTH, 2 * C_WIDTH
O_DQ = 3 * C_WIDTH
O_DK = O_DQ + D_WIDTH
O_DV = O_DK + D_KV * HEAD_DIM
O_DG = O_DV + D_KV * HEAD_DIM
IN_ODD = O_DG + D_WIDTH

BAND = 128
GATE_W = 512
VMEM_LIMIT = 48 * 1024 * 1024


def _cparams(*sem):
    return pltpu.CompilerParams(dimension_semantics=sem, vmem_limit_bytes=VMEM_LIMIT)


def _dot_nt(a, b, precision=None):
    return lax.dot_general(a, b, (((1,), (1,)), ((), ())), preferred_element_type=F32, precision=precision)


def _dot(a, b):
    return jnp.dot(a, b, preferred_element_type=F32)


def _iota(shape, dim):
    return lax.broadcasted_iota(jnp.int32, shape, dim)


def _div(x, d):
    assert d & (d - 1) == 0
    return x >> (d.bit_length() - 1)


def _mod(x, d):
    assert d & (d - 1) == 0
    return x & (d - 1)


def _norm_proj_kernel(x_ref, g_ref, w_ref, o_ref, xn_ref):
    @pl.when(pl.program_id(1) == 0)
    def _():
        x = x_ref[...]
        r = lax.rsqrt(jnp.mean(x * x, axis=-1, keepdims=True) + EPS)
        xn_ref[...] = (x * r * g_ref[...]).astype(BF16)

    o_ref[...] = _dot(xn_ref[...], w_ref[...])


def norm_proj(x, g, w, tm, tn=512):
    m, d = x.shape
    n = w.shape[1]
    return pl.pallas_call(
        _norm_proj_kernel,
        grid=(m // tm, n // tn),
        in_specs=[pl.BlockSpec((tm, d), lambda i, j: (i, 0)),
                  pl.BlockSpec((1, d), lambda i, j: (0, 0)),
                  pl.BlockSpec((d, tn), lambda i, j: (0, j))],
        out_specs=pl.BlockSpec((tm, tn), lambda i, j: (i, j)),
        out_shape=jax.ShapeDtypeStruct((m, n), F32),
        scratch_shapes=[pltpu.VMEM((tm, d), BF16)],
        compiler_params=_cparams("parallel", "arbitrary"),
        name="norm_proj",
    )(x, g.reshape(1, d), w)


def _masked_attention(q, k, v, visible):
    s = jnp.where(visible, _dot_nt(q, k), NEG)
    m = s.max(-1, keepdims=True)
    p = jnp.exp(s - m)
    return _dot(p.astype(BF16), v), m, p.sum(-1, keepdims=True)


def _rows(ref, start, count, stride):
    if stride == 1:
        return ref[pl.ds(start, count), :]
    return ref[pl.ds(start, count, stride=stride), :]


A_TILE = 256


def _attn_a_prompt_kernel(q1, k1, v1, q2, k2, v2, q3, k3, v3, o_ref, o1_s, m1_s, l1_s, o2_s, m2_s, l2_s):
    seq = q1.shape[0]
    wide = (BAND, HEAD_DIM)
    tile = A_TILE
    qi = _iota((tile, tile + BAND), 0)
    kc = _iota((tile, tile + BAND), 1)
    vis_next = jnp.logical_and(kc >= qi, kc <= qi + BAND)
    qi0 = _iota((tile, tile), 0)
    kc0 = _iota((tile, tile), 1)
    vis_first = jnp.logical_and(kc0 <= qi0, kc0 >= qi0 - BAND)

    def banded(qr, kr, vr, base, stride, t):
        first = base + stride * t * tile
        q = (_rows(qr, first, tile, stride) * ATTN_SCALE).astype(BF16)
        if t == 0:
            k, v, vis = _rows(kr, first, tile, stride), _rows(vr, first, tile, stride), vis_first
        else:
            back = first - stride * BAND
            k, v, vis = _rows(kr, back, tile + BAND, stride), _rows(vr, back, tile + BAND, stride), vis_next
        return _masked_attention(q, k.astype(BF16), v.astype(BF16), vis)

    for t in range(seq // tile):
        rows = slice(t * tile, (t + 1) * tile)
        o, m, l = banded(q1, k1, v1, 0, 1, t)
        o1_s[rows, :] = o
        m1_s[rows, :] = jnp.broadcast_to(m, (tile, HEAD_DIM))
        l1_s[rows, :] = jnp.broadcast_to(l, (tile, HEAD_DIM))

    d2 = A_GROUPS[1][1]
    sub2 = seq // d2
    for r in range(d2):
        for t in range(sub2 // tile):
            rows = slice(t * tile, (t + 1) * tile)
            o, m, l = banded(q2, k2, v2, r, d2, t)
            o2_s[r, rows, :] = o
            m2_s[r, rows, :] = jnp.broadcast_to(m, (tile, HEAD_DIM))
            l2_s[r, rows, :] = jnp.broadcast_to(l, (tile, HEAD_DIM))

    d3 = A_GROUPS[2][1]
    ratio = d3 // d2
    pack = tile // BAND
    same = _div(qi0, BAND) == _div(kc0, BAND)
    vis_pack = jnp.logical_and(same, _mod(kc0, BAND) <= _mod(qi0, BAND))
    for r in range(d3):
        if r % pack == 0:
            def packed(ref, scale=None, r=r):
                x = jnp.concatenate([_rows(ref, r + u, BAND, d3) for u in range(pack)], axis=0)
                return (x if scale is None else x * scale).astype(BF16)
            o3p, m3p, l3p = _masked_attention(packed(q3, ATTN_SCALE), packed(k3), packed(v3), vis_pack)
        part = slice((r % pack) * BAND, (r % pack + 1) * BAND)
        o3 = o3p[part]
        m3 = jnp.broadcast_to(m3p[part], wide)
        l3 = jnp.broadcast_to(l3p[part], wide)
        o1 = o1_s[pl.ds(r, BAND, stride=d3), :]
        m1 = m1_s[pl.ds(r, BAND, stride=d3), :]
        l1 = l1_s[pl.ds(r, BAND, stride=d3), :]
        r2, off2 = r % d2, r // d2
        o2 = o2_s[r2, pl.ds(off2, BAND, stride=ratio), :]
        m2 = m2_s[r2, pl.ds(off2, BAND, stride=ratio), :]
        l2 = l2_s[r2, pl.ds(off2, BAND, stride=ratio), :]
        mmax = jnp.maximum(jnp.maximum(m1, m2), m3)
        e1, e2, e3 = jnp.exp(m1 - mmax), jnp.exp(m2 - mmax), jnp.exp(m3 - mmax)
        num = o1 * e1 + o2 * e2 + o3 * e3
        den = l1 * e1 + l2 * e2 + l3 * e3
        o_ref[pl.ds(r, BAND, stride=d3), :] = num / den


def attn_a_prompt(h3):
    n, seq, _ = h3.shape
    assert seq == A_GROUPS[2][0] and all(w // d == BAND for w, d in A_GROUPS)

    def spec(off, gi):
        blk = off // HEAD_DIM + gi * A_HEADS
        return pl.BlockSpec((None, seq, HEAD_DIM), lambda b, hh: (b, 0, blk + hh))

    in_specs = []
    for gi in range(A_NG):
        in_specs += [spec(E_AQ, gi), spec(E_AK, gi), spec(E_AV, gi)]
    d2 = A_GROUPS[1][1]
    stat = pltpu.VMEM((seq, HEAD_DIM), F32)
    stat2 = pltpu.VMEM((d2, seq // d2, HEAD_DIM), F32)
    return pl.pallas_call(
        _attn_a_prompt_kernel,
        grid=(n, A_HEADS),
        in_specs=in_specs,
        out_specs=pl.BlockSpec((None, seq, HEAD_DIM), lambda b, hh: (b, 0, hh)),
        out_shape=jax.ShapeDtypeStruct((n, seq, A_WIDTH), F32),
        scratch_shapes=[stat, stat, stat, stat2, stat2, stat2],
        compiler_params=_cparams("parallel", "parallel"),
        name="attn_a_prompt",
    )(*([h3] * 9))


def _attn_b_prompt_kernel(sink_ref, q_ref, k_ref, v_ref, o_ref, kb_s, vt_s):
    seq = q_ref.shape[0]
    g = B_HEADS // B_KV
    nq = g * BAND
    nblk = seq // BAND
    kvh = pl.program_id(1)
    for b in range(nblk):
        rows = slice(b * BAND, (b + 1) * BAND)
        kb_s[rows, :] = k_ref[rows, :].astype(BF16)
        vt_s[b] = v_ref[rows, :].T.astype(BF16)
    head = _div(_iota((1, nq), 1), BAND)
    sink = jnp.zeros((1, nq), F32)
    for gi in range(g):
        sink = jnp.where(head == gi, sink_ref[kvh * g + gi], sink)
    qi = _mod(_iota((2 * BAND, nq), 1), BAND)
    key = _iota((2 * BAND, nq), 0)
    band = jnp.logical_and(key >= qi, key <= qi + BAND)
    for b in range(nblk):
        rows = slice(b * BAND, (b + 1) * BAND)
        qb = q_ref[rows, :]
        q = jnp.concatenate([qb[:, gi * HEAD_DIM:(gi + 1) * HEAD_DIM] for gi in range(g)], axis=0)
        q = (q * ATTN_SCALE).astype(BF16)
        if b == 0:
            s = jnp.where(band[BAND:], _dot_nt(kb_s[rows, :], q), NEG)
        else:
            s = jnp.where(band, _dot_nt(kb_s[(b - 1) * BAND:(b + 1) * BAND, :], q), NEG)
        m = s.max(0, keepdims=True)
        pf = jnp.exp(s - m)
        l = pf.sum(0, keepdims=True)
        p = pf.astype(BF16)
        if b == 0:
            o = _dot(vt_s[0], p)
        else:
            o = _dot(vt_s[b - 1], p[:BAND]) + _dot(vt_s[b], p[BAND:])
        m2 = jnp.maximum(m, sink)
        a = jnp.exp(m - m2)
        out = o * (a / (l * a + jnp.exp(sink - m2)))
        for gi in range(g):
            o_ref[rows, gi * HEAD_DIM:(gi + 1) * HEAD_DIM] = out[:, gi * BAND:(gi + 1) * BAND].T


def attn_b_prompt(h3, sink):
    n, seq, _ = h3.shape
    gw = (B_HEADS // B_KV) * HEAD_DIM
    return pl.pallas_call(
        _attn_b_prompt_kernel,
        grid=(n, B_KV),
        in_specs=[pl.BlockSpec(memory_space=pltpu.SMEM),
                  pl.BlockSpec((None, seq, gw), lambda b, hh: (b, 0, E_BQ // gw + hh)),
                  pl.BlockSpec((None, seq, HEAD_DIM), lambda b, hh: (b, 0, E_BK // HEAD_DIM + hh)),
                  pl.BlockSpec((None, seq, HEAD_DIM), lambda b, hh: (b, 0, E_BV // HEAD_DIM + hh))],
        out_specs=pl.BlockSpec((None, seq, gw), lambda b, hh: (b, 0, hh)),
        out_shape=jax.ShapeDtypeStruct((n, seq, B_WIDTH), F32),
        scratch_shapes=[pltpu.VMEM((seq, HEAD_DIM), BF16), pltpu.VMEM((seq // BAND, HEAD_DIM, BAND), BF16)],
        compiler_params=_cparams("parallel", "parallel"),
        name="attn_b_prompt",
    )(sink, h3, h3, h3)


KV_ROWS_BLOCK = 512


def _kv_rows_kernel(k_ref, v_ref, o_ref, *, heads):
    npos = k_ref.shape[0]
    for kv, src in enumerate((k_ref, v_ref)):
        for hh in range(heads):
            o_ref[pl.ds(kv * heads + hh, npos, stride=2 * heads), :] = src[:, hh * HEAD_DIM:(hh + 1) * HEAD_DIM]


def kv_rows(h3, k_off, v_off, heads, window):
    n, seq, _ = h3.shape
    w = min(window, seq)
    hw = heads * HEAD_DIM
    npos = min(w, KV_ROWS_BLOCK)
    first = (seq - w) // npos
    rows = npos * 2 * heads
    out = pl.pallas_call(
        functools.partial(_kv_rows_kernel, heads=heads),
        grid=(n, w // npos),
        in_specs=[pl.BlockSpec((None, npos, hw), lambda b, c: (b, first + c, k_off // hw)),
                  pl.BlockSpec((None, npos, hw), lambda b, c: (b, first + c, v_off // hw))],
        out_specs=pl.BlockSpec((None, rows, HEAD_DIM), lambda b, c: (b, c, 0)),
        out_shape=jax.ShapeDtypeStruct((n, w * 2 * heads, HEAD_DIM), F32),
        compiler_params=_cparams("parallel", "parallel"),
        name="kv_rows",
    )(h3, h3)
    return out.reshape(n, w, 2, heads, HEAD_DIM)


DEC_CHUNK = 2048
DEC_UNROLL = 4


def _decode_kernel(*refs, heads, q_heads, dil, use_sink):
    if use_sink:
        sink_ref, q_ref, kn_ref, vn_ref, nf_ref, c_ref, nc_ref, out_ref, bias_s, biasn_s, l_s = refs
    else:
        q_ref, kn_ref, vn_ref, nf_ref, c_ref, nc_ref, out_ref, m_ref, l_ref, bias_s, biasn_s, l_s = refs
    slots = 2 * heads
    seqs, rows = c_ref.shape[:2]
    lb = rows // slots
    nq = q_ref.shape[1]
    t_new = kn_ref.shape[1] // heads
    per_kv = q_heads // heads
    chunk = min(DEC_CHUNK, rows)
    fresh = nf_ref.shape[1]

    @pl.when(pl.program_id(0) == 0)
    def _():
        row = _iota((nq, rows), 0)
        col = _iota((nq, rows), 1)
        t = _div(row, q_heads)
        kvh = _div(_mod(row, q_heads), per_kv)
        pos = _div(col, slots)
        ok = (_mod(col, slots) == kvh) & (pos >= t) & (_mod(lb + t - pos, dil) == 0)
        bias_s[...] = jnp.where(ok, 0.0, NEG)
        rown = _iota((nq, t_new * heads), 0)
        coln = _iota((nq, t_new * heads), 1)
        tn = _div(rown, q_heads)
        kvhn = _div(_mod(rown, q_heads), per_kv)
        tk = _div(coln, heads)
        okn = (_mod(coln, heads) == kvhn) & (tk <= tn) & (_mod(tn - tk, dil) == 0)
        biasn_s[...] = jnp.where(okn, 0.0, NEG)

    def one_seq(b, carry):
        q = (q_ref[b] * ATTN_SCALE).astype(BF16)
        sn = _dot_nt(q, kn_ref[b].astype(BF16)) + biasn_s[...]
        m = sn.max(-1, keepdims=True)
        for c in range(rows // chunk):
            sl = slice(c * chunk, (c + 1) * chunk)
            s = _dot_nt(q, c_ref[b, sl, :].astype(BF16)) + bias_s[:, sl]
            l_s[b, :, sl] = s
            m = jnp.maximum(m, s.max(-1, keepdims=True))
        pn = jnp.exp(sn - m)
        den = pn.sum(-1, keepdims=True)
        o = _dot(pn.astype(BF16), vn_ref[b].astype(BF16))
        for c in range(rows // chunk):
            sl = slice(c * chunk, (c + 1) * chunk)
            p = jnp.exp(l_s[b, :, sl] - m)
            den = den + p.sum(-1, keepdims=True)
            o = o + _dot(pltpu.roll(p, heads, axis=1).astype(BF16), c_ref[b, sl, :].astype(BF16))

        if use_sink:
            sink = sink_ref[...]
            m2 = jnp.maximum(m, sink)
            a = jnp.exp(m - m2)
            out_ref[b] = o * (a / (den * a + jnp.exp(sink - m2)))
        else:
            out_ref[b] = o
            m_ref[b] = jnp.broadcast_to(m, (nq, HEAD_DIM))
            l_ref[b] = jnp.broadcast_to(den, (nq, HEAD_DIM))

        nc_ref[b, pl.ds(0, rows - fresh), :] = c_ref[b, pl.ds(fresh, rows - fresh), :]
        nc_ref[b, pl.ds(rows - fresh, fresh), :] = nf_ref[b]
        return carry

    if seqs == 1:
        one_seq(0, 0)
    else:
        lax.fori_loop(0, seqs, one_seq, 0, unroll=min(seqs, DEC_UNROLL))


DEC_BLOCK_BYTES = 4 * 1024 * 1024


def decode_attn(cache, q, k_new, v_new, heads, q_heads, dil, sink=None):
    n, lb = cache.shape[:2]
    rows = lb * 2 * heads
    flat = cache.reshape(n, rows, HEAD_DIM)
    k4 = k_new.reshape(n, -1, 1, heads, HEAD_DIM)
    v4 = v_new.reshape(n, -1, 1, heads, HEAD_DIM)
    new_flat = jnp.concatenate([k4, v4], axis=2).reshape(n, -1, HEAD_DIM)
    seqs = 1
    while seqs * 2 * rows * HEAD_DIM * 4 <= DEC_BLOCK_BYTES and n % (seqs * 2) == 0:
        seqs *= 2
    nq = q.shape[1]
    tk = k_new.shape[1]
    use_sink = sink is not None

    def per_seq(r):
        return pl.BlockSpec((seqs, r, HEAD_DIM), lambda b: (b, 0, 0))

    in_specs = [per_seq(nq), per_seq(tk), per_seq(tk), per_seq(new_flat.shape[1]), per_seq(rows)]
    args = [q, k_new, v_new, new_flat, flat]
    stat = jax.ShapeDtypeStruct((n, nq, HEAD_DIM), F32)
    out_shape = [jax.ShapeDtypeStruct(flat.shape, F32), stat]
    out_specs = [per_seq(rows), per_seq(nq)]
    if use_sink:
        in_specs = [pl.BlockSpec((nq, HEAD_DIM), lambda b: (0, 0))] + in_specs
        args = [sink] + args
    else:
        out_shape += [stat, stat]
        out_specs += [per_seq(nq), per_seq(nq)]
    outs = pl.pallas_call(
        functools.partial(_decode_kernel, heads=heads, q_heads=q_heads, dil=dil, use_sink=use_sink),
        grid=(n // seqs,),
        in_specs=in_specs,
        out_specs=out_specs,
        out_shape=out_shape,
        scratch_shapes=[pltpu.VMEM((nq, rows), F32), pltpu.VMEM((nq, tk), F32), pltpu.VMEM((seqs, nq, rows), F32)],
        compiler_params=_cparams("arbitrary"),
        name="decode_attn",
    )(*args)
    return (outs[0].reshape(cache.shape),) + tuple(outs[1:])


def _merge_kernel(*refs):
    o_ref = refs[-1]
    stats = [refs[3 * g:3 * g + 3] for g in range(A_NG)]
    mmax = functools.reduce(jnp.maximum, [m[...] for _, m, _ in stats])
    num = 0.0
    den = 0.0
    for o, m, l in stats:
        e = jnp.exp(m[...] - mmax)
        num = num + o[...] * e
        den = den + l[...] * e
    o_ref[...] = num / den


def merge_groups(stats):
    flat = [a.reshape(-1, HEAD_DIM) for st in stats for a in st]
    out = pl.pallas_call(_merge_kernel, out_shape=jax.ShapeDtypeStruct(flat[0].shape, F32),
                         compiler_params=pltpu.CompilerParams(vmem_limit_bytes=VMEM_LIMIT),
                         name="merge_groups")(*flat)
    return out


def _out_proj_kernel(*refs, pieces):
    x_ref = refs[0]
    o_refs = refs[1:1 + pieces]
    g_refs = refs[1 + pieces:1 + 2 * pieces]
    w_ref, gp_ref, y_ref = refs[1 + 2 * pieces:]
    acc = None
    for p in range(pieces):
        g = g_refs[p][...]
        mix = (o_refs[p][...] * (g * (1.0 / (1.0 + jnp.exp(-g))))).astype(BF16)
        d = _dot(mix, w_ref[p * GATE_W:(p + 1) * GATE_W, :])
        acc = d if acc is None else acc + d
    r = lax.rsqrt(jnp.mean(acc * acc, axis=-1, keepdims=True) + EPS)
    y_ref[...] = x_ref[...] + acc * r * gp_ref[...]


def out_proj(x, outs, h, gate_offs, w_out, g_post, tm=512):
    m, d = x.shape
    tm = min(tm, m)
    o_args, o_specs = [], []
    for arr in outs:
        for c in range(arr.shape[1] // GATE_W):
            o_args.append(arr)
            o_specs.append(pl.BlockSpec((tm, GATE_W), lambda i, c=c: (i, c)))
    g_specs = [pl.BlockSpec((tm, GATE_W), lambda i, c=off // GATE_W: (i, c)) for off in gate_offs]
    pieces = len(o_args)
    assert pieces == len(gate_offs) and pieces * GATE_W == w_out.shape[0]
    return pl.pallas_call(
        functools.partial(_out_proj_kernel, pieces=pieces),
        grid=(m // tm,),
        in_specs=[pl.BlockSpec((tm, d), lambda i: (i, 0))] + o_specs + g_specs + [
            pl.BlockSpec(w_out.shape, lambda i: (0, 0), pipeline_mode=pl.Buffered(1)),
            pl.BlockSpec((1, d), lambda i: (0, 0))],
        out_specs=pl.BlockSpec((tm, d), lambda i: (i, 0)),
        out_shape=jax.ShapeDtypeStruct((m, d), F32),
        compiler_params=_cparams("parallel"),
        name="out_proj",
    )(x, *o_args, *([h] * pieces), w_out, g_post.reshape(1, d))


def _layernorm(x, g, b):
    xc = x - jnp.mean(x, axis=-1, keepdims=True)
    y = xc * lax.rsqrt(jnp.mean(xc * xc, axis=-1, keepdims=True) + EPS)
    return y * g + b


def _cmix_prompt_kernel(cu_ref, cv_ref, lg_ref, lb_ref, ws_ref, bs_ref, o_ref):
    v = _layernorm(cv_ref[...], lg_ref[...], lb_ref[...]).astype(BF16)
    tril = _iota((C_CHUNK, C_CHUNK), 1) <= _iota((C_CHUNK, C_CHUNK), 0)
    for g in range(C_GROUPS):
        cols = slice(g * C_GDIM, (g + 1) * C_GDIM)
        w = jnp.where(tril, ws_ref[g], 0.0).astype(BF16)
        for c in range(cu_ref.shape[0] // C_CHUNK):
            rows = slice(c * C_CHUNK, (c + 1) * C_CHUNK)
            o_ref[rows, cols] = cu_ref[rows, cols] * (_dot(w, v[rows, cols]) + bs_ref[:, cols])


def cmix_prompt(h, ln_g, ln_b, ws, bs, tm=512):
    m = h.shape[0]
    bs_rows = jnp.repeat(bs.T, C_GDIM, axis=1)
    vec = pl.BlockSpec((1, C_WIDTH), lambda i: (0, 0))
    return pl.pallas_call(
        _cmix_prompt_kernel,
        grid=(m // tm,),
        in_specs=[pl.BlockSpec((tm, C_WIDTH), lambda i: (i, O_CU // C_WIDTH)),
                  pl.BlockSpec((tm, C_WIDTH), lambda i: (i, O_CV // C_WIDTH)),
                  vec, vec,
                  pl.BlockSpec(ws.shape, lambda i: (0, 0, 0)),
                  pl.BlockSpec(bs_rows.shape, lambda i: (0, 0))],
        out_specs=pl.BlockSpec((tm, C_WIDTH), lambda i: (i, 0)),
        out_shape=jax.ShapeDtypeStruct((m, C_WIDTH), F32),
        compiler_params=_cparams("parallel"),
        name="cmix_prompt",
    )(h, h, ln_g.reshape(1, -1), ln_b.reshape(1, -1), ws, bs_rows)


def _cmix_sample_kernel(cu_ref, cv_ref, lg_ref, lb_ref, w_ref, bs_ref, o_ref, v_ref):
    v = _layernorm(cv_ref[...], lg_ref[...], lb_ref[...])
    v_ref[...] = v
    t = v.shape[1]
    i = _iota((t, C_WIDTH), 0)
    acc = jnp.broadcast_to(bs_ref[...][None], v.shape)
    for j in range(t):
        wj = jnp.where(i >= j, w_ref[j], 0.0)
        acc = acc + wj[None] * v[:, j:j + 1, :]
    o_ref[...] = cu_ref[...] * acc


def cmix_sample(h3, ln_g, ln_b, ws, bs):
    n, t, _ = h3.shape
    w_cols = jnp.repeat(ws[:, :t, :t].transpose(2, 1, 0), C_GDIM, axis=2)
    bs_rows = jnp.repeat(bs[:, :t].T, C_GDIM, axis=1)
    vec = pl.BlockSpec((1, 1, C_WIDTH), lambda i: (0, 0, 0))
    shape = jax.ShapeDtypeStruct((n, t, C_WIDTH), F32)
    blk = pl.BlockSpec((n, t, C_WIDTH), lambda i: (0, 0, 0))
    return pl.pallas_call(
        _cmix_sample_kernel,
        grid=(1,),
        in_specs=[pl.BlockSpec((n, t, C_WIDTH), lambda i: (0, 0, O_CU // C_WIDTH)),
                  pl.BlockSpec((n, t, C_WIDTH), lambda i: (0, 0, O_CV // C_WIDTH)),
                  vec, vec,
                  pl.BlockSpec(w_cols.shape, lambda i: (0, 0, 0)),
                  pl.BlockSpec(bs_rows.shape, lambda i: (0, 0))],
        out_specs=[blk, blk],
        out_shape=[shape, shape],
        compiler_params=_cparams("arbitrary"),
        name="cmix_sample",
    )(h3, h3, ln_g.reshape(1, 1, -1), ln_b.reshape(1, 1, -1), w_cols, bs_rows)


def _top_blocks(scores, allowed, axis):
    nb = scores.shape[axis]
    bidx = _iota(scores.shape, axis).astype(F32)
    cur = jnp.where(allowed, scores, NEG)
    sel = jnp.zeros(scores.shape, F32)
    for _ in range(D_TOPK):
        mx = cur.max(axis, keepdims=True)
        first = jnp.where(cur == mx, bidx, float(nb)).min(axis, keepdims=True)
        pick = bidx == first
        sel = jnp.where(pick, jnp.where(allowed, 1.0, sel), sel)
        cur = jnp.where(pick, BELOW_NEG, cur)
    return sel


MOBA_LANE_HEADS = 4


def _moba_prompt_kernel(q_ref, k_ref, v_ref, o_ref, km_s, kb_s, vt_s, bias_s):
    nb = kb_s.shape[0]
    g = D_HEADS // D_KV
    nq = g * D_BLOCK
    qb = pl.program_id(2)

    @pl.when(qb == 0)
    def _():
        for b in range(nb):
            rows = slice(b * D_BLOCK, (b + 1) * D_BLOCK)
            kblk = k_ref[rows, :]
            km_s[b:b + 1, :] = jnp.mean(kblk, axis=0, keepdims=True)
            kb_s[b] = kblk.astype(BF16)
            vt_s[b] = v_ref[rows, :].T.astype(BF16)

    qf = jnp.concatenate([q_ref[:, gi * HEAD_DIM:(gi + 1) * HEAD_DIM] for gi in range(g)], axis=0) * ATTN_SCALE
    q = qf.astype(BF16)
    scores = _dot_nt(km_s[...], qf, precision=lax.Precision.HIGHEST)
    sel = _top_blocks(scores, _iota((nb, nq), 0) < qb, axis=0)
    bias_s[...] = jnp.where(sel > 0.5, 0.0, NEG)

    gl = MOBA_LANE_HEADS
    width = gl * D_BLOCK
    causal = _iota((D_BLOCK, width), 0) <= _mod(_iota((D_BLOCK, width), 1), D_BLOCK)
    for part in range(g // gl):
        lanes = slice(part * width, (part + 1) * width)
        qp = q[part * width:(part + 1) * width]
        s = jnp.where(causal, _dot_nt(kb_s[qb], qp), NEG)
        m = s.max(0, keepdims=True)
        p = jnp.exp(s - m)
        l = p.sum(0, keepdims=True)
        acc = _dot(vt_s[qb], p.astype(BF16))

        def body(j, carry, qp=qp, lanes=lanes):
            m, l, acc = carry
            s = _dot_nt(kb_s[j], qp) + bias_s[pl.ds(j, 1), lanes]
            m_new = jnp.maximum(m, s.max(0, keepdims=True))
            alpha = jnp.exp(m - m_new)
            p = jnp.exp(s - m_new)
            return m_new, alpha * l + p.sum(0, keepdims=True), alpha * acc + _dot(vt_s[j], p.astype(BF16))

        m, l, acc = lax.fori_loop(0, qb, body, (m, l, acc))
        out = acc / l
        for gi in range(gl):
            col = (part * gl + gi) * HEAD_DIM
            o_ref[:, col:col + HEAD_DIM] = out[:, gi * D_BLOCK:(gi + 1) * D_BLOCK].T


def moba_prompt(h3):
    n, seq, _ = h3.shape
    g = D_HEADS // D_KV
    gw = g * HEAD_DIM
    nb = seq // D_BLOCK
    return pl.pallas_call(
        _moba_prompt_kernel,
        grid=(n, D_KV, nb),
        in_specs=[pl.BlockSpec((None, D_BLOCK, gw), lambda b, hh, i: (b, i, O_DQ // gw + hh)),
                  pl.BlockSpec((None, seq, HEAD_DIM), lambda b, hh, i: (b, 0, O_DK // HEAD_DIM + hh)),
                  pl.BlockSpec((None, seq, HEAD_DIM), lambda b, hh, i: (b, 0, O_DV // HEAD_DIM + hh))],
        out_specs=pl.BlockSpec((None, D_BLOCK, gw), lambda b, hh, i: (b, i, hh)),
        out_shape=jax.ShapeDtypeStruct((n, seq, D_WIDTH), F32),
        scratch_shapes=[pltpu.VMEM((nb, HEAD_DIM), F32),
                        pltpu.VMEM((nb, D_BLOCK, HEAD_DIM), BF16),
                        pltpu.VMEM((nb, HEAD_DIM, D_BLOCK), BF16),
                        pltpu.VMEM((nb, g * D_BLOCK), F32)],
        compiler_params=_cparams("parallel", "parallel", "arbitrary"),
        name="moba_prompt",
    )(h3, h3, h3)


KV_PAGES_BLOCK = 4


def _kv_pages_kernel(k_ref, v_ref, o_ref):
    for pg in range(o_ref.shape[0]):
        rows = slice(pg * PAGE_SIZE, (pg + 1) * PAGE_SIZE)
        for kv, src in enumerate((k_ref, v_ref)):
            for hh in range(D_KV):
                o_ref[pg, kv, hh] = src[rows, hh * HEAD_DIM:(hh + 1) * HEAD_DIM]


def kv_pages(h3):
    n, seq, _ = h3.shape
    kw = D_KV * HEAD_DIM
    pb = KV_PAGES_BLOCK
    return pl.pallas_call(
        _kv_pages_kernel,
        grid=(n, seq // (pb * PAGE_SIZE)),
        in_specs=[pl.BlockSpec((None, pb * PAGE_SIZE, kw), lambda b, p: (b, p, O_DK // kw)),
                  pl.BlockSpec((None, pb * PAGE_SIZE, kw), lambda b, p: (b, p, O_DV // kw))],
        out_specs=pl.BlockSpec((None, pb, 2, D_KV, PAGE_SIZE, HEAD_DIM), lambda b, p: (b, p, 0, 0, 0, 0)),
        out_shape=jax.ShapeDtypeStruct((n, seq // PAGE_SIZE, 2, D_KV, PAGE_SIZE, HEAD_DIM), F32),
        compiler_params=_cparams("parallel", "parallel"),
        name="kv_pages",
    )(h3, h3)


MOBA_CHUNK_PAGES = 16
MOBA_VMEM_LIMIT = 52 * 1024 * 1024


def _moba_decode_kernel(pt_ref, q_ref, kn_ref, vn_ref, pool_ref, o_ref, buf, sem, lg_s, km_s, p_s, bias_s, e_s):
    seq = pl.program_id(0)
    n_seq = pl.num_programs(0)
    n_pages = buf.shape[1]
    cp = MOBA_CHUNK_PAGES
    ppb = D_BLOCK // PAGE_SIZE
    n_blk = n_pages // ppb
    g = D_HEADS // D_KV
    nq = q_ref.shape[1]
    lane_blk = (nq, PAGE_SIZE)

    def page_copies(s, slot):
        return [pltpu.make_async_copy(pool_ref.at[pt_ref[s, pg]], buf.at[slot, pg], sem.at[slot])
                for pg in range(n_pages)]

    @pl.when(seq == 0)
    def _():
        for c in page_copies(0, 0):
            c.start()
        e_s[...] = jnp.where(_div(_iota(e_s.shape, 1), PAGE_SIZE) == _iota(e_s.shape, 0), 1.0, 0.0).astype(BF16)

    @pl.when(seq + 1 < n_seq)
    def _():
        for c in page_copies(seq + 1, (seq + 1) % 2):
            c.start()

    slot = seq % 2
    for c in page_copies(seq, slot):
        c.wait()

    for hk in range(D_KV):
        qf = q_ref[hk] * ATTN_SCALE
        q = qf.astype(BF16)
        for c in range(n_pages // cp):
            ks = [buf[slot, c * cp + i, 0, hk] for i in range(cp)]
            lg_s[hk, :, c * cp * PAGE_SIZE:(c + 1) * cp * PAGE_SIZE] = _dot_nt(
                q, jnp.concatenate([k.astype(BF16) for k in ks], axis=0))
            sums = [k.sum(axis=0, keepdims=True) for k in ks]
            means = [functools.reduce(lambda a, b: a + b, sums[b * ppb:(b + 1) * ppb]) * (1.0 / D_BLOCK)
                     for b in range(cp // ppb)]
            km_s[hk, c * (cp // ppb):(c + 1) * (cp // ppb), :] = jnp.concatenate(means, axis=0)

        scores = _dot_nt(qf, km_s[hk], precision=lax.Precision.HIGHEST)
        sel = _top_blocks(scores, _iota(scores.shape, 1) >= 0, axis=1)
        bias_s[...] = jnp.where(_dot(sel.astype(BF16), e_s[...]) > 0.5, 0.0, NEG)

        own = _dot_nt(q, kn_ref[hk].astype(BF16))
        own = jnp.where(_iota(own.shape, 1) <= _div(_iota(own.shape, 0), g), own, NEG)

        def masked(pg):
            b = pg // ppb
            return lg_s[hk, :, pg * PAGE_SIZE:(pg + 1) * PAGE_SIZE] + bias_s[:, b * PAGE_SIZE:(b + 1) * PAGE_SIZE]

        mx = masked(0)
        for pg in range(1, n_pages):
            mx = jnp.maximum(mx, masked(pg))
        m = jnp.maximum(mx.max(-1, keepdims=True), own.max(-1, keepdims=True))
        m_rep = jnp.broadcast_to(m, lane_blk)
        tot = jnp.zeros(lane_blk, F32)
        for pg in range(n_pages):
            p = jnp.exp(masked(pg) - m_rep)
            tot = tot + p
            p_s[hk, :, pg * PAGE_SIZE:(pg + 1) * PAGE_SIZE] = p.astype(BF16)
        po = jnp.exp(own - m)
        den = tot.sum(-1, keepdims=True) + po.sum(-1, keepdims=True)

        acc = _dot(po.astype(BF16), vn_ref[hk].astype(BF16))
        for c in range(n_pages // cp):
            v = jnp.concatenate([buf[slot, c * cp + i, 1, hk].astype(BF16) for i in range(cp)], axis=0)
            acc = acc + _dot(p_s[hk, :, c * cp * PAGE_SIZE:(c + 1) * cp * PAGE_SIZE], v)
        o_ref[hk] = acc / den


def moba_decode(pool, page_table, q, k_new, v_new):
    n, n_pages = page_table.shape
    assert (n_pages * PAGE_SIZE) % D_BLOCK == 0 and n_pages % MOBA_CHUNK_PAGES == 0
    nq, t = q.shape[2], k_new.shape[2]
    n_blk = n_pages * PAGE_SIZE // D_BLOCK
    keys = n_pages * PAGE_SIZE

    def per_seq(r):
        return pl.BlockSpec((None, D_KV, r, HEAD_DIM), lambda b, pt: (b, 0, 0, 0))

    grid_spec = pltpu.PrefetchScalarGridSpec(
        num_scalar_prefetch=1,
        grid=(n,),
        in_specs=[per_seq(nq), per_seq(t), per_seq(t), pl.BlockSpec(memory_space=pl.ANY)],
        out_specs=per_seq(nq),
        scratch_shapes=[pltpu.VMEM((2, n_pages) + pool.shape[1:], F32),
                        pltpu.SemaphoreType.DMA((2,)),
                        pltpu.VMEM((D_KV, nq, keys), F32),
                        pltpu.VMEM((D_KV, n_blk, HEAD_DIM), F32),
                        pltpu.VMEM((D_KV, nq, keys), BF16),
                        pltpu.VMEM((nq, n_blk * PAGE_SIZE), F32),
                        pltpu.VMEM((n_blk, n_blk * PAGE_SIZE), BF16)])
    return pl.pallas_call(
        _moba_decode_kernel,
        grid_spec=grid_spec,
        out_shape=jax.ShapeDtypeStruct((n, D_KV, nq, HEAD_DIM), F32),
        compiler_params=pltpu.CompilerParams(dimension_semantics=("arbitrary",), vmem_limit_bytes=MOBA_VMEM_LIMIT),
        name="moba_decode",
    )(page_table, q, k_new, v_new, pool)


def even_layer_prompt(x, w_in, w_out, g_pre, g_post, sink):
    n, s, d = x.shape
    x2 = x.reshape(n * s, d)
    h = norm_proj(x2, g_pre, w_in, tm=1024)
    h3 = h.reshape(n, s, IN_EVEN)
    a_out = attn_a_prompt(h3).reshape(n * s, A_WIDTH)
    b_out = attn_b_prompt(h3, sink).reshape(n * s, B_WIDTH)
    y = out_proj(x2, [a_out, b_out], h, [E_AG, E_BG, E_BG + GATE_W], w_out, g_post)
    rows = [kv_rows(h3, E_AK + gi * A_WIDTH, E_AV + gi * A_WIDTH, A_HEADS, win) for gi, (win, _) in enumerate(A_GROUPS)]
    rows.append(kv_rows(h3, E_BK, E_BV, B_KV, B_WINDOW))
    return y.reshape(n, s, d), rows


def even_layer_sample(x, bufs_a, buf_b, w_in, w_out, g_pre, g_post, sink):
    n, t, d = x.shape
    x2 = x.reshape(n * t, d)
    h = norm_proj(x2, g_pre, w_in, tm=n * t)
    stats, new_bufs = [], []
    for gi, (_, dil) in enumerate(A_GROUPS):
        cols = slice(gi * A_WIDTH, (gi + 1) * A_WIDTH)
        q = h[:, E_AQ:E_AK][:, cols].reshape(n, t * A_HEADS, HEAD_DIM)
        k = h[:, E_AK:E_AV][:, cols].reshape(n, t * A_HEADS, HEAD_DIM)
        v = h[:, E_AV:E_AG][:, cols].reshape(n, t * A_HEADS, HEAD_DIM)
        buf, o, m, l = decode_attn(bufs_a[gi], q, k, v, A_HEADS, A_HEADS, dil)
        stats.append((o, m, l))
        new_bufs.append(buf)
    a_out = merge_groups(stats).reshape(n * t, A_WIDTH)
    q = h[:, E_BQ:E_BK].reshape(n, t * B_HEADS, HEAD_DIM)
    k = h[:, E_BK:E_BV].reshape(n, t * B_KV, HEAD_DIM)
    v = h[:, E_BV:E_BG].reshape(n, t * B_KV, HEAD_DIM)
    sink_rows = jnp.broadcast_to(jnp.tile(sink, t)[:, None], (t * B_HEADS, HEAD_DIM))
    buf, b_out = decode_attn(buf_b, q, k, v, B_KV, B_HEADS, 1, sink=sink_rows)
    new_bufs.append(buf)
    y = out_proj(x2, [a_out, b_out.reshape(n * t, B_WIDTH)], h, [E_AG, E_BG, E_BG + GATE_W], w_out, g_post)
    return y.reshape(n, t, d), new_bufs


def odd_layer_prompt(x, w_in, w_out, g_pre, g_post, ln_g, ln_b, ws, bs):
    n, s, d = x.shape
    x2 = x.reshape(n * s, d)
    h = norm_proj(x2, g_pre, w_in, tm=1024)
    h3 = h.reshape(n, s, IN_ODD)
    c_out = cmix_prompt(h, ln_g, ln_b, ws, bs)
    d_out = moba_prompt(h3).reshape(n * s, D_WIDTH)
    gates = [O_CG, O_CG + GATE_W, O_DG, O_DG + GATE_W]
    y = out_proj(x2, [c_out, d_out], h, gates, w_out, g_post)
    return y.reshape(n, s, d), kv_pages(h3)


def odd_layer_sample(x, pool, page_table, w_in, w_out, g_pre, g_post, ln_g, ln_b, ws, bs):
    n, t, d = x.shape
    g = D_HEADS // D_KV
    x2 = x.reshape(n * t, d)
    h = norm_proj(x2, g_pre, w_in, tm=n * t)
    c_out, v_rows = cmix_sample(h.reshape(n, t, IN_ODD), ln_g, ln_b, ws, bs)
    q = h[:, O_DQ:O_DK].reshape(n, t, D_KV, g, HEAD_DIM).transpose(0, 2, 1, 3, 4).reshape(n, D_KV, t * g, HEAD_DIM)
    k = h[:, O_DK:O_DV].reshape(n, t, D_KV, HEAD_DIM).transpose(0, 2, 1, 3)
    v = h[:, O_DV:O_DG].reshape(n, t, D_KV, HEAD_DIM).transpose(0, 2, 1, 3)
    o = moba_decode(pool, page_table, q, k, v)
    d_out = o.reshape(n, D_KV, t, g, HEAD_DIM).transpose(0, 2, 1, 3, 4).reshape(n * t, D_WIDTH)
    gates = [O_CG, O_CG + GATE_W, O_DG, O_DG + GATE_W]
    y = out_proj(x2, [c_out.reshape(n * t, C_WIDTH), d_out], h, gates, w_out, g_post)
    return y.reshape(n, t, d), jnp.stack([k, v], axis=1), v_rows


def kernel(x_prompt, x_sample, cache_a1, cache_a2, cache_a3, cache_b, cache_d, page_table, norm_pre_even, norm_post_even, w_in_even, w_out_even, sink_b, norm_pre_odd, norm_post_odd, w_in_odd, w_out_odd, c_ln_g, c_ln_b, c_ws, c_bs):
    yp, ys = x_prompt, x_sample
    caches_a = (cache_a1, cache_a2, cache_a3)
    a_p, a_s = [[] for _ in range(A_NG)], [[] for _ in range(A_NG)]
    b_p, b_s, d_p, d_s, c_s = [], [], [], [], []
    for layer in range(DEPTH):
        li = layer // 2
        if layer % 2 == 0:
            w_in, w_out = w_in_even[li].astype(BF16), w_out_even[li].astype(BF16)
            yp, rows = even_layer_prompt(yp, w_in, w_out, norm_pre_even[li], norm_post_even[li], sink_b[li])
            ys, bufs = even_layer_sample(ys, [c[li] for c in caches_a], cache_b[li], w_in, w_out,
                                         norm_pre_even[li], norm_post_even[li], sink_b[li])
            for gi in range(A_NG):
                a_p[gi].append(rows[gi])
                a_s[gi].append(bufs[gi])
            b_p.append(rows[A_NG])
            b_s.append(bufs[A_NG])
        else:
            w_in, w_out = w_in_odd[li].astype(BF16), w_out_odd[li].astype(BF16)
            yp, kvp = odd_layer_prompt(yp, w_in, w_out, norm_pre_odd[li], norm_post_odd[li],
                                       c_ln_g[li], c_ln_b[li], c_ws[li], c_bs[li])
            ys, kvs, vrows = odd_layer_sample(ys, cache_d[li], page_table, w_in, w_out, norm_pre_odd[li],
                                              norm_post_odd[li], c_ln_g[li], c_ln_b[li], c_ws[li], c_bs[li])
            d_p.append(kvp)
            d_s.append(kvs)
            c_s.append(vrows)
    return (yp, ys, jnp.stack(a_p[0]), jnp.stack(a_s[0]), jnp.stack(a_p[1]), jnp.stack(a_s[1]),
            jnp.stack(a_p[2]), jnp.stack(a_s[2]), jnp.stack(b_p), jnp.stack(b_s),
            jnp.stack(d_p), jnp.stack(d_s), jnp.stack(c_s))
```

```python
import functools

import jax
import jax.numpy as jnp
from jax import lax
from jax.experimental import pallas as pl
from jax.experimental.pallas import tpu as pltpu

F32 = jnp.float32
BF16 = jnp.bfloat16

D_MODEL = 2048
DEPTH = 2
DEC_SEQ = 8
PAST_LEN = 8192
PAGE_SIZE = 128
HEAD_DIM = 128
ATTN_SCALE = HEAD_DIM ** -0.5
A_GROUPS = ((128, 1), (512, 4), (2048, 16))
A_NG = len(A_GROUPS)
A_HEADS = 4
A_QKV = A_NG * A_HEADS * HEAD_DIM
A_WIDTH = A_HEADS * HEAD_DIM
B_HEADS = 8
B_KV = 2
B_WINDOW = 128
B_WIDTH = B_HEADS * HEAD_DIM
C_CHUNK = 128
C_GROUPS = 8
C_GDIM = 128
C_WIDTH = C_GROUPS * C_GDIM
D_HEADS = 8
D_KV = 2
D_BLOCK = 256
D_TOPK = 3
D_WIDTH = D_HEADS * HEAD_DIM
EPS = 1e-6
NEG = -1e30
BELOW_NEG = -3e38

E_AQ, E_AK, E_AV = 0, A_QKV, 2 * A_QKV
E_AG = 3 * A_QKV
E_BQ = E_AG + A_WIDTH
E_BK = E_BQ + B_WIDTH
E_BV = E_BK + B_KV * HEAD_DIM
E_BG = E_BV + B_KV * HEAD_DIM
IN_EVEN = E_BG + B_WIDTH
O_CU, O_CV, O_CG = 0, C_WIDTH, 2 * C_WIDTH
O_DQ = 3 * C_WIDTH
O_DK = O_DQ + D_WIDTH
O_DV = O_DK + D_KV * HEAD_DIM
O_DG = O_DV + D_KV * HEAD_DIM
IN_ODD = O_DG + D_WIDTH

BAND = 128
GATE_W = 512
VMEM_LIMIT = 48 * 1024 * 1024


def _cparams(*sem):
    return pltpu.CompilerParams(dimension_semantics=sem, vmem_limit_bytes=VMEM_LIMIT)


def _dot_nt(a, b, precision=None):
    return lax.dot_general(a, b, (((1,), (1,)), ((), ())), preferred_element_type=F32, precision=precision)


def _dot(a, b):
    return jnp.dot(a, b, preferred_element_type=F32)


def _iota(shape, dim):
    return lax.broadcasted_iota(jnp.int32, shape, dim)


def _div(x, d):
    assert d & (d - 1) == 0
    return x >> (d.bit_length() - 1)


def _mod(x, d):
    assert d & (d - 1) == 0
    return x & (d - 1)


def _norm_proj_kernel(x_ref, g_ref, w_ref, o_ref, xn_ref):
    @pl.when(pl.program_id(1) == 0)
    def _():
        x = x_ref[...]
        r = lax.rsqrt(jnp.mean(x * x, axis=-1, keepdims=True) + EPS)
        xn_ref[...] = (x * r * g_ref[...]).astype(BF16)

    o_ref[...] = _dot(xn_ref[...], w_ref[...])


def norm_proj(x, g, w, tm, tn=512):
    m, d = x.shape
    n = w.shape[1]
    return pl.pallas_call(
        _norm_proj_kernel,
        grid=(m // tm, n // tn),
        in_specs=[pl.BlockSpec((tm, d), lambda i, j: (i, 0)),
                  pl.BlockSpec((1, d), lambda i, j: (0, 0)),
                  pl.BlockSpec((d, tn), lambda i, j: (0, j))],
        out_specs=pl.BlockSpec((tm, tn), lambda i, j: (i, j)),
        out_shape=jax.ShapeDtypeStruct((m, n), F32),
        scratch_shapes=[pltpu.VMEM((tm, d), BF16)],
        compiler_params=_cparams("parallel", "arbitrary"),
        name="norm_proj",
    )(x, g.reshape(1, d), w)


def _masked_attention(q, k, v, visible):
    s = jnp.where(visible, _dot_nt(q, k), NEG)
    m = s.max(-1, keepdims=True)
    p = jnp.exp(s - m)
    return _dot(p.astype(BF16), v), m, p.sum(-1, keepdims=True)


def _rows(ref, start, count, stride):
    if stride == 1:
        return ref[pl.ds(start, count), :]
    return ref[pl.ds(start, count, stride=stride), :]


A_TILE = 256


def _attn_a_prompt_kernel(q1, k1, v1, q2, k2, v2, q3, k3, v3, o_ref, o1_s, m1_s, l1_s, o2_s, m2_s, l2_s):
    seq = q1.shape[0]
    wide = (BAND, HEAD_DIM)
    tile = A_TILE
    qi = _iota((tile, tile + BAND), 0)
    kc = _iota((tile, tile + BAND), 1)
    vis_next = jnp.logical_and(kc >= qi, kc <= qi + BAND)
    qi0 = _iota((tile, tile), 0)
    kc0 = _iota((tile, tile), 1)
    vis_first = jnp.logical_and(kc0 <= qi0, kc0 >= qi0 - BAND)

    def banded(qr, kr, vr, base, stride, t):
        first = base + stride * t * tile
        q = (_rows(qr, first, tile, stride) * ATTN_SCALE).astype(BF16)
        if t == 0:
            k, v, vis = _rows(kr, first, tile, stride), _rows(vr, first, tile, stride), vis_first
        else:
            back = first - stride * BAND
            k, v, vis = _rows(kr, back, tile + BAND, stride), _rows(vr, back, tile + BAND, stride), vis_next
        return _masked_attention(q, k.astype(BF16), v.astype(BF16), vis)

    for t in range(seq // tile):
        rows = slice(t * tile, (t + 1) * tile)
        o, m, l = banded(q1, k1, v1, 0, 1, t)
        o1_s[rows, :] = o
        m1_s[rows, :] = jnp.broadcast_to(m, (tile, HEAD_DIM))
        l1_s[rows, :] = jnp.broadcast_to(l, (tile, HEAD_DIM))

    d2 = A_GROUPS[1][1]
    sub2 = seq // d2
    for r in range(d2):
        for t in range(sub2 // tile):
            rows = slice(t * tile, (t + 1) * tile)
            o, m, l = banded(q2, k2, v2, r, d2, t)
            o2_s[r, rows, :] = o
            m2_s[r, rows, :] = jnp.broadcast_to(m, (tile, HEAD_DIM))
            l2_s[r, rows, :] = jnp.broadcast_to(l, (tile, HEAD_DIM))

    d3 = A_GROUPS[2][1]
    ratio = d3 // d2
    pack = tile // BAND
    same = _div(qi0, BAND) == _div(kc0, BAND)
    vis_pack = jnp.logical_and(same, _mod(kc0, BAND) <= _mod(qi0, BAND))
    for r in range(d3):
        if r % pack == 0:
            def packed(ref, scale=None, r=r):
                x = jnp.concatenate([_rows(ref, r + u, BAND, d3) for u in range(pack)], axis=0)
                return (x if scale is None else x * scale).astype(BF16)
            o3p, m3p, l3p = _masked_attention(packed(q3, ATTN_SCALE), packed(k3), packed(v3), vis_pack)
        part = slice((r % pack) * BAND, (r % pack + 1) * BAND)
        o3 = o3p[part]
        m3 = jnp.broadcast_to(m3p[part], wide)
        l3 = jnp.broadcast_to(l3p[part], wide)
        o1 = o1_s[pl.ds(r, BAND, stride=d3), :]
        m1 = m1_s[pl.ds(r, BAND, stride=d3), :]
        l1 = l1_s[pl.ds(r, BAND, stride=d3), :]
        r2, off2 = r % d2, r // d2
        o2 = o2_s[r2, pl.ds(off2, BAND, stride=ratio), :]
        m2 = m2_s[r2, pl.ds(off2, BAND, stride=ratio), :]
        l2 = l2_s[r2, pl.ds(off2, BAND, stride=ratio), :]
        mmax = jnp.maximum(jnp.maximum(m1, m2), m3)
        e1, e2, e3 = jnp.exp(m1 - mmax), jnp.exp(m2 - mmax), jnp.exp(m3 - mmax)
        num = o1 * e1 + o2 * e2 + o3 * e3
        den = l1 * e1 + l2 * e2 + l3 * e3
        o_ref[pl.ds(r, BAND, stride=d3), :] = num / den


def attn_a_prompt(h3):
    n, seq, _ = h3.shape
    assert seq == A_GROUPS[2][0] and all(w // d == BAND for w, d in A_GROUPS)

    def spec(off, gi):
        blk = off // HEAD_DIM + gi * A_HEADS
        return pl.BlockSpec((None, seq, HEAD_DIM), lambda b, hh: (b, 0, blk + hh))

    in_specs = []
    for gi in range(A_NG):
        in_specs += [spec(E_AQ, gi), spec(E_AK, gi), spec(E_AV, gi)]
    d2 = A_GROUPS[1][1]
    stat = pltpu.VMEM((seq, HEAD_DIM), F32)
    stat2 = pltpu.VMEM((d2, seq // d2, HEAD_DIM), F32)
    return pl.pallas_call(
        _attn_a_prompt_kernel,
        grid=(n, A_HEADS),
        in_specs=in_specs,
        out_specs=pl.BlockSpec((None, seq, HEAD_DIM), lambda b, hh: (b, 0, hh)),
        out_shape=jax.ShapeDtypeStruct((n, seq, A_WIDTH), F32),
        scratch_shapes=[stat, stat, stat, stat2, stat2, stat2],
        compiler_params=_cparams("parallel", "parallel"),
        name="attn_a_prompt",
    )(*([h3] * 9))


def _attn_b_prompt_kernel(sink_ref, q_ref, k_ref, v_ref, o_ref, kb_s, vt_s):
    seq = q_ref.shape[0]
    g = B_HEADS // B_KV
    nq = g * BAND
    nblk = seq // BAND
    kvh = pl.program_id(1)
    for b in range(nblk):
        rows = slice(b * BAND, (b + 1) * BAND)
        kb_s[rows, :] = k_ref[rows, :].astype(BF16)
        vt_s[b] = v_ref[rows, :].T.astype(BF16)
    head = _div(_iota((1, nq), 1), BAND)
    sink = jnp.zeros((1, nq), F32)
    for gi in range(g):
        sink = jnp.where(head == gi, sink_ref[kvh * g + gi], sink)
    qi = _mod(_iota((2 * BAND, nq), 1), BAND)
    key = _iota((2 * BAND, nq), 0)
    band = jnp.logical_and(key >= qi, key <= qi + BAND)
    for b in range(nblk):
        rows = slice(b * BAND, (b + 1) * BAND)
        qb = q_ref[rows, :]
        q = jnp.concatenate([qb[:, gi * HEAD_DIM:(gi + 1) * HEAD_DIM] for gi in range(g)], axis=0)
        q = (q * ATTN_SCALE).astype(BF16)
        if b == 0:
            s = jnp.where(band[BAND:], _dot_nt(kb_s[rows, :], q), NEG)
        else:
            s = jnp.where(band, _dot_nt(kb_s[(b - 1) * BAND:(b + 1) * BAND, :], q), NEG)
        m = s.max(0, keepdims=True)
        pf = jnp.exp(s - m)
        l = pf.sum(0, keepdims=True)
        p = pf.astype(BF16)
        if b == 0:
            o = _dot(vt_s[0], p)
        else:
            o = _dot(vt_s[b - 1], p[:BAND]) + _dot(vt_s[b], p[BAND:])
        m2 = jnp.maximum(m, sink)
        a = jnp.exp(m - m2)
        out = o * (a / (l * a + jnp.exp(sink - m2)))
        for gi in range(g):
            o_ref[rows, gi * HEAD_DIM:(gi + 1) * HEAD_DIM] = out[:, gi * BAND:(gi + 1) * BAND].T


def attn_b_prompt(h3, sink):
    n, seq, _ = h3.shape
    gw = (B_HEADS // B_KV) * HEAD_DIM
    return pl.pallas_call(
        _attn_b_prompt_kernel,
        grid=(n, B_KV),
        in_specs=[pl.BlockSpec(memory_space=pltpu.SMEM),
                  pl.BlockSpec((None, seq, gw), lambda b, hh: (b, 0, E_BQ // gw + hh)),
                  pl.BlockSpec((None, seq, HEAD_DIM), lambda b, hh: (b, 0, E_BK // HEAD_DIM + hh)),
                  pl.BlockSpec((None, seq, HEAD_DIM), lambda b, hh: (b, 0, E_BV // HEAD_DIM + hh))],
        out_specs=pl.BlockSpec((None, seq, gw), lambda b, hh: (b, 0, hh)),
        out_shape=jax.ShapeDtypeStruct((n, seq, B_WIDTH), F32),
        scratch_shapes=[pltpu.VMEM((seq, HEAD_DIM), BF16), pltpu.VMEM((seq // BAND, HEAD_DIM, BAND), BF16)],
        compiler_params=_cparams("parallel", "parallel"),
        name="attn_b_prompt",
    )(sink, h3, h3, h3)


KV_ROWS_BLOCK = 512


def _kv_rows_kernel(k_ref, v_ref, o_ref, *, heads):
    npos = k_ref.shape[0]
    for kv, src in enumerate((k_ref, v_ref)):
        for hh in range(heads):
            o_ref[pl.ds(kv * heads + hh, npos, stride=2 * heads), :] = src[:, hh * HEAD_DIM:(hh + 1) * HEAD_DIM]


def kv_rows(h3, k_off, v_off, heads, window):
    n, seq, _ = h3.shape
    w = min(window, seq)
    hw = heads * HEAD_DIM
    npos = min(w, KV_ROWS_BLOCK)
    first = (seq - w) // npos
    rows = npos * 2 * heads
    out = pl.pallas_call(
        functools.partial(_kv_rows_kernel, heads=heads),
        grid=(n, w // npos),
        in_specs=[pl.BlockSpec((None, npos, hw), lambda b, c: (b, first + c, k_off // hw)),
                  pl.BlockSpec((None, npos, hw), lambda b, c: (b, first + c, v_off // hw))],
        out_specs=pl.BlockSpec((None, rows, HEAD_DIM), lambda b, c: (b, c, 0)),
        out_shape=jax.ShapeDtypeStruct((n, w * 2 * heads, HEAD_DIM), F32),
        compiler_params=_cparams("parallel", "parallel"),
        name="kv_rows",
    )(h3, h3)
    return out.reshape(n, w, 2, heads, HEAD_DIM)


DEC_CHUNK = 2048
DEC_UNROLL = 4


def _decode_kernel(*refs, heads, q_heads, dil, use_sink, ndim=1):
    if use_sink:
        sink_ref, q_ref, kn_ref, vn_ref, nf_ref, c_ref, nc_ref, out_ref, bias_s, biasn_s, l_s, xb_s = refs
    else:
        q_ref, kn_ref, vn_ref, nf_ref, c_ref, nc_ref, out_ref, m_ref, l_ref, bias_s, biasn_s, l_s, xb_s = refs
    slots = 2 * heads
    seqs, rows = c_ref.shape[:2]
    lb = rows // slots
    nq = q_ref.shape[1]
    t_new = kn_ref.shape[1] // heads
    per_kv = q_heads // heads
    chunk = min(DEC_CHUNK, rows)
    fresh = nf_ref.shape[1]

    @pl.when(functools.reduce(jnp.logical_and, [pl.program_id(d) == 0 for d in range(ndim)]))
    def _():
        row = _iota((nq, rows), 0)
        col = _iota((nq, rows), 1)
        t = _div(row, q_heads)
        kvh = _div(_mod(row, q_heads), per_kv)
        pos = _div(col, slots)
        ok = (_mod(col, slots) == kvh) & (pos >= t) & (_mod(lb + t - pos, dil) == 0)
        bias_s[...] = jnp.where(ok, 0.0, NEG)
        rown = _iota((nq, t_new * heads), 0)
        coln = _iota((nq, t_new * heads), 1)
        tn = _div(rown, q_heads)
        kvhn = _div(_mod(rown, q_heads), per_kv)
        tk = _div(coln, heads)
        okn = (_mod(coln, heads) == kvhn) & (tk <= tn) & (_mod(tn - tk, dil) == 0)
        biasn_s[...] = jnp.where(okn, 0.0, NEG)

    def one_seq(b, carry):
        q = (q_ref[b] * ATTN_SCALE).astype(BF16)
        sn = _dot_nt(q, kn_ref[b].astype(BF16)) + biasn_s[...]
        m = sn.max(-1, keepdims=True)
        for c in range(rows // chunk):
            sl = slice(c * chunk, (c + 1) * chunk)
            x = c_ref[b, sl, :]
            if c == 0:
                nc_ref[b, 0:chunk - fresh, :] = x[fresh:]
            else:
                nc_ref[b, c * chunk - fresh:(c + 1) * chunk - fresh, :] = x
            xb = x.astype(BF16)
            xb_s[b, sl, :] = xb
            s = _dot_nt(q, xb) + bias_s[:, sl]
            l_s[b, :, sl] = s
            m = jnp.maximum(m, s.max(-1, keepdims=True))
        pn = jnp.exp(sn - m)
        den = pn.sum(-1, keepdims=True)
        o = _dot(pn.astype(BF16), vn_ref[b].astype(BF16))
        for c in range(rows // chunk):
            sl = slice(c * chunk, (c + 1) * chunk)
            p = jnp.exp(l_s[b, :, sl] - m)
            den = den + p.sum(-1, keepdims=True)
            o = o + _dot(pltpu.roll(p, heads, axis=1).astype(BF16), xb_s[b, sl, :])

        if use_sink:
            sink = sink_ref[...]
            m2 = jnp.maximum(m, sink)
            a = jnp.exp(m - m2)
            out_ref[b] = o * (a / (den * a + jnp.exp(sink - m2)))
        else:
            out_ref[b] = o
            m_ref[b] = jnp.broadcast_to(m, (nq, HEAD_DIM))
            l_ref[b] = jnp.broadcast_to(den, (nq, HEAD_DIM))

        nc_ref[b, pl.ds(rows - fresh, fresh), :] = nf_ref[b]
        return carry

    if seqs == 1:
        one_seq(0, 0)
    else:
        lax.fori_loop(0, seqs, one_seq, 0, unroll=min(seqs, DEC_UNROLL))


DEC_BLOCK_BYTES = 4 * 1024 * 1024


def decode_parts(cache, q, k_new, v_new, heads, q_heads, dil, sink=None, seqs=None, block_index=None, ndim=1):
    n, lb = cache.shape[:2]
    rows = lb * 2 * heads
    flat = cache.reshape(n, rows, HEAD_DIM)
    k4 = k_new.reshape(n, -1, 1, heads, HEAD_DIM)
    v4 = v_new.reshape(n, -1, 1, heads, HEAD_DIM)
    new_flat = jnp.concatenate([k4, v4], axis=2).reshape(n, -1, HEAD_DIM)
    if seqs is None:
        seqs = 1
        while seqs * 2 * rows * HEAD_DIM * 4 <= DEC_BLOCK_BYTES and n % (seqs * 2) == 0:
            seqs *= 2
    if block_index is None:
        block_index = lambda b: b
    nq = q.shape[1]
    tk = k_new.shape[1]
    use_sink = sink is not None

    def per_seq(r):
        return pl.BlockSpec((seqs, r, HEAD_DIM), lambda *ids: (block_index(*ids), 0, 0))

    in_specs = [per_seq(nq), per_seq(tk), per_seq(tk), per_seq(new_flat.shape[1]), per_seq(rows)]
    args = [q, k_new, v_new, new_flat, flat]
    stat = jax.ShapeDtypeStruct((n, nq, HEAD_DIM), F32)
    out_shape = [jax.ShapeDtypeStruct(flat.shape, F32), stat]
    out_specs = [per_seq(rows), per_seq(nq)]
    if use_sink:
        in_specs = [pl.BlockSpec((nq, HEAD_DIM), lambda *ids: (0, 0))] + in_specs
        args = [sink] + args
    else:
        out_shape += [stat, stat]
        out_specs += [per_seq(nq), per_seq(nq)]
    return dict(
        kernel=functools.partial(_decode_kernel, heads=heads, q_heads=q_heads, dil=dil, use_sink=use_sink, ndim=ndim),
        in_specs=in_specs, args=args, out_specs=out_specs, out_shape=out_shape, steps=n // seqs,
        scratch_shapes=[pltpu.VMEM((nq, rows), F32), pltpu.VMEM((nq, tk), F32), pltpu.VMEM((seqs, nq, rows), F32),
                        pltpu.VMEM((seqs, rows, HEAD_DIM), BF16)],
        finish=lambda outs: (outs[0].reshape(cache.shape),) + tuple(outs[1:]))


def decode_attn(cache, q, k_new, v_new, heads, q_heads, dil, sink=None):
    parts = decode_parts(cache, q, k_new, v_new, heads, q_heads, dil, sink)
    outs = pl.pallas_call(
        parts["kernel"],
        grid=(parts["steps"],),
        in_specs=parts["in_specs"],
        out_specs=parts["out_specs"],
        out_shape=parts["out_shape"],
        scratch_shapes=parts["scratch_shapes"],
        compiler_params=_cparams("arbitrary"),
        name="decode_attn",
    )(*parts["args"])
    return parts["finish"](outs)


def _merge_kernel(*refs):
    o_ref = refs[-1]
    stats = [refs[3 * g:3 * g + 3] for g in range(A_NG)]
    mmax = functools.reduce(jnp.maximum, [m[...] for _, m, _ in stats])
    num = 0.0
    den = 0.0
    for o, m, l in stats:
        e = jnp.exp(m[...] - mmax)
        num = num + o[...] * e
        den = den + l[...] * e
    o_ref[...] = num / den


def merge_groups(stats):
    flat = [a.reshape(-1, HEAD_DIM) for st in stats for a in st]
    out = pl.pallas_call(_merge_kernel, out_shape=jax.ShapeDtypeStruct(flat[0].shape, F32),
                         compiler_params=pltpu.CompilerParams(vmem_limit_bytes=VMEM_LIMIT),
                         name="merge_groups")(*flat)
    return out


def _out_proj_kernel(*refs, pieces):
    x_ref = refs[0]
    o_refs = refs[1:1 + pieces]
    g_refs = refs[1 + pieces:1 + 2 * pieces]
    w_ref, gp_ref, y_ref = refs[1 + 2 * pieces:]
    acc = None
    for p in range(pieces):
        g = g_refs[p][...]
        mix = (o_refs[p][...] * (g * (1.0 / (1.0 + jnp.exp(-g))))).astype(BF16)
        d = _dot(mix, w_ref[p * GATE_W:(p + 1) * GATE_W, :])
        acc = d if acc is None else acc + d
    r = lax.rsqrt(jnp.mean(acc * acc, axis=-1, keepdims=True) + EPS)
    y_ref[...] = x_ref[...] + acc * r * gp_ref[...]


def out_proj(x, outs, h, gate_offs, w_out, g_post, tm=512):
    m, d = x.shape
    tm = min(tm, m)
    o_args, o_specs = [], []
    for arr in outs:
        for c in range(arr.shape[1] // GATE_W):
            o_args.append(arr)
            o_specs.append(pl.BlockSpec((tm, GATE_W), lambda i, c=c: (i, c)))
    g_specs = [pl.BlockSpec((tm, GATE_W), lambda i, c=off // GATE_W: (i, c)) for off in gate_offs]
    pieces = len(o_args)
    assert pieces == len(gate_offs) and pieces * GATE_W == w_out.shape[0]
    return pl.pallas_call(
        functools.partial(_out_proj_kernel, pieces=pieces),
        grid=(m // tm,),
        in_specs=[pl.BlockSpec((tm, d), lambda i: (i, 0))] + o_specs + g_specs + [
            pl.BlockSpec(w_out.shape, lambda i: (0, 0), pipeline_mode=pl.Buffered(1)),
            pl.BlockSpec((1, d), lambda i: (0, 0))],
        out_specs=pl.BlockSpec((tm, d), lambda i: (i, 0)),
        out_shape=jax.ShapeDtypeStruct((m, d), F32),
        compiler_params=_cparams("parallel"),
        name="out_proj",
    )(x, *o_args, *([h] * pieces), w_out, g_post.reshape(1, d))


def _layernorm(x, g, b):
    xc = x - jnp.mean(x, axis=-1, keepdims=True)
    y = xc * lax.rsqrt(jnp.mean(xc * xc, axis=-1, keepdims=True) + EPS)
    return y * g + b


def _cmix_prompt_kernel(cu_ref, cv_ref, lg_ref, lb_ref, ws_ref, bs_ref, o_ref):
    v = _layernorm(cv_ref[...], lg_ref[...], lb_ref[...]).astype(BF16)
    tril = _iota((C_CHUNK, C_CHUNK), 1) <= _iota((C_CHUNK, C_CHUNK), 0)
    for g in range(C_GROUPS):
        cols = slice(g * C_GDIM, (g + 1) * C_GDIM)
        w = jnp.where(tril, ws_ref[g], 0.0).astype(BF16)
        for c in range(cu_ref.shape[0] // C_CHUNK):
            rows = slice(c * C_CHUNK, (c + 1) * C_CHUNK)
            o_ref[rows, cols] = cu_ref[rows, cols] * (_dot(w, v[rows, cols]) + bs_ref[:, cols])


def cmix_prompt(h, ln_g, ln_b, ws, bs, tm=512):
    m = h.shape[0]
    bs_rows = jnp.repeat(bs.T, C_GDIM, axis=1)
    vec = pl.BlockSpec((1, C_WIDTH), lambda i: (0, 0))
    return pl.pallas_call(
        _cmix_prompt_kernel,
        grid=(m // tm,),
        in_specs=[pl.BlockSpec((tm, C_WIDTH), lambda i: (i, O_CU // C_WIDTH)),
                  pl.BlockSpec((tm, C_WIDTH), lambda i: (i, O_CV // C_WIDTH)),
                  vec, vec,
                  pl.BlockSpec(ws.shape, lambda i: (0, 0, 0)),
                  pl.BlockSpec(bs_rows.shape, lambda i: (0, 0))],
        out_specs=pl.BlockSpec((tm, C_WIDTH), lambda i: (i, 0)),
        out_shape=jax.ShapeDtypeStruct((m, C_WIDTH), F32),
        compiler_params=_cparams("parallel"),
        name="cmix_prompt",
    )(h, h, ln_g.reshape(1, -1), ln_b.reshape(1, -1), ws, bs_rows)


def _cmix_sample_kernel(cu_ref, cv_ref, lg_ref, lb_ref, w_ref, bs_ref, o_ref, v_ref):
    v = _layernorm(cv_ref[...], lg_ref[...], lb_ref[...])
    v_ref[...] = v
    t = v.shape[1]
    i = _iota((t, C_WIDTH), 0)
    acc = jnp.broadcast_to(bs_ref[...][None], v.shape)
    for j in range(t):
        wj = jnp.where(i >= j, w_ref[j], 0.0)
        acc = acc + wj[None] * v[:, j:j + 1, :]
    o_ref[...] = cu_ref[...] * acc


def cmix_sample(h3, ln_g, ln_b, ws, bs):
    n, t, _ = h3.shape
    w_cols = jnp.repeat(ws[:, :t, :t].transpose(2, 1, 0), C_GDIM, axis=2)
    bs_rows = jnp.repeat(bs[:, :t].T, C_GDIM, axis=1)
    vec = pl.BlockSpec((1, 1, C_WIDTH), lambda i: (0, 0, 0))
    shape = jax.ShapeDtypeStruct((n, t, C_WIDTH), F32)
    blk = pl.BlockSpec((n, t, C_WIDTH), lambda i: (0, 0, 0))
    return pl.pallas_call(
        _cmix_sample_kernel,
        grid=(1,),
        in_specs=[pl.BlockSpec((n, t, C_WIDTH), lambda i: (0, 0, O_CU // C_WIDTH)),
                  pl.BlockSpec((n, t, C_WIDTH), lambda i: (0, 0, O_CV // C_WIDTH)),
                  vec, vec,
                  pl.BlockSpec(w_cols.shape, lambda i: (0, 0, 0)),
                  pl.BlockSpec(bs_rows.shape, lambda i: (0, 0))],
        out_specs=[blk, blk],
        out_shape=[shape, shape],
        compiler_params=_cparams("arbitrary"),
        name="cmix_sample",
    )(h3, h3, ln_g.reshape(1, 1, -1), ln_b.reshape(1, 1, -1), w_cols, bs_rows)


def _top_blocks(scores, allowed, axis):
    nb = scores.shape[axis]
    bidx = _iota(scores.shape, axis).astype(F32)
    cur = jnp.where(allowed, scores, NEG)
    sel = jnp.zeros(scores.shape, F32)
    for _ in range(D_TOPK):
        mx = cur.max(axis, keepdims=True)
        first = jnp.where(cur == mx, bidx, float(nb)).min(axis, keepdims=True)
        pick = bidx == first
        sel = jnp.where(pick, jnp.where(allowed, 1.0, sel), sel)
        cur = jnp.where(pick, BELOW_NEG, cur)
    return sel


MOBA_LANE_HEADS = 4


def _moba_prompt_kernel(q_ref, k_ref, v_ref, o_ref, km_s, kb_s, vt_s, bias_s):
    nb = kb_s.shape[0]
    g = D_HEADS // D_KV
    nq = g * D_BLOCK
    qb = pl.program_id(2)

    @pl.when(qb == 0)
    def _():
        for b in range(nb):
            rows = slice(b * D_BLOCK, (b + 1) * D_BLOCK)
            kblk = k_ref[rows, :]
            km_s[b:b + 1, :] = jnp.mean(kblk, axis=0, keepdims=True)
            kb_s[b] = kblk.astype(BF16)
            vt_s[b] = v_ref[rows, :].T.astype(BF16)

    qf = jnp.concatenate([q_ref[:, gi * HEAD_DIM:(gi + 1) * HEAD_DIM] for gi in range(g)], axis=0) * ATTN_SCALE
    q = qf.astype(BF16)
    scores = _dot_nt(km_s[...], qf, precision=lax.Precision.HIGHEST)
    sel = _top_blocks(scores, _iota((nb, nq), 0) < qb, axis=0)
    bias_s[...] = jnp.where(sel > 0.5, 0.0, NEG)

    gl = MOBA_LANE_HEADS
    width = gl * D_BLOCK
    causal = _iota((D_BLOCK, width), 0) <= _mod(_iota((D_BLOCK, width), 1), D_BLOCK)
    for part in range(g // gl):
        lanes = slice(part * width, (part + 1) * width)
        qp = q[part * width:(part + 1) * width]
        s = jnp.where(causal, _dot_nt(kb_s[qb], qp), NEG)
        m = s.max(0, keepdims=True)
        p = jnp.exp(s - m)
        l = p.sum(0, keepdims=True)
        acc = _dot(vt_s[qb], p.astype(BF16))

        def body(j, carry, qp=qp, lanes=lanes):
            m, l, acc = carry
            s = _dot_nt(kb_s[j], qp) + bias_s[pl.ds(j, 1), lanes]
            m_new = jnp.maximum(m, s.max(0, keepdims=True))
            alpha = jnp.exp(m - m_new)
            p = jnp.exp(s - m_new)
            return m_new, alpha * l + p.sum(0, keepdims=True), alpha * acc + _dot(vt_s[j], p.astype(BF16))

        m, l, acc = lax.fori_loop(0, qb, body, (m, l, acc))
        out = acc / l
        for gi in range(gl):
            col = (part * gl + gi) * HEAD_DIM
            o_ref[:, col:col + HEAD_DIM] = out[:, gi * D_BLOCK:(gi + 1) * D_BLOCK].T


RIDER_VMEM_LIMIT = 56 * 1024 * 1024


def moba_prompt_steps(h3):
    n, seq, _ = h3.shape
    return n * D_KV * (seq // D_BLOCK)


def moba_prompt(h3, rider=None):
    n, seq, _ = h3.shape
    g = D_HEADS // D_KV
    gw = g * HEAD_DIM
    nb = seq // D_BLOCK
    in_specs = [pl.BlockSpec((None, D_BLOCK, gw), lambda b, hh, i: (b, i, O_DQ // gw + hh)),
                pl.BlockSpec((None, seq, HEAD_DIM), lambda b, hh, i: (b, 0, O_DK // HEAD_DIM + hh)),
                pl.BlockSpec((None, seq, HEAD_DIM), lambda b, hh, i: (b, 0, O_DV // HEAD_DIM + hh))]
    out_spec = pl.BlockSpec((None, D_BLOCK, gw), lambda b, hh, i: (b, i, hh))
    out_shape = jax.ShapeDtypeStruct((n, seq, D_WIDTH), F32)
    scratch = [pltpu.VMEM((nb, HEAD_DIM), F32),
               pltpu.VMEM((nb, D_BLOCK, HEAD_DIM), BF16),
               pltpu.VMEM((nb, HEAD_DIM, D_BLOCK), BF16),
               pltpu.VMEM((nb, g * D_BLOCK), F32)]
    if rider is None:
        return pl.pallas_call(
            _moba_prompt_kernel, grid=(n, D_KV, nb), in_specs=in_specs, out_specs=out_spec, out_shape=out_shape,
            scratch_shapes=scratch, compiler_params=_cparams("parallel", "parallel", "arbitrary"),
            name="moba_prompt")(h3, h3, h3)

    parts = rider(block_index=lambda b, hh, i: (b * D_KV + hh) * nb + i, ndim=3)
    assert parts["steps"] == n * D_KV * nb
    n_in, n_rin, n_rout, n_s = len(in_specs), len(parts["in_specs"]), len(parts["out_specs"]), len(scratch)

    def kernel(*refs):
        own_in, rider_in = refs[:n_in], refs[n_in:n_in + n_rin]
        p = n_in + n_rin
        own_out, rider_out = refs[p], refs[p + 1:p + 1 + n_rout]
        p += 1 + n_rout
        _moba_prompt_kernel(*own_in, own_out, *refs[p:p + n_s])
        parts["kernel"](*rider_in, *rider_out, *refs[p + n_s:])

    outs = pl.pallas_call(
        kernel, grid=(n, D_KV, nb),
        in_specs=in_specs + parts["in_specs"],
        out_specs=[out_spec] + parts["out_specs"],
        out_shape=[out_shape] + parts["out_shape"],
        scratch_shapes=scratch + parts["scratch_shapes"],
        compiler_params=pltpu.CompilerParams(dimension_semantics=("arbitrary",) * 3,
                                             vmem_limit_bytes=RIDER_VMEM_LIMIT),
        name="moba_prompt_decode")(h3, h3, h3, *parts["args"])
    return outs[0], parts["finish"](outs[1:])


KV_PAGES_BLOCK = 4


def _kv_pages_kernel(k_ref, v_ref, o_ref):
    for pg in range(o_ref.shape[0]):
        rows = slice(pg * PAGE_SIZE, (pg + 1) * PAGE_SIZE)
        for kv, src in enumerate((k_ref, v_ref)):
            for hh in range(D_KV):
                o_ref[pg, kv, hh] = src[rows, hh * HEAD_DIM:(hh + 1) * HEAD_DIM]


def kv_pages(h3):
    n, seq, _ = h3.shape
    kw = D_KV * HEAD_DIM
    pb = KV_PAGES_BLOCK
    return pl.pallas_call(
        _kv_pages_kernel,
        grid=(n, seq // (pb * PAGE_SIZE)),
        in_specs=[pl.BlockSpec((None, pb * PAGE_SIZE, kw), lambda b, p: (b, p, O_DK // kw)),
                  pl.BlockSpec((None, pb * PAGE_SIZE, kw), lambda b, p: (b, p, O_DV // kw))],
        out_specs=pl.BlockSpec((None, pb, 2, D_KV, PAGE_SIZE, HEAD_DIM), lambda b, p: (b, p, 0, 0, 0, 0)),
        out_shape=jax.ShapeDtypeStruct((n, seq // PAGE_SIZE, 2, D_KV, PAGE_SIZE, HEAD_DIM), F32),
        compiler_params=_cparams("parallel", "parallel"),
        name="kv_pages",
    )(h3, h3)


MOBA_CHUNK_PAGES = 16
MOBA_VMEM_LIMIT = 52 * 1024 * 1024


def _moba_decode_kernel(pt_ref, q_ref, kn_ref, vn_ref, pool_ref, o_ref, buf, sem, lg_s, km_s, p_s, bias_s, e_s):
    seq = pl.program_id(0)
    n_seq = pl.num_programs(0)
    n_pages = buf.shape[1]
    cp = MOBA_CHUNK_PAGES
    ppb = D_BLOCK // PAGE_SIZE
    n_blk = n_pages // ppb
    g = D_HEADS // D_KV
    nq = q_ref.shape[1]
    lane_blk = (nq, PAGE_SIZE)

    def page_copies(s, slot):
        return [pltpu.make_async_copy(pool_ref.at[pt_ref[s, pg]], buf.at[slot, pg], sem.at[slot])
                for pg in range(n_pages)]

    @pl.when(seq == 0)
    def _():
        for c in page_copies(0, 0):
            c.start()
        e_s[...] = jnp.where(_div(_iota(e_s.shape, 1), PAGE_SIZE) == _iota(e_s.shape, 0), 1.0, 0.0).astype(BF16)

    @pl.when(seq + 1 < n_seq)
    def _():
        for c in page_copies(seq + 1, (seq + 1) % 2):
            c.start()

    slot = seq % 2
    for c in page_copies(seq, slot):
        c.wait()

    for hk in range(D_KV):
        qf = q_ref[hk] * ATTN_SCALE
        q = qf.astype(BF16)
        for c in range(n_pages // cp):
            ks = [buf[slot, c * cp + i, 0, hk] for i in range(cp)]
            lg_s[hk, :, c * cp * PAGE_SIZE:(c + 1) * cp * PAGE_SIZE] = _dot_nt(
                q, jnp.concatenate([k.astype(BF16) for k in ks], axis=0))
            sums = [k.sum(axis=0, keepdims=True) for k in ks]
            means = [functools.reduce(lambda a, b: a + b, sums[b * ppb:(b + 1) * ppb]) * (1.0 / D_BLOCK)
                     for b in range(cp // ppb)]
            km_s[hk, c * (cp // ppb):(c + 1) * (cp // ppb), :] = jnp.concatenate(means, axis=0)

        scores = _dot_nt(qf, km_s[hk], precision=lax.Precision.HIGHEST)
        sel = _top_blocks(scores, _iota(scores.shape, 1) >= 0, axis=1)
        bias_s[...] = jnp.where(_dot(sel.astype(BF16), e_s[...]) > 0.5, 0.0, NEG)

        own = _dot_nt(q, kn_ref[hk].astype(BF16))
        own = jnp.where(_iota(own.shape, 1) <= _div(_iota(own.shape, 0), g), own, NEG)

        def masked(pg):
            b = pg // ppb
            return lg_s[hk, :, pg * PAGE_SIZE:(pg + 1) * PAGE_SIZE] + bias_s[:, b * PAGE_SIZE:(b + 1) * PAGE_SIZE]

        mx = masked(0)
        for pg in range(1, n_pages):
            mx = jnp.maximum(mx, masked(pg))
        m = jnp.maximum(mx.max(-1, keepdims=True), own.max(-1, keepdims=True))
        m_rep = jnp.broadcast_to(m, lane_blk)
        tot = jnp.zeros(lane_blk, F32)
        for pg in range(n_pages):
            p = jnp.exp(masked(pg) - m_rep)
            tot = tot + p
            p_s[hk, :, pg * PAGE_SIZE:(pg + 1) * PAGE_SIZE] = p.astype(BF16)
        po = jnp.exp(own - m)
        den = tot.sum(-1, keepdims=True) + po.sum(-1, keepdims=True)

        acc = _dot(po.astype(BF16), vn_ref[hk].astype(BF16))
        for c in range(n_pages // cp):
            v = jnp.concatenate([buf[slot, c * cp + i, 1, hk].astype(BF16) for i in range(cp)], axis=0)
            acc = acc + _dot(p_s[hk, :, c * cp * PAGE_SIZE:(c + 1) * cp * PAGE_SIZE], v)
        o_ref[hk] = acc / den


def moba_decode(pool, page_table, q, k_new, v_new):
    n, n_pages = page_table.shape
    assert (n_pages * PAGE_SIZE) % D_BLOCK == 0 and n_pages % MOBA_CHUNK_PAGES == 0
    nq, t = q.shape[2], k_new.shape[2]
    n_blk = n_pages * PAGE_SIZE // D_BLOCK
    keys = n_pages * PAGE_SIZE

    def per_seq(r):
        return pl.BlockSpec((None, D_KV, r, HEAD_DIM), lambda b, pt: (b, 0, 0, 0))

    grid_spec = pltpu.PrefetchScalarGridSpec(
        num_scalar_prefetch=1,
        grid=(n,),
        in_specs=[per_seq(nq), per_seq(t), per_seq(t), pl.BlockSpec(memory_space=pl.ANY)],
        out_specs=per_seq(nq),
        scratch_shapes=[pltpu.VMEM((2, n_pages) + pool.shape[1:], F32),
                        pltpu.SemaphoreType.DMA((2,)),
                        pltpu.VMEM((D_KV, nq, keys), F32),
                        pltpu.VMEM((D_KV, n_blk, HEAD_DIM), F32),
                        pltpu.VMEM((D_KV, nq, keys), BF16),
                        pltpu.VMEM((nq, n_blk * PAGE_SIZE), F32),
                        pltpu.VMEM((n_blk, n_blk * PAGE_SIZE), BF16)])
    return pl.pallas_call(
        _moba_decode_kernel,
        grid_spec=grid_spec,
        out_shape=jax.ShapeDtypeStruct((n, D_KV, nq, HEAD_DIM), F32),
        compiler_params=pltpu.CompilerParams(dimension_semantics=("arbitrary",), vmem_limit_bytes=MOBA_VMEM_LIMIT),
        name="moba_decode",
    )(page_table, q, k_new, v_new, pool)


def even_layer_prompt(x, w_in, w_out, g_pre, g_post, sink):
    n, s, d = x.shape
    x2 = x.reshape(n * s, d)
    h = norm_proj(x2, g_pre, w_in, tm=1024)
    h3 = h.reshape(n, s, IN_EVEN)
    a_out = attn_a_prompt(h3).reshape(n * s, A_WIDTH)
    b_out = attn_b_prompt(h3, sink).reshape(n * s, B_WIDTH)
    y = out_proj(x2, [a_out, b_out], h, [E_AG, E_BG, E_BG + GATE_W], w_out, g_post)
    rows = [kv_rows(h3, E_AK + gi * A_WIDTH, E_AV + gi * A_WIDTH, A_HEADS, win) for gi, (win, _) in enumerate(A_GROUPS)]
    rows.append(kv_rows(h3, E_BK, E_BV, B_KV, B_WINDOW))
    return y.reshape(n, s, d), rows


def even_layer_sample(x, bufs_a, buf_b, w_in, w_out, g_pre, g_post, sink, ride=None):
    n, t, d = x.shape
    x2 = x.reshape(n * t, d)
    h = norm_proj(x2, g_pre, w_in, tm=n * t)
    big = max(range(A_NG), key=lambda gi: bufs_a[gi].size)
    stats, new_bufs = [], []
    for gi, (_, dil) in enumerate(A_GROUPS):
        cols = slice(gi * A_WIDTH, (gi + 1) * A_WIDTH)
        q = h[:, E_AQ:E_AK][:, cols].reshape(n, t * A_HEADS, HEAD_DIM)
        k = h[:, E_AK:E_AV][:, cols].reshape(n, t * A_HEADS, HEAD_DIM)
        v = h[:, E_AV:E_AG][:, cols].reshape(n, t * A_HEADS, HEAD_DIM)
        if ride is not None and gi == big:
            buf, o, m, l = ride(functools.partial(decode_parts, bufs_a[gi], q, k, v, A_HEADS, A_HEADS, dil, seqs=1))
        else:
            buf, o, m, l = decode_attn(bufs_a[gi], q, k, v, A_HEADS, A_HEADS, dil)
        stats.append((o, m, l))
        new_bufs.append(buf)
    a_out = merge_groups(stats).reshape(n * t, A_WIDTH)
    q = h[:, E_BQ:E_BK].reshape(n, t * B_HEADS, HEAD_DIM)
    k = h[:, E_BK:E_BV].reshape(n, t * B_KV, HEAD_DIM)
    v = h[:, E_BV:E_BG].reshape(n, t * B_KV, HEAD_DIM)
    sink_rows = jnp.broadcast_to(jnp.tile(sink, t)[:, None], (t * B_HEADS, HEAD_DIM))
    buf, b_out = decode_attn(buf_b, q, k, v, B_KV, B_HEADS, 1, sink=sink_rows)
    new_bufs.append(buf)
    y = out_proj(x2, [a_out, b_out.reshape(n * t, B_WIDTH)], h, [E_AG, E_BG, E_BG + GATE_W], w_out, g_post)
    return y.reshape(n, t, d), new_bufs


def odd_layer_prompt(x, w_in, w_out, g_pre, g_post, ln_g, ln_b, ws, bs, rider=None):
    n, s, d = x.shape
    x2 = x.reshape(n * s, d)
    h = norm_proj(x2, g_pre, w_in, tm=1024)
    h3 = h.reshape(n, s, IN_ODD)
    c_out = cmix_prompt(h, ln_g, ln_b, ws, bs)
    if rider is None:
        d_out, rider_outs = moba_prompt(h3), None
    else:
        d_out, rider_outs = moba_prompt(h3, rider)
    gates = [O_CG, O_CG + GATE_W, O_DG, O_DG + GATE_W]
    y = out_proj(x2, [c_out, d_out.reshape(n * s, D_WIDTH)], h, gates, w_out, g_post)
    return y.reshape(n, s, d), kv_pages(h3), rider_outs


def odd_layer_sample(x, pool, page_table, w_in, w_out, g_pre, g_post, ln_g, ln_b, ws, bs):
    n, t, d = x.shape
    g = D_HEADS // D_KV
    x2 = x.reshape(n * t, d)
    h = norm_proj(x2, g_pre, w_in, tm=n * t)
    c_out, v_rows = cmix_sample(h.reshape(n, t, IN_ODD), ln_g, ln_b, ws, bs)
    q = h[:, O_DQ:O_DK].reshape(n, t, D_KV, g, HEAD_DIM).transpose(0, 2, 1, 3, 4).reshape(n, D_KV, t * g, HEAD_DIM)
    k = h[:, O_DK:O_DV].reshape(n, t, D_KV, HEAD_DIM).transpose(0, 2, 1, 3)
    v = h[:, O_DV:O_DG].reshape(n, t, D_KV, HEAD_DIM).transpose(0, 2, 1, 3)
    o = moba_decode(pool, page_table, q, k, v)
    d_out = o.reshape(n, D_KV, t, g, HEAD_DIM).transpose(0, 2, 1, 3, 4).reshape(n * t, D_WIDTH)
    gates = [O_CG, O_CG + GATE_W, O_DG, O_DG + GATE_W]
    y = out_proj(x2, [c_out.reshape(n * t, C_WIDTH), d_out], h, gates, w_out, g_post)
    return y.reshape(n, t, d), jnp.stack([k, v], axis=1), v_rows


def kernel(x_prompt, x_sample, cache_a1, cache_a2, cache_a3, cache_b, cache_d, page_table, norm_pre_even, norm_post_even, w_in_even, w_out_even, sink_b, norm_pre_odd, norm_post_odd, w_in_odd, w_out_odd, c_ln_g, c_ln_b, c_ws, c_bs):
    yp, ys = x_prompt, x_sample
    caches_a = (cache_a1, cache_a2, cache_a3)
    a_p, a_s = [[] for _ in range(A_NG)], [[] for _ in range(A_NG)]
    b_p, b_s, d_p, d_s, c_s = [], [], [], [], []

    def odd_prompt(x, li, rider=None):
        return odd_layer_prompt(x, w_in_odd[li].astype(BF16), w_out_odd[li].astype(BF16), norm_pre_odd[li],
                                norm_post_odd[li], c_ln_g[li], c_ln_b[li], c_ws[li], c_bs[li], rider)

    odd_done = None
    for layer in range(DEPTH):
        li = layer // 2
        if layer % 2 == 0:
            w_in, w_out = w_in_even[li].astype(BF16), w_out_even[li].astype(BF16)
            yp, rows = even_layer_prompt(yp, w_in, w_out, norm_pre_even[li], norm_post_even[li], sink_b[li])
            ride = None
            if layer + 1 < DEPTH and ys.shape[0] == yp.shape[0] * D_KV * (yp.shape[1] // D_BLOCK):
                def ride(make_parts, x=yp, lo=(layer + 1) // 2):
                    nonlocal odd_done
                    y_odd, kvp_odd, rider_outs = odd_prompt(x, lo, make_parts)
                    odd_done = (y_odd, kvp_odd)
                    return rider_outs
            ys, bufs = even_layer_sample(ys, [c[li] for c in caches_a], cache_b[li], w_in, w_out,
                                         norm_pre_even[li], norm_post_even[li], sink_b[li], ride)
            for gi in range(A_NG):
                a_p[gi].append(rows[gi])
                a_s[gi].append(bufs[gi])
            b_p.append(rows[A_NG])
            b_s.append(bufs[A_NG])
        else:
            w_in, w_out = w_in_odd[li].astype(BF16), w_out_odd[li].astype(BF16)
            if odd_done is not None:
                (yp, kvp), odd_done = odd_done, None
            else:
                yp, kvp, _ = odd_prompt(yp, li)
            ys, kvs, vrows = odd_layer_sample(ys, cache_d[li], page_table, w_in, w_out, norm_pre_odd[li],
                                              norm_post_odd[li], c_ln_g[li], c_ln_b[li], c_ws[li], c_bs[li])
            d_p.append(kvp)
            d_s.append(kvs)
            c_s.append(vrows)
    return (yp, ys, jnp.stack(a_p[0]), jnp.stack(a_s[0]), jnp.stack(a_p[1]), jnp.stack(a_s[1]),
            jnp.stack(a_p[2]), jnp.stack(a_s[2]), jnp.stack(b_p), jnp.stack(b_s),
            jnp.stack(d_p), jnp.stack(d_s), jnp.stack(c_s))
```

```python
import functools

import jax
import jax.numpy as jnp
from jax import lax
from jax.experimental import pallas as pl
from jax.experimental.pallas import tpu as pltpu

F32 = jnp.float32
BF16 = jnp.bfloat16

D_MODEL = 2048
DEPTH = 2
DEC_SEQ = 8
PAST_LEN = 8192
PAGE_SIZE = 128
HEAD_DIM = 128
ATTN_SCALE = HEAD_DIM ** -0.5
A_GROUPS = ((128, 1), (512, 4), (2048, 16))
A_NG = len(A_GROUPS)
A_HEADS = 4
A_QKV = A_NG * A_HEADS * HEAD_DIM
A_WIDTH = A_HEADS * HEAD_DIM
B_HEADS = 8
B_KV = 2
B_WINDOW = 128
B_WIDTH = B_HEADS * HEAD_DIM
C_CHUNK = 128
C_GROUPS = 8
C_GDIM = 128
C_WIDTH = C_GROUPS * C_GDIM
D_HEADS = 8
D_KV = 2
D_BLOCK = 256
D_TOPK = 3
D_WIDTH = D_HEADS * HEAD_DIM
EPS = 1e-6
NEG = -1e30
BELOW_NEG = -3e38

E_AQ, E_AK, E_AV = 0, A_QKV, 2 * A_QKV
E_AG = 3 * A_QKV
E_BQ = E_AG + A_WIDTH
E_BK = E_BQ + B_WIDTH
E_BV = E_BK + B_KV * HEAD_DIM
E_BG = E_BV + B_KV * HEAD_DIM
IN_EVEN = E_BG + B_WIDTH
O_CU, O_CV, O_CG = 0, C_WIDTH, 2 * C_WIDTH
O_DQ = 3 * C_WIDTH
O_DK = O_DQ + D_WIDTH
O_DV = O_DK + D_KV * HEAD_DIM
O_DG = O_DV + D_KV * HEAD_DIM
IN_ODD = O_DG + D_WIDTH

BAND = 128
GATE_W = 512
VMEM_LIMIT = 48 * 1024 * 1024


def _cparams(*sem):
    return pltpu.CompilerParams(dimension_semantics=sem, vmem_limit_bytes=VMEM_LIMIT)


def _dot_nt(a, b, precision=None):
    return lax.dot_general(a, b, (((1,), (1,)), ((), ())), preferred_element_type=F32, precision=precision)


def _dot(a, b):
    return jnp.dot(a, b, preferred_element_type=F32)


def _iota(shape, dim):
    return lax.broadcasted_iota(jnp.int32, shape, dim)


def _div(x, d):
    assert d & (d - 1) == 0
    return x >> (d.bit_length() - 1)


def _mod(x, d):
    assert d & (d - 1) == 0
    return x & (d - 1)


def _norm_proj_kernel(x_ref, g_ref, w_ref, o_ref, xn_ref):
    @pl.when(pl.program_id(1) == 0)
    def _():
        x = x_ref[...]
        r = lax.rsqrt(jnp.mean(x * x, axis=-1, keepdims=True) + EPS)
        xn_ref[...] = (x * r * g_ref[...]).astype(BF16)

    o_ref[...] = _dot(xn_ref[...], w_ref[...])


def norm_proj(x, g, w, tm, tn=512):
    m, d = x.shape
    n = w.shape[1]
    return pl.pallas_call(
        _norm_proj_kernel,
        grid=(m // tm, n // tn),
        in_specs=[pl.BlockSpec((tm, d), lambda i, j: (i, 0)),
                  pl.BlockSpec((1, d), lambda i, j: (0, 0)),
                  pl.BlockSpec((d, tn), lambda i, j: (0, j))],
        out_specs=pl.BlockSpec((tm, tn), lambda i, j: (i, j)),
        out_shape=jax.ShapeDtypeStruct((m, n), F32),
        scratch_shapes=[pltpu.VMEM((tm, d), BF16)],
        compiler_params=_cparams("parallel", "arbitrary"),
        name="norm_proj",
    )(x, g.reshape(1, d), w)


def _masked_attention(q, k, v, visible):
    s = jnp.where(visible, _dot_nt(q, k), NEG)
    m = s.max(-1, keepdims=True)
    p = jnp.exp(s - m)
    return _dot(p.astype(BF16), v), m, p.sum(-1, keepdims=True)


def _rows(ref, start, count, stride):
    if stride == 1:
        return ref[pl.ds(start, count), :]
    return ref[pl.ds(start, count, stride=stride), :]


A_TILE = 256


def _attn_a_prompt_kernel(q1, k1, v1, q2, k2, v2, q3, k3, v3, o_ref, o1_s, m1_s, l1_s, o2_s, m2_s, l2_s):
    seq = q1.shape[0]
    wide = (BAND, HEAD_DIM)
    tile = A_TILE
    qi = _iota((tile, tile + BAND), 0)
    kc = _iota((tile, tile + BAND), 1)
    vis_next = jnp.logical_and(kc >= qi, kc <= qi + BAND)
    qi0 = _iota((tile, tile), 0)
    kc0 = _iota((tile, tile), 1)
    vis_first = jnp.logical_and(kc0 <= qi0, kc0 >= qi0 - BAND)

    def banded(qr, kr, vr, base, stride, t):
        first = base + stride * t * tile
        q = (_rows(qr, first, tile, stride) * ATTN_SCALE).astype(BF16)
        if t == 0:
            k, v, vis = _rows(kr, first, tile, stride), _rows(vr, first, tile, stride), vis_first
        else:
            back = first - stride * BAND
            k, v, vis = _rows(kr, back, tile + BAND, stride), _rows(vr, back, tile + BAND, stride), vis_next
        return _masked_attention(q, k.astype(BF16), v.astype(BF16), vis)

    for t in range(seq // tile):
        rows = slice(t * tile, (t + 1) * tile)
        o, m, l = banded(q1, k1, v1, 0, 1, t)
        o1_s[rows, :] = o
        m1_s[rows, :] = jnp.broadcast_to(m, (tile, HEAD_DIM))
        l1_s[rows, :] = jnp.broadcast_to(l, (tile, HEAD_DIM))

    d2 = A_GROUPS[1][1]
    sub2 = seq // d2
    for r in range(d2):
        for t in range(sub2 // tile):
            rows = slice(t * tile, (t + 1) * tile)
            o, m, l = banded(q2, k2, v2, r, d2, t)
            o2_s[r, rows, :] = o
            m2_s[r, rows, :] = jnp.broadcast_to(m, (tile, HEAD_DIM))
            l2_s[r, rows, :] = jnp.broadcast_to(l, (tile, HEAD_DIM))

    d3 = A_GROUPS[2][1]
    ratio = d3 // d2
    pack = tile // BAND
    same = _div(qi0, BAND) == _div(kc0, BAND)
    vis_pack = jnp.logical_and(same, _mod(kc0, BAND) <= _mod(qi0, BAND))
    for r in range(d3):
        if r % pack == 0:
            def packed(ref, scale=None, r=r):
                x = jnp.concatenate([_rows(ref, r + u, BAND, d3) for u in range(pack)], axis=0)
                return (x if scale is None else x * scale).astype(BF16)
            o3p, m3p, l3p = _masked_attention(packed(q3, ATTN_SCALE), packed(k3), packed(v3), vis_pack)
        part = slice((r % pack) * BAND, (r % pack + 1) * BAND)
        o3 = o3p[part]
        m3 = jnp.broadcast_to(m3p[part], wide)
        l3 = jnp.broadcast_to(l3p[part], wide)
        o1 = o1_s[pl.ds(r, BAND, stride=d3), :]
        m1 = m1_s[pl.ds(r, BAND, stride=d3), :]
        l1 = l1_s[pl.ds(r, BAND, stride=d3), :]
        r2, off2 = r % d2, r // d2
        o2 = o2_s[r2, pl.ds(off2, BAND, stride=ratio), :]
        m2 = m2_s[r2, pl.ds(off2, BAND, stride=ratio), :]
        l2 = l2_s[r2, pl.ds(off2, BAND, stride=ratio), :]
        mmax = jnp.maximum(jnp.maximum(m1, m2), m3)
        e1, e2, e3 = jnp.exp(m1 - mmax), jnp.exp(m2 - mmax), jnp.exp(m3 - mmax)
        num = o1 * e1 + o2 * e2 + o3 * e3
        den = l1 * e1 + l2 * e2 + l3 * e3
        o_ref[pl.ds(r, BAND, stride=d3), :] = num / den


def attn_a_prompt(h3):
    n, seq, _ = h3.shape
    assert seq == A_GROUPS[2][0] and all(w // d == BAND for w, d in A_GROUPS)

    def spec(off, gi):
        blk = off // HEAD_DIM + gi * A_HEADS
        return pl.BlockSpec((None, seq, HEAD_DIM), lambda b, hh: (b, 0, blk + hh))

    in_specs = []
    for gi in range(A_NG):
        in_specs += [spec(E_AQ, gi), spec(E_AK, gi), spec(E_AV, gi)]
    d2 = A_GROUPS[1][1]
    stat = pltpu.VMEM((seq, HEAD_DIM), F32)
    stat2 = pltpu.VMEM((d2, seq // d2, HEAD_DIM), F32)
    return pl.pallas_call(
        _attn_a_prompt_kernel,
        grid=(n, A_HEADS),
        in_specs=in_specs,
        out_specs=pl.BlockSpec((None, seq, HEAD_DIM), lambda b, hh: (b, 0, hh)),
        out_shape=jax.ShapeDtypeStruct((n, seq, A_WIDTH), F32),
        scratch_shapes=[stat, stat, stat, stat2, stat2, stat2],
        compiler_params=_cparams("parallel", "parallel"),
        name="attn_a_prompt",
    )(*([h3] * 9))


def _attn_b_prompt_kernel(sink_ref, q_ref, k_ref, v_ref, o_ref, kb_s, vt_s):
    seq = q_ref.shape[0]
    g = B_HEADS // B_KV
    nq = g * BAND
    nblk = seq // BAND
    kvh = pl.program_id(1)
    for b in range(nblk):
        rows = slice(b * BAND, (b + 1) * BAND)
        kb_s[rows, :] = k_ref[rows, :].astype(BF16)
        vt_s[b] = v_ref[rows, :].T.astype(BF16)
    head = _div(_iota((1, nq), 1), BAND)
    sink = jnp.zeros((1, nq), F32)
    for gi in range(g):
        sink = jnp.where(head == gi, sink_ref[kvh * g + gi], sink)
    qi = _mod(_iota((2 * BAND, nq), 1), BAND)
    key = _iota((2 * BAND, nq), 0)
    band = jnp.logical_and(key >= qi, key <= qi + BAND)
    for b in range(nblk):
        rows = slice(b * BAND, (b + 1) * BAND)
        qb = q_ref[rows, :]
        q = jnp.concatenate([qb[:, gi * HEAD_DIM:(gi + 1) * HEAD_DIM] for gi in range(g)], axis=0)
        q = (q * ATTN_SCALE).astype(BF16)
        if b == 0:
            s = jnp.where(band[BAND:], _dot_nt(kb_s[rows, :], q), NEG)
        else:
            s = jnp.where(band, _dot_nt(kb_s[(b - 1) * BAND:(b + 1) * BAND, :], q), NEG)
        m = s.max(0, keepdims=True)
        pf = jnp.exp(s - m)
        l = pf.sum(0, keepdims=True)
        p = pf.astype(BF16)
        if b == 0:
            o = _dot(vt_s[0], p)
        else:
            o = _dot(vt_s[b - 1], p[:BAND]) + _dot(vt_s[b], p[BAND:])
        m2 = jnp.maximum(m, sink)
        a = jnp.exp(m - m2)
        out = o * (a / (l * a + jnp.exp(sink - m2)))
        for gi in range(g):
            o_ref[rows, gi * HEAD_DIM:(gi + 1) * HEAD_DIM] = out[:, gi * BAND:(gi + 1) * BAND].T


def attn_b_prompt(h3, sink):
    n, seq, _ = h3.shape
    gw = (B_HEADS // B_KV) * HEAD_DIM
    return pl.pallas_call(
        _attn_b_prompt_kernel,
        grid=(n, B_KV),
        in_specs=[pl.BlockSpec(memory_space=pltpu.SMEM),
                  pl.BlockSpec((None, seq, gw), lambda b, hh: (b, 0, E_BQ // gw + hh)),
                  pl.BlockSpec((None, seq, HEAD_DIM), lambda b, hh: (b, 0, E_BK // HEAD_DIM + hh)),
                  pl.BlockSpec((None, seq, HEAD_DIM), lambda b, hh: (b, 0, E_BV // HEAD_DIM + hh))],
        out_specs=pl.BlockSpec((None, seq, gw), lambda b, hh: (b, 0, hh)),
        out_shape=jax.ShapeDtypeStruct((n, seq, B_WIDTH), F32),
        scratch_shapes=[pltpu.VMEM((seq, HEAD_DIM), BF16), pltpu.VMEM((seq // BAND, HEAD_DIM, BAND), BF16)],
        compiler_params=_cparams("parallel", "parallel"),
        name="attn_b_prompt",
    )(sink, h3, h3, h3)


KV_ROWS_BLOCK = 512


def _kv_rows_kernel(k_ref, v_ref, o_ref, *, heads):
    npos = k_ref.shape[0]
    for kv, src in enumerate((k_ref, v_ref)):
        for hh in range(heads):
            o_ref[pl.ds(kv * heads + hh, npos, stride=2 * heads), :] = src[:, hh * HEAD_DIM:(hh + 1) * HEAD_DIM]


def kv_rows(h3, k_off, v_off, heads, window):
    n, seq, _ = h3.shape
    w = min(window, seq)
    hw = heads * HEAD_DIM
    npos = min(w, KV_ROWS_BLOCK)
    first = (seq - w) // npos
    rows = npos * 2 * heads
    out = pl.pallas_call(
        functools.partial(_kv_rows_kernel, heads=heads),
        grid=(n, w // npos),
        in_specs=[pl.BlockSpec((None, npos, hw), lambda b, c: (b, first + c, k_off // hw)),
                  pl.BlockSpec((None, npos, hw), lambda b, c: (b, first + c, v_off // hw))],
        out_specs=pl.BlockSpec((None, rows, HEAD_DIM), lambda b, c: (b, c, 0)),
        out_shape=jax.ShapeDtypeStruct((n, w * 2 * heads, HEAD_DIM), F32),
        compiler_params=_cparams("parallel", "parallel"),
        name="kv_rows",
    )(h3, h3)
    return out.reshape(n, w, 2, heads, HEAD_DIM)


DEC_CHUNK = 2048
DEC_UNROLL = 4


def _decode_kernel(*refs, heads, q_heads, dil, use_sink, ndim=1):
    if use_sink:
        sink_ref, q_ref, kn_ref, vn_ref, nf_ref, c_ref, nc_ref, out_ref, bias_s, biasn_s, l_s, xb_s = refs
    else:
        q_ref, kn_ref, vn_ref, nf_ref, c_ref, nc_ref, out_ref, m_ref, l_ref, bias_s, biasn_s, l_s, xb_s = refs
    slots = 2 * heads
    seqs, rows = c_ref.shape[:2]
    lb = rows // slots
    nq = q_ref.shape[1]
    t_new = kn_ref.shape[1] // heads
    per_kv = q_heads // heads
    chunk = min(DEC_CHUNK, rows)
    fresh = nf_ref.shape[1]
    keep = min(dil, t_new)
    period, kept = dil * slots, keep * slots
    cchunk = chunk * keep // dil

    @pl.when(functools.reduce(jnp.logical_and, [pl.program_id(d) == 0 for d in range(ndim)]))
    def _():
        ncols = rows * keep // dil
        row = _iota((nq, ncols), 0)
        col = _iota((nq, ncols), 1)
        t = _div(row, q_heads)
        kvh = _div(_mod(row, q_heads), per_kv)
        within = _mod(col, kept)
        pos = _div(col, kept) * dil + _div(within, slots)
        ok = (_mod(within, slots) == kvh) & (pos >= t) & (_mod(lb + t - pos, dil) == 0)
        bias_s[...] = jnp.where(ok, 0.0, NEG)
        rown = _iota((nq, t_new * heads), 0)
        coln = _iota((nq, t_new * heads), 1)
        tn = _div(rown, q_heads)
        kvhn = _div(_mod(rown, q_heads), per_kv)
        tk = _div(coln, heads)
        okn = (_mod(coln, heads) == kvhn) & (tk <= tn) & (_mod(tn - tk, dil) == 0)
        biasn_s[...] = jnp.where(okn, 0.0, NEG)

    def one_seq(b, carry):
        q = (q_ref[b] * ATTN_SCALE).astype(BF16)
        sn = _dot_nt(q, kn_ref[b].astype(BF16)) + biasn_s[...]
        m = sn.max(-1, keepdims=True)
        for c in range(rows // chunk):
            sl = slice(c * chunk, (c + 1) * chunk)
            x = c_ref[b, sl, :]
            if c == 0:
                nc_ref[b, 0:chunk - fresh, :] = x[fresh:]
            else:
                nc_ref[b, c * chunk - fresh:(c + 1) * chunk - fresh, :] = x
            if keep < dil:
                x = jnp.concatenate([x[u * period:u * period + kept] for u in range(chunk // period)], axis=0)
            sl = slice(c * cchunk, (c + 1) * cchunk)
            xb = x.astype(BF16)
            xb_s[b, sl, :] = xb
            s = _dot_nt(q, xb) + bias_s[:, sl]
            l_s[b, :, sl] = s
            m = jnp.maximum(m, s.max(-1, keepdims=True))
        pn = jnp.exp(sn - m)
        den = pn.sum(-1, keepdims=True)
        o = _dot(pn.astype(BF16), vn_ref[b].astype(BF16))
        for c in range(rows // chunk):
            sl = slice(c * cchunk, (c + 1) * cchunk)
            p = jnp.exp(l_s[b, :, sl] - m)
            den = den + p.sum(-1, keepdims=True)
            o = o + _dot(pltpu.roll(p, heads, axis=1).astype(BF16), xb_s[b, sl, :])

        if use_sink:
            sink = sink_ref[...]
            m2 = jnp.maximum(m, sink)
            a = jnp.exp(m - m2)
            out_ref[b] = o * (a / (den * a + jnp.exp(sink - m2)))
        else:
            out_ref[b] = o
            m_ref[b] = jnp.broadcast_to(m, (nq, HEAD_DIM))
            l_ref[b] = jnp.broadcast_to(den, (nq, HEAD_DIM))

        nc_ref[b, pl.ds(rows - fresh, fresh), :] = nf_ref[b]
        return carry

    if seqs == 1:
        one_seq(0, 0)
    else:
        lax.fori_loop(0, seqs, one_seq, 0, unroll=min(seqs, DEC_UNROLL))


DEC_BLOCK_BYTES = 4 * 1024 * 1024


def decode_parts(cache, q, k_new, v_new, heads, q_heads, dil, sink=None, seqs=None, block_index=None, ndim=1):
    n, lb = cache.shape[:2]
    rows = lb * 2 * heads
    flat = cache.reshape(n, rows, HEAD_DIM)
    k4 = k_new.reshape(n, -1, 1, heads, HEAD_DIM)
    v4 = v_new.reshape(n, -1, 1, heads, HEAD_DIM)
    new_flat = jnp.concatenate([k4, v4], axis=2).reshape(n, -1, HEAD_DIM)
    if seqs is None:
        seqs = 1
        while seqs * 2 * rows * HEAD_DIM * 4 <= DEC_BLOCK_BYTES and n % (seqs * 2) == 0:
            seqs *= 2
    if block_index is None:
        block_index = lambda b: b
    nq = q.shape[1]
    tk = k_new.shape[1]
    use_sink = sink is not None
    assert lb % dil == 0 and min(DEC_CHUNK, rows) % (dil * 2 * heads) == 0
    cols = rows * min(dil, tk // heads) // dil

    def per_seq(r):
        return pl.BlockSpec((seqs, r, HEAD_DIM), lambda *ids: (block_index(*ids), 0, 0))

    in_specs = [per_seq(nq), per_seq(tk), per_seq(tk), per_seq(new_flat.shape[1]), per_seq(rows)]
    args = [q, k_new, v_new, new_flat, flat]
    stat = jax.ShapeDtypeStruct((n, nq, HEAD_DIM), F32)
    out_shape = [jax.ShapeDtypeStruct(flat.shape, F32), stat]
    out_specs = [per_seq(rows), per_seq(nq)]
    if use_sink:
        in_specs = [pl.BlockSpec((nq, HEAD_DIM), lambda *ids: (0, 0))] + in_specs
        args = [sink] + args
    else:
        out_shape += [stat, stat]
        out_specs += [per_seq(nq), per_seq(nq)]
    return dict(
        kernel=functools.partial(_decode_kernel, heads=heads, q_heads=q_heads, dil=dil, use_sink=use_sink, ndim=ndim),
        in_specs=in_specs, args=args, out_specs=out_specs, out_shape=out_shape, steps=n // seqs,
        scratch_shapes=[pltpu.VMEM((nq, cols), F32), pltpu.VMEM((nq, tk), F32), pltpu.VMEM((seqs, nq, cols), F32),
                        pltpu.VMEM((seqs, cols, HEAD_DIM), BF16)],
        finish=lambda outs: (outs[0].reshape(cache.shape),) + tuple(outs[1:]))


def decode_attn(cache, q, k_new, v_new, heads, q_heads, dil, sink=None):
    parts = decode_parts(cache, q, k_new, v_new, heads, q_heads, dil, sink)
    outs = pl.pallas_call(
        parts["kernel"],
        grid=(parts["steps"],),
        in_specs=parts["in_specs"],
        out_specs=parts["out_specs"],
        out_shape=parts["out_shape"],
        scratch_shapes=parts["scratch_shapes"],
        compiler_params=_cparams("arbitrary"),
        name="decode_attn",
    )(*parts["args"])
    return parts["finish"](outs)


def _merge_kernel(*refs):
    o_ref = refs[-1]
    stats = [refs[3 * g:3 * g + 3] for g in range(A_NG)]
    mmax = functools.reduce(jnp.maximum, [m[...] for _, m, _ in stats])
    num = 0.0
    den = 0.0
    for o, m, l in stats:
        e = jnp.exp(m[...] - mmax)
        num = num + o[...] * e
        den = den + l[...] * e
    o_ref[...] = num / den


def merge_groups(stats):
    flat = [a.reshape(-1, HEAD_DIM) for st in stats for a in st]
    out = pl.pallas_call(_merge_kernel, out_shape=jax.ShapeDtypeStruct(flat[0].shape, F32),
                         compiler_params=pltpu.CompilerParams(vmem_limit_bytes=VMEM_LIMIT),
                         name="merge_groups")(*flat)
    return out


def _out_proj_kernel(*refs, pieces):
    x_ref = refs[0]
    o_refs = refs[1:1 + pieces]
    g_refs = refs[1 + pieces:1 + 2 * pieces]
    w_ref, gp_ref, y_ref = refs[1 + 2 * pieces:]
    acc = None
    for p in range(pieces):
        g = g_refs[p][...]
        mix = (o_refs[p][...] * (g * (1.0 / (1.0 + jnp.exp(-g))))).astype(BF16)
        d = _dot(mix, w_ref[p * GATE_W:(p + 1) * GATE_W, :])
        acc = d if acc is None else acc + d
    r = lax.rsqrt(jnp.mean(acc * acc, axis=-1, keepdims=True) + EPS)
    y_ref[...] = x_ref[...] + acc * r * gp_ref[...]


def out_proj(x, outs, h, gate_offs, w_out, g_post, tm=512):
    m, d = x.shape
    tm = min(tm, m)
    o_args, o_specs = [], []
    for arr in outs:
        for c in range(arr.shape[1] // GATE_W):
            o_args.append(arr)
            o_specs.append(pl.BlockSpec((tm, GATE_W), lambda i, c=c: (i, c)))
    g_specs = [pl.BlockSpec((tm, GATE_W), lambda i, c=off // GATE_W: (i, c)) for off in gate_offs]
    pieces = len(o_args)
    assert pieces == len(gate_offs) and pieces * GATE_W == w_out.shape[0]
    return pl.pallas_call(
        functools.partial(_out_proj_kernel, pieces=pieces),
        grid=(m // tm,),
        in_specs=[pl.BlockSpec((tm, d), lambda i: (i, 0))] + o_specs + g_specs + [
            pl.BlockSpec(w_out.shape, lambda i: (0, 0), pipeline_mode=pl.Buffered(1)),
            pl.BlockSpec((1, d), lambda i: (0, 0))],
        out_specs=pl.BlockSpec((tm, d), lambda i: (i, 0)),
        out_shape=jax.ShapeDtypeStruct((m, d), F32),
        compiler_params=_cparams("parallel"),
        name="out_proj",
    )(x, *o_args, *([h] * pieces), w_out, g_post.reshape(1, d))


def _layernorm(x, g, b):
    xc = x - jnp.mean(x, axis=-1, keepdims=True)
    y = xc * lax.rsqrt(jnp.mean(xc * xc, axis=-1, keepdims=True) + EPS)
    return y * g + b


def _cmix_prompt_kernel(cu_ref, cv_ref, lg_ref, lb_ref, ws_ref, bs_ref, o_ref):
    v = _layernorm(cv_ref[...], lg_ref[...], lb_ref[...]).astype(BF16)
    tril = _iota((C_CHUNK, C_CHUNK), 1) <= _iota((C_CHUNK, C_CHUNK), 0)
    for g in range(C_GROUPS):
        cols = slice(g * C_GDIM, (g + 1) * C_GDIM)
        w = jnp.where(tril, ws_ref[g], 0.0).astype(BF16)
        for c in range(cu_ref.shape[0] // C_CHUNK):
            rows = slice(c * C_CHUNK, (c + 1) * C_CHUNK)
            o_ref[rows, cols] = cu_ref[rows, cols] * (_dot(w, v[rows, cols]) + bs_ref[:, cols])


def cmix_prompt(h, ln_g, ln_b, ws, bs, tm=512):
    m = h.shape[0]
    bs_rows = jnp.repeat(bs.T, C_GDIM, axis=1)
    vec = pl.BlockSpec((1, C_WIDTH), lambda i: (0, 0))
    return pl.pallas_call(
        _cmix_prompt_kernel,
        grid=(m // tm,),
        in_specs=[pl.BlockSpec((tm, C_WIDTH), lambda i: (i, O_CU // C_WIDTH)),
                  pl.BlockSpec((tm, C_WIDTH), lambda i: (i, O_CV // C_WIDTH)),
                  vec, vec,
                  pl.BlockSpec(ws.shape, lambda i: (0, 0, 0)),
                  pl.BlockSpec(bs_rows.shape, lambda i: (0, 0))],
        out_specs=pl.BlockSpec((tm, C_WIDTH), lambda i: (i, 0)),
        out_shape=jax.ShapeDtypeStruct((m, C_WIDTH), F32),
        compiler_params=_cparams("parallel"),
        name="cmix_prompt",
    )(h, h, ln_g.reshape(1, -1), ln_b.reshape(1, -1), ws, bs_rows)


def _cmix_sample_kernel(cu_ref, cv_ref, lg_ref, lb_ref, w_ref, bs_ref, o_ref, v_ref):
    v = _layernorm(cv_ref[...], lg_ref[...], lb_ref[...])
    v_ref[...] = v
    t = v.shape[1]
    i = _iota((t, C_WIDTH), 0)
    acc = jnp.broadcast_to(bs_ref[...][None], v.shape)
    for j in range(t):
        wj = jnp.where(i >= j, w_ref[j], 0.0)
        acc = acc + wj[None] * v[:, j:j + 1, :]
    o_ref[...] = cu_ref[...] * acc


def cmix_sample(h3, ln_g, ln_b, ws, bs):
    n, t, _ = h3.shape
    w_cols = jnp.repeat(ws[:, :t, :t].transpose(2, 1, 0), C_GDIM, axis=2)
    bs_rows = jnp.repeat(bs[:, :t].T, C_GDIM, axis=1)
    vec = pl.BlockSpec((1, 1, C_WIDTH), lambda i: (0, 0, 0))
    shape = jax.ShapeDtypeStruct((n, t, C_WIDTH), F32)
    blk = pl.BlockSpec((n, t, C_WIDTH), lambda i: (0, 0, 0))
    return pl.pallas_call(
        _cmix_sample_kernel,
        grid=(1,),
        in_specs=[pl.BlockSpec((n, t, C_WIDTH), lambda i: (0, 0, O_CU // C_WIDTH)),
                  pl.BlockSpec((n, t, C_WIDTH), lambda i: (0, 0, O_CV // C_WIDTH)),
                  vec, vec,
                  pl.BlockSpec(w_cols.shape, lambda i: (0, 0, 0)),
                  pl.BlockSpec(bs_rows.shape, lambda i: (0, 0))],
        out_specs=[blk, blk],
        out_shape=[shape, shape],
        compiler_params=_cparams("arbitrary"),
        name="cmix_sample",
    )(h3, h3, ln_g.reshape(1, 1, -1), ln_b.reshape(1, 1, -1), w_cols, bs_rows)


def _top_blocks(scores, allowed, axis):
    nb = scores.shape[axis]
    bidx = _iota(scores.shape, axis).astype(F32)
    cur = jnp.where(allowed, scores, NEG)
    sel = jnp.zeros(scores.shape, F32)
    for _ in range(D_TOPK):
        mx = cur.max(axis, keepdims=True)
        first = jnp.where(cur == mx, bidx, float(nb)).min(axis, keepdims=True)
        pick = bidx == first
        sel = jnp.where(pick, jnp.where(allowed, 1.0, sel), sel)
        cur = jnp.where(pick, BELOW_NEG, cur)
    return sel


MOBA_LANE_HEADS = 4


def _moba_prompt_kernel(q_ref, k_ref, v_ref, o_ref, km_s, kb_s, vt_s, bias_s):
    nb = kb_s.shape[0]
    g = D_HEADS // D_KV
    nq = g * D_BLOCK
    qb = pl.program_id(2)

    @pl.when(qb == 0)
    def _():
        for b in range(nb):
            rows = slice(b * D_BLOCK, (b + 1) * D_BLOCK)
            kblk = k_ref[rows, :]
            km_s[b:b + 1, :] = jnp.mean(kblk, axis=0, keepdims=True)
            kb_s[b] = kblk.astype(BF16)
            vt_s[b] = v_ref[rows, :].T.astype(BF16)

    qf = jnp.concatenate([q_ref[:, gi * HEAD_DIM:(gi + 1) * HEAD_DIM] for gi in range(g)], axis=0) * ATTN_SCALE
    q = qf.astype(BF16)
    scores = _dot_nt(km_s[...], qf, precision=lax.Precision.HIGHEST)
    sel = _top_blocks(scores, _iota((nb, nq), 0) < qb, axis=0)
    bias_s[...] = jnp.where(sel > 0.5, 0.0, NEG)

    gl = MOBA_LANE_HEADS
    width = gl * D_BLOCK
    causal = _iota((D_BLOCK, width), 0) <= _mod(_iota((D_BLOCK, width), 1), D_BLOCK)
    for part in range(g // gl):
        lanes = slice(part * width, (part + 1) * width)
        qp = q[part * width:(part + 1) * width]
        s = jnp.where(causal, _dot_nt(kb_s[qb], qp), NEG)
        m = s.max(0, keepdims=True)
        p = jnp.exp(s - m)
        l = p.sum(0, keepdims=True)
        acc = _dot(vt_s[qb], p.astype(BF16))

        def body(j, carry, qp=qp, lanes=lanes):
            m, l, acc = carry
            s = _dot_nt(kb_s[j], qp) + bias_s[pl.ds(j, 1), lanes]
            m_new = jnp.maximum(m, s.max(0, keepdims=True))
            alpha = jnp.exp(m - m_new)
            p = jnp.exp(s - m_new)
            return m_new, alpha * l + p.sum(0, keepdims=True), alpha * acc + _dot(vt_s[j], p.astype(BF16))

        m, l, acc = lax.fori_loop(0, qb, body, (m, l, acc))
        out = acc / l
        for gi in range(gl):
            col = (part * gl + gi) * HEAD_DIM
            o_ref[:, col:col + HEAD_DIM] = out[:, gi * D_BLOCK:(gi + 1) * D_BLOCK].T


RIDER_VMEM_LIMIT = 56 * 1024 * 1024


def moba_prompt_steps(h3):
    n, seq, _ = h3.shape
    return n * D_KV * (seq // D_BLOCK)


def moba_prompt(h3, rider=None):
    n, seq, _ = h3.shape
    g = D_HEADS // D_KV
    gw = g * HEAD_DIM
    nb = seq // D_BLOCK
    in_specs = [pl.BlockSpec((None, D_BLOCK, gw), lambda b, hh, i: (b, i, O_DQ // gw + hh)),
                pl.BlockSpec((None, seq, HEAD_DIM), lambda b, hh, i: (b, 0, O_DK // HEAD_DIM + hh)),
                pl.BlockSpec((None, seq, HEAD_DIM), lambda b, hh, i: (b, 0, O_DV // HEAD_DIM + hh))]
    out_spec = pl.BlockSpec((None, D_BLOCK, gw), lambda b, hh, i: (b, i, hh))
    out_shape = jax.ShapeDtypeStruct((n, seq, D_WIDTH), F32)
    scratch = [pltpu.VMEM((nb, HEAD_DIM), F32),
               pltpu.VMEM((nb, D_BLOCK, HEAD_DIM), BF16),
               pltpu.VMEM((nb, HEAD_DIM, D_BLOCK), BF16),
               pltpu.VMEM((nb, g * D_BLOCK), F32)]
    if rider is None:
        return pl.pallas_call(
            _moba_prompt_kernel, grid=(n, D_KV, nb), in_specs=in_specs, out_specs=out_spec, out_shape=out_shape,
            scratch_shapes=scratch, compiler_params=_cparams("parallel", "parallel", "arbitrary"),
            name="moba_prompt")(h3, h3, h3)

    parts = rider(block_index=lambda b, hh, i: (b * D_KV + hh) * nb + i, ndim=3)
    assert parts["steps"] == n * D_KV * nb
    n_in, n_rin, n_rout, n_s = len(in_specs), len(parts["in_specs"]), len(parts["out_specs"]), len(scratch)

    def kernel(*refs):
        own_in, rider_in = refs[:n_in], refs[n_in:n_in + n_rin]
        p = n_in + n_rin
        own_out, rider_out = refs[p], refs[p + 1:p + 1 + n_rout]
        p += 1 + n_rout
        _moba_prompt_kernel(*own_in, own_out, *refs[p:p + n_s])
        parts["kernel"](*rider_in, *rider_out, *refs[p + n_s:])

    outs = pl.pallas_call(
        kernel, grid=(n, D_KV, nb),
        in_specs=in_specs + parts["in_specs"],
        out_specs=[out_spec] + parts["out_specs"],
        out_shape=[out_shape] + parts["out_shape"],
        scratch_shapes=scratch + parts["scratch_shapes"],
        compiler_params=pltpu.CompilerParams(dimension_semantics=("arbitrary",) * 3,
                                             vmem_limit_bytes=RIDER_VMEM_LIMIT),
        name="moba_prompt_decode")(h3, h3, h3, *parts["args"])
    return outs[0], parts["finish"](outs[1:])


KV_PAGES_BLOCK = 4


def _kv_pages_kernel(k_ref, v_ref, o_ref):
    for pg in range(o_ref.shape[0]):
        rows = slice(pg * PAGE_SIZE, (pg + 1) * PAGE_SIZE)
        for kv, src in enumerate((k_ref, v_ref)):
            for hh in range(D_KV):
                o_ref[pg, kv, hh] = src[rows, hh * HEAD_DIM:(hh + 1) * HEAD_DIM]


def kv_pages(h3):
    n, seq, _ = h3.shape
    kw = D_KV * HEAD_DIM
    pb = KV_PAGES_BLOCK
    return pl.pallas_call(
        _kv_pages_kernel,
        grid=(n, seq // (pb * PAGE_SIZE)),
        in_specs=[pl.BlockSpec((None, pb * PAGE_SIZE, kw), lambda b, p: (b, p, O_DK // kw)),
                  pl.BlockSpec((None, pb * PAGE_SIZE, kw), lambda b, p: (b, p, O_DV // kw))],
        out_specs=pl.BlockSpec((None, pb, 2, D_KV, PAGE_SIZE, HEAD_DIM), lambda b, p: (b, p, 0, 0, 0, 0)),
        out_shape=jax.ShapeDtypeStruct((n, seq // PAGE_SIZE, 2, D_KV, PAGE_SIZE, HEAD_DIM), F32),
        compiler_params=_cparams("parallel", "parallel"),
        name="kv_pages",
    )(h3, h3)


MOBA_CHUNK_PAGES = 16
MOBA_VMEM_LIMIT = 52 * 1024 * 1024


def _moba_decode_kernel(pt_ref, q_ref, kn_ref, vn_ref, pool_ref, o_ref, buf, sem, lg_s, km_s, p_s, bias_s, e_s):
    seq = pl.program_id(0)
    n_seq = pl.num_programs(0)
    n_pages = buf.shape[1]
    cp = MOBA_CHUNK_PAGES
    ppb = D_BLOCK // PAGE_SIZE
    n_blk = n_pages // ppb
    g = D_HEADS // D_KV
    nq = q_ref.shape[1]
    lane_blk = (nq, PAGE_SIZE)

    def page_copies(s, slot):
        return [pltpu.make_async_copy(pool_ref.at[pt_ref[s, pg]], buf.at[slot, pg], sem.at[slot])
                for pg in range(n_pages)]

    @pl.when(seq == 0)
    def _():
        for c in page_copies(0, 0):
            c.start()
        e_s[...] = jnp.where(_div(_iota(e_s.shape, 1), PAGE_SIZE) == _iota(e_s.shape, 0), 1.0, 0.0).astype(BF16)

    @pl.when(seq + 1 < n_seq)
    def _():
        for c in page_copies(seq + 1, (seq + 1) % 2):
            c.start()

    slot = seq % 2
    for c in page_copies(seq, slot):
        c.wait()

    for hk in range(D_KV):
        qf = q_ref[hk] * ATTN_SCALE
        q = qf.astype(BF16)
        for c in range(n_pages // cp):
            ks = [buf[slot, c * cp + i, 0, hk] for i in range(cp)]
            lg_s[hk, :, c * cp * PAGE_SIZE:(c + 1) * cp * PAGE_SIZE] = _dot_nt(
                q, jnp.concatenate([k.astype(BF16) for k in ks], axis=0))
            sums = [k.sum(axis=0, keepdims=True) for k in ks]
            means = [functools.reduce(lambda a, b: a + b, sums[b * ppb:(b + 1) * ppb]) * (1.0 / D_BLOCK)
                     for b in range(cp // ppb)]
            km_s[hk, c * (cp // ppb):(c + 1) * (cp // ppb), :] = jnp.concatenate(means, axis=0)

        scores = _dot_nt(qf, km_s[hk], precision=lax.Precision.HIGHEST)
        sel = _top_blocks(scores, _iota(scores.shape, 1) >= 0, axis=1)
        bias_s[...] = jnp.where(_dot(sel.astype(BF16), e_s[...]) > 0.5, 0.0, NEG)

        own = _dot_nt(q, kn_ref[hk].astype(BF16))
        own = jnp.where(_iota(own.shape, 1) <= _div(_iota(own.shape, 0), g), own, NEG)

        def masked(pg):
            b = pg // ppb
            return lg_s[hk, :, pg * PAGE_SIZE:(pg + 1) * PAGE_SIZE] + bias_s[:, b * PAGE_SIZE:(b + 1) * PAGE_SIZE]

        mx = masked(0)
        for pg in range(1, n_pages):
            mx = jnp.maximum(mx, masked(pg))
        m = jnp.maximum(mx.max(-1, keepdims=True), own.max(-1, keepdims=True))
        m_rep = jnp.broadcast_to(m, lane_blk)
        tot = jnp.zeros(lane_blk, F32)
        for pg in range(n_pages):
            p = jnp.exp(masked(pg) - m_rep)
            tot = tot + p
            p_s[hk, :, pg * PAGE_SIZE:(pg + 1) * PAGE_SIZE] = p.astype(BF16)
        po = jnp.exp(own - m)
        den = tot.sum(-1, keepdims=True) + po.sum(-1, keepdims=True)

        acc = _dot(po.astype(BF16), vn_ref[hk].astype(BF16))
        for c in range(n_pages // cp):
            v = jnp.concatenate([buf[slot, c * cp + i, 1, hk].astype(BF16) for i in range(cp)], axis=0)
            acc = acc + _dot(p_s[hk, :, c * cp * PAGE_SIZE:(c + 1) * cp * PAGE_SIZE], v)
        o_ref[hk] = acc / den


def moba_decode(pool, page_table, q, k_new, v_new):
    n, n_pages = page_table.shape
    assert (n_pages * PAGE_SIZE) % D_BLOCK == 0 and n_pages % MOBA_CHUNK_PAGES == 0
    nq, t = q.shape[2], k_new.shape[2]
    n_blk = n_pages * PAGE_SIZE // D_BLOCK
    keys = n_pages * PAGE_SIZE

    def per_seq(r):
        return pl.BlockSpec((None, D_KV, r, HEAD_DIM), lambda b, pt: (b, 0, 0, 0))

    grid_spec = pltpu.PrefetchScalarGridSpec(
        num_scalar_prefetch=1,
        grid=(n,),
        in_specs=[per_seq(nq), per_seq(t), per_seq(t), pl.BlockSpec(memory_space=pl.ANY)],
        out_specs=per_seq(nq),
        scratch_shapes=[pltpu.VMEM((2, n_pages) + pool.shape[1:], F32),
                        pltpu.SemaphoreType.DMA((2,)),
                        pltpu.VMEM((D_KV, nq, keys), F32),
                        pltpu.VMEM((D_KV, n_blk, HEAD_DIM), F32),
                        pltpu.VMEM((D_KV, nq, keys), BF16),
                        pltpu.VMEM((nq, n_blk * PAGE_SIZE), F32),
                        pltpu.VMEM((n_blk, n_blk * PAGE_SIZE), BF16)])
    return pl.pallas_call(
        _moba_decode_kernel,
        grid_spec=grid_spec,
        out_shape=jax.ShapeDtypeStruct((n, D_KV, nq, HEAD_DIM), F32),
        compiler_params=pltpu.CompilerParams(dimension_semantics=("arbitrary",), vmem_limit_bytes=MOBA_VMEM_LIMIT),
        name="moba_decode",
    )(page_table, q, k_new, v_new, pool)


def even_layer_prompt(x, w_in, w_out, g_pre, g_post, sink):
    n, s, d = x.shape
    x2 = x.reshape(n * s, d)
    h = norm_proj(x2, g_pre, w_in, tm=1024)
    h3 = h.reshape(n, s, IN_EVEN)
    a_out = attn_a_prompt(h3).reshape(n * s, A_WIDTH)
    b_out = attn_b_prompt(h3, sink).reshape(n * s, B_WIDTH)
    y = out_proj(x2, [a_out, b_out], h, [E_AG, E_BG, E_BG + GATE_W], w_out, g_post)
    rows = [kv_rows(h3, E_AK + gi * A_WIDTH, E_AV + gi * A_WIDTH, A_HEADS, win) for gi, (win, _) in enumerate(A_GROUPS)]
    rows.append(kv_rows(h3, E_BK, E_BV, B_KV, B_WINDOW))
    return y.reshape(n, s, d), rows


def even_layer_sample(x, bufs_a, buf_b, w_in, w_out, g_pre, g_post, sink, ride=None):
    n, t, d = x.shape
    x2 = x.reshape(n * t, d)
    h = norm_proj(x2, g_pre, w_in, tm=n * t)
    big = max(range(A_NG), key=lambda gi: bufs_a[gi].size)
    stats, new_bufs = [], []
    for gi, (_, dil) in enumerate(A_GROUPS):
        cols = slice(gi * A_WIDTH, (gi + 1) * A_WIDTH)
        q = h[:, E_AQ:E_AK][:, cols].reshape(n, t * A_HEADS, HEAD_DIM)
        k = h[:, E_AK:E_AV][:, cols].reshape(n, t * A_HEADS, HEAD_DIM)
        v = h[:, E_AV:E_AG][:, cols].reshape(n, t * A_HEADS, HEAD_DIM)
        if ride is not None and gi == big:
            buf, o, m, l = ride(functools.partial(decode_parts, bufs_a[gi], q, k, v, A_HEADS, A_HEADS, dil, seqs=1))
        else:
            buf, o, m, l = decode_attn(bufs_a[gi], q, k, v, A_HEADS, A_HEADS, dil)
        stats.append((o, m, l))
        new_bufs.append(buf)
    a_out = merge_groups(stats).reshape(n * t, A_WIDTH)
    q = h[:, E_BQ:E_BK].reshape(n, t * B_HEADS, HEAD_DIM)
    k = h[:, E_BK:E_BV].reshape(n, t * B_KV, HEAD_DIM)
    v = h[:, E_BV:E_BG].reshape(n, t * B_KV, HEAD_DIM)
    sink_rows = jnp.broadcast_to(jnp.tile(sink, t)[:, None], (t * B_HEADS, HEAD_DIM))
    buf, b_out = decode_attn(buf_b, q, k, v, B_KV, B_HEADS, 1, sink=sink_rows)
    new_bufs.append(buf)
    y = out_proj(x2, [a_out, b_out.reshape(n * t, B_WIDTH)], h, [E_AG, E_BG, E_BG + GATE_W], w_out, g_post)
    return y.reshape(n, t, d), new_bufs


def odd_layer_prompt(x, w_in, w_out, g_pre, g_post, ln_g, ln_b, ws, bs, rider=None):
    n, s, d = x.shape
    x2 = x.reshape(n * s, d)
    h = norm_proj(x2, g_pre, w_in, tm=1024)
    h3 = h.reshape(n, s, IN_ODD)
    c_out = cmix_prompt(h, ln_g, ln_b, ws, bs)
    if rider is None:
        d_out, rider_outs = moba_prompt(h3), None
    else:
        d_out, rider_outs = moba_prompt(h3, rider)
    gates = [O_CG, O_CG + GATE_W, O_DG, O_DG + GATE_W]
    y = out_proj(x2, [c_out, d_out.reshape(n * s, D_WIDTH)], h, gates, w_out, g_post)
    return y.reshape(n, s, d), kv_pages(h3), rider_outs


def odd_layer_sample(x, pool, page_table, w_in, w_out, g_pre, g_post, ln_g, ln_b, ws, bs):
    n, t, d = x.shape
    g = D_HEADS // D_KV
    x2 = x.reshape(n * t, d)
    h = norm_proj(x2, g_pre, w_in, tm=n * t)
    c_out, v_rows = cmix_sample(h.reshape(n, t, IN_ODD), ln_g, ln_b, ws, bs)
    q = h[:, O_DQ:O_DK].reshape(n, t, D_KV, g, HEAD_DIM).transpose(0, 2, 1, 3, 4).reshape(n, D_KV, t * g, HEAD_DIM)
    k = h[:, O_DK:O_DV].reshape(n, t, D_KV, HEAD_DIM).transpose(0, 2, 1, 3)
    v = h[:, O_DV:O_DG].reshape(n, t, D_KV, HEAD_DIM).transpose(0, 2, 1, 3)
    o = moba_decode(pool, page_table, q, k, v)
    d_out = o.reshape(n, D_KV, t, g, HEAD_DIM).transpose(0, 2, 1, 3, 4).reshape(n * t, D_WIDTH)
    gates = [O_CG, O_CG + GATE_W, O_DG, O_DG + GATE_W]
    y = out_proj(x2, [c_out.reshape(n * t, C_WIDTH), d_out], h, gates, w_out, g_post)
    return y.reshape(n, t, d), jnp.stack([k, v], axis=1), v_rows


def kernel(x_prompt, x_sample, cache_a1, cache_a2, cache_a3, cache_b, cache_d, page_table, norm_pre_even, norm_post_even, w_in_even, w_out_even, sink_b, norm_pre_odd, norm_post_odd, w_in_odd, w_out_odd, c_ln_g, c_ln_b, c_ws, c_bs):
    yp, ys = x_prompt, x_sample
    caches_a = (cache_a1, cache_a2, cache_a3)
    a_p, a_s = [[] for _ in range(A_NG)], [[] for _ in range(A_NG)]
    b_p, b_s, d_p, d_s, c_s = [], [], [], [], []

    def odd_prompt(x, li, rider=None):
        return odd_layer_prompt(x, w_in_odd[li].astype(BF16), w_out_odd[li].astype(BF16), norm_pre_odd[li],
                                norm_post_odd[li], c_ln_g[li], c_ln_b[li], c_ws[li], c_bs[li], rider)

    odd_done = None
    for layer in range(DEPTH):
        li = layer // 2
        if layer % 2 == 0:
            w_in, w_out = w_in_even[li].astype(BF16), w_out_even[li].astype(BF16)
            yp, rows = even_layer_prompt(yp, w_in, w_out, norm_pre_even[li], norm_post_even[li], sink_b[li])
            ride = None
            if layer + 1 < DEPTH and ys.shape[0] == yp.shape[0] * D_KV * (yp.shape[1] // D_BLOCK):
                def ride(make_parts, x=yp, lo=(layer + 1) // 2):
                    nonlocal odd_done
                    y_odd, kvp_odd, rider_outs = odd_prompt(x, lo, make_parts)
                    odd_done = (y_odd, kvp_odd)
                    return rider_outs
            ys, bufs = even_layer_sample(ys, [c[li] for c in caches_a], cache_b[li], w_in, w_out,
                                         norm_pre_even[li], norm_post_even[li], sink_b[li], ride)
            for gi in range(A_NG):
                a_p[gi].append(rows[gi])
                a_s[gi].append(bufs[gi])
            b_p.append(rows[A_NG])
            b_s.append(bufs[A_NG])
        else:
            w_in, w_out = w_in_odd[li].astype(BF16), w_out_odd[li].astype(BF16)
            if odd_done is not None:
                (yp, kvp), odd_done = odd_done, None
            else:
                yp, kvp, _ = odd_prompt(yp, li)
            ys, kvs, vrows = odd_layer_sample(ys, cache_d[li], page_table, w_in, w_out, norm_pre_odd[li],
                                              norm_post_odd[li], c_ln_g[li], c_ln_b[li], c_ws[li], c_bs[li])
            d_p.append(kvp)
            d_s.append(kvs)
            c_s.append(vrows)
    return (yp, ys, jnp.stack(a_p[0]), jnp.stack(a_s[0]), jnp.stack(a_p[1]), jnp.stack(a_s[1]),
            jnp.stack(a_p[2]), jnp.stack(a_s[2]), jnp.stack(b_p), jnp.stack(b_s),
            jnp.stack(d_p), jnp.stack(d_s), jnp.stack(c_s))
```

```python
import functools

import jax
import jax.numpy as jnp
from jax import lax
from jax.experimental import pallas as pl
from jax.experimental.pallas import tpu as pltpu

F32 = jnp.float32
BF16 = jnp.bfloat16

DEPTH = 2
PAGE_SIZE = 128
HEAD_DIM = 128
ATTN_SCALE = HEAD_DIM ** -0.5
A_GROUPS = ((128, 1), (512, 4), (2048, 16))
A_NG = len(A_GROUPS)
A_HEADS = 4
A_QKV = A_NG * A_HEADS * HEAD_DIM
A_WIDTH = A_HEADS * HEAD_DIM
B_HEADS = 8
B_KV = 2
B_WINDOW = 128
B_WIDTH = B_HEADS * HEAD_DIM
C_CHUNK = 128
C_GROUPS = 8
C_GDIM = 128
C_WIDTH = C_GROUPS * C_GDIM
D_HEADS = 8
D_KV = 2
D_BLOCK = 256
D_TOPK = 3
D_WIDTH = D_HEADS * HEAD_DIM
EPS = 1e-6
NEG = -1e30
BELOW_NEG = -3e38

E_AQ, E_AK, E_AV = 0, A_QKV, 2 * A_QKV
E_AG = 3 * A_QKV
E_BQ = E_AG + A_WIDTH
E_BK = E_BQ + B_WIDTH
E_BV = E_BK + B_KV * HEAD_DIM
E_BG = E_BV + B_KV * HEAD_DIM
IN_EVEN = E_BG + B_WIDTH
O_CU, O_CV, O_CG = 0, C_WIDTH, 2 * C_WIDTH
O_DQ = 3 * C_WIDTH
O_DK = O_DQ + D_WIDTH
O_DV = O_DK + D_KV * HEAD_DIM
O_DG = O_DV + D_KV * HEAD_DIM
IN_ODD = O_DG + D_WIDTH

BAND = 128
GATE_W = 512
VMEM_LIMIT = 48 * 1024 * 1024


def _cparams(*sem):
    return pltpu.CompilerParams(dimension_semantics=sem, vmem_limit_bytes=VMEM_LIMIT)


def _dot_nt(a, b, precision=None):
    return lax.dot_general(a, b, (((1,), (1,)), ((), ())), preferred_element_type=F32, precision=precision)


def _dot(a, b):
    return jnp.dot(a, b, preferred_element_type=F32)


def _iota(shape, dim):
    return lax.broadcasted_iota(jnp.int32, shape, dim)


def _div(x, d):
    assert d & (d - 1) == 0
    return x >> (d.bit_length() - 1)


def _mod(x, d):
    assert d & (d - 1) == 0
    return x & (d - 1)


def _norm_proj_kernel(x_ref, g_ref, w_ref, o_ref, xn_ref):
    @pl.when(pl.program_id(1) == 0)
    def _():
        x = x_ref[...]
        r = lax.rsqrt(jnp.mean(x * x, axis=-1, keepdims=True) + EPS)
        xn_ref[...] = (x * r * g_ref[...]).astype(BF16)

    o_ref[...] = _dot(xn_ref[...], w_ref[...])


def norm_proj(x, g, w, tm, tn=512):
    m, d = x.shape
    n = w.shape[1]
    return pl.pallas_call(
        _norm_proj_kernel,
        grid=(m // tm, n // tn),
        in_specs=[pl.BlockSpec((tm, d), lambda i, j: (i, 0)),
                  pl.BlockSpec((1, d), lambda i, j: (0, 0)),
                  pl.BlockSpec((d, tn), lambda i, j: (0, j))],
        out_specs=pl.BlockSpec((tm, tn), lambda i, j: (i, j)),
        out_shape=jax.ShapeDtypeStruct((m, n), F32),
        scratch_shapes=[pltpu.VMEM((tm, d), BF16)],
        compiler_params=_cparams("parallel", "arbitrary"),
        name="norm_proj",
    )(x, g.reshape(1, d), w)


def _masked_attention(q, k, v, visible):
    s = jnp.where(visible, _dot_nt(q, k), NEG)
    m = s.max(-1, keepdims=True)
    p = jnp.exp(s - m)
    return _dot(p.astype(BF16), v), m, p.sum(-1, keepdims=True)


def _rows(ref, start, count, stride):
    if stride == 1:
        return ref[pl.ds(start, count), :]
    return ref[pl.ds(start, count, stride=stride), :]


A_TILE = 256


def _attn_a_prompt_kernel(q1, k1, v1, q2, k2, v2, q3, k3, v3, o_ref, o1_s, m1_s, l1_s, o2_s, m2_s, l2_s):
    seq = q1.shape[0]
    wide = (BAND, HEAD_DIM)
    tile = A_TILE
    qi = _iota((tile, tile + BAND), 0)
    kc = _iota((tile, tile + BAND), 1)
    vis_next = jnp.logical_and(kc >= qi, kc <= qi + BAND)
    qi0 = _iota((tile, tile), 0)
    kc0 = _iota((tile, tile), 1)
    vis_first = jnp.logical_and(kc0 <= qi0, kc0 >= qi0 - BAND)

    def banded(qr, kr, vr, base, stride, t):
        first = base + stride * t * tile
        q = (_rows(qr, first, tile, stride) * ATTN_SCALE).astype(BF16)
        if t == 0:
            k, v, vis = _rows(kr, first, tile, stride), _rows(vr, first, tile, stride), vis_first
        else:
            back = first - stride * BAND
            k, v, vis = _rows(kr, back, tile + BAND, stride), _rows(vr, back, tile + BAND, stride), vis_next
        return _masked_attention(q, k.astype(BF16), v.astype(BF16), vis)

    for t in range(seq // tile):
        rows = slice(t * tile, (t + 1) * tile)
        o, m, l = banded(q1, k1, v1, 0, 1, t)
        o1_s[rows, :] = o
        m1_s[rows, :] = jnp.broadcast_to(m, (tile, HEAD_DIM))
        l1_s[rows, :] = jnp.broadcast_to(l, (tile, HEAD_DIM))

    d2 = A_GROUPS[1][1]
    sub2 = seq // d2
    for r in range(d2):
        for t in range(sub2 // tile):
            rows = slice(t * tile, (t + 1) * tile)
            o, m, l = banded(q2, k2, v2, r, d2, t)
            o2_s[r, rows, :] = o
            m2_s[r, rows, :] = jnp.broadcast_to(m, (tile, HEAD_DIM))
            l2_s[r, rows, :] = jnp.broadcast_to(l, (tile, HEAD_DIM))

    d3 = A_GROUPS[2][1]
    ratio = d3 // d2
    pack = tile // BAND
    same = _div(qi0, BAND) == _div(kc0, BAND)
    vis_pack = jnp.logical_and(same, _mod(kc0, BAND) <= _mod(qi0, BAND))
    for r in range(d3):
        if r % pack == 0:
            def packed(ref, scale=None, r=r):
                x = jnp.concatenate([_rows(ref, r + u, BAND, d3) for u in range(pack)], axis=0)
                return (x if scale is None else x * scale).astype(BF16)
            o3p, m3p, l3p = _masked_attention(packed(q3, ATTN_SCALE), packed(k3), packed(v3), vis_pack)
        part = slice((r % pack) * BAND, (r % pack + 1) * BAND)
        o3 = o3p[part]
        m3 = jnp.broadcast_to(m3p[part], wide)
        l3 = jnp.broadcast_to(l3p[part], wide)
        o1 = o1_s[pl.ds(r, BAND, stride=d3), :]
        m1 = m1_s[pl.ds(r, BAND, stride=d3), :]
        l1 = l1_s[pl.ds(r, BAND, stride=d3), :]
        r2, off2 = r % d2, r // d2
        o2 = o2_s[r2, pl.ds(off2, BAND, stride=ratio), :]
        m2 = m2_s[r2, pl.ds(off2, BAND, stride=ratio), :]
        l2 = l2_s[r2, pl.ds(off2, BAND, stride=ratio), :]
        mmax = jnp.maximum(jnp.maximum(m1, m2), m3)
        e1, e2, e3 = jnp.exp(m1 - mmax), jnp.exp(m2 - mmax), jnp.exp(m3 - mmax)
        num = o1 * e1 + o2 * e2 + o3 * e3
        den = l1 * e1 + l2 * e2 + l3 * e3
        o_ref[pl.ds(r, BAND, stride=d3), :] = num / den


def attn_a_prompt(h3):
    n, seq, _ = h3.shape
    assert seq == A_GROUPS[2][0] and all(w // d == BAND for w, d in A_GROUPS)

    def spec(off, gi):
        blk = off // HEAD_DIM + gi * A_HEADS
        return pl.BlockSpec((None, seq, HEAD_DIM), lambda b, hh: (b, 0, blk + hh))

    in_specs = []
    for gi in range(A_NG):
        in_specs += [spec(E_AQ, gi), spec(E_AK, gi), spec(E_AV, gi)]
    d2 = A_GROUPS[1][1]
    stat = pltpu.VMEM((seq, HEAD_DIM), F32)
    stat2 = pltpu.VMEM((d2, seq // d2, HEAD_DIM), F32)
    return pl.pallas_call(
        _attn_a_prompt_kernel,
        grid=(n, A_HEADS),
        in_specs=in_specs,
        out_specs=pl.BlockSpec((None, seq, HEAD_DIM), lambda b, hh: (b, 0, hh)),
        out_shape=jax.ShapeDtypeStruct((n, seq, A_WIDTH), F32),
        scratch_shapes=[stat, stat, stat, stat2, stat2, stat2],
        compiler_params=_cparams("parallel", "parallel"),
        name="attn_a_prompt",
    )(*([h3] * 9))


def _attn_b_prompt_kernel(sink_ref, q_ref, k_ref, v_ref, o_ref, kb_s, vt_s):
    seq = q_ref.shape[0]
    g = B_HEADS // B_KV
    nq = g * BAND
    nblk = seq // BAND
    kvh = pl.program_id(1)
    for b in range(nblk):
        rows = slice(b * BAND, (b + 1) * BAND)
        kb_s[rows, :] = k_ref[rows, :].astype(BF16)
        vt_s[b] = v_ref[rows, :].T.astype(BF16)
    head = _div(_iota((1, nq), 1), BAND)
    sink = jnp.zeros((1, nq), F32)
    for gi in range(g):
        sink = jnp.where(head == gi, sink_ref[kvh * g + gi], sink)
    qi = _mod(_iota((2 * BAND, nq), 1), BAND)
    key = _iota((2 * BAND, nq), 0)
    band = jnp.logical_and(key >= qi, key <= qi + BAND)
    for b in range(nblk):
        rows = slice(b * BAND, (b + 1) * BAND)
        qb = q_ref[rows, :]
        q = jnp.concatenate([qb[:, gi * HEAD_DIM:(gi + 1) * HEAD_DIM] for gi in range(g)], axis=0)
        q = (q * ATTN_SCALE).astype(BF16)
        if b == 0:
            s = jnp.where(band[BAND:], _dot_nt(kb_s[rows, :], q), NEG)
        else:
            s = jnp.where(band, _dot_nt(kb_s[(b - 1) * BAND:(b + 1) * BAND, :], q), NEG)
        m = s.max(0, keepdims=True)
        pf = jnp.exp(s - m)
        l = pf.sum(0, keepdims=True)
        p = pf.astype(BF16)
        if b == 0:
            o = _dot(vt_s[0], p)
        else:
            o = _dot(vt_s[b - 1], p[:BAND]) + _dot(vt_s[b], p[BAND:])
        m2 = jnp.maximum(m, sink)
        a = jnp.exp(m - m2)
        out = o * (a / (l * a + jnp.exp(sink - m2)))
        for gi in range(g):
            o_ref[rows, gi * HEAD_DIM:(gi + 1) * HEAD_DIM] = out[:, gi * BAND:(gi + 1) * BAND].T


def attn_b_prompt(h3, sink):
    n, seq, _ = h3.shape
    gw = (B_HEADS // B_KV) * HEAD_DIM
    return pl.pallas_call(
        _attn_b_prompt_kernel,
        grid=(n, B_KV),
        in_specs=[pl.BlockSpec(memory_space=pltpu.SMEM),
                  pl.BlockSpec((None, seq, gw), lambda b, hh: (b, 0, E_BQ // gw + hh)),
                  pl.BlockSpec((None, seq, HEAD_DIM), lambda b, hh: (b, 0, E_BK // HEAD_DIM + hh)),
                  pl.BlockSpec((None, seq, HEAD_DIM), lambda b, hh: (b, 0, E_BV // HEAD_DIM + hh))],
        out_specs=pl.BlockSpec((None, seq, gw), lambda b, hh: (b, 0, hh)),
        out_shape=jax.ShapeDtypeStruct((n, seq, B_WIDTH), F32),
        scratch_shapes=[pltpu.VMEM((seq, HEAD_DIM), BF16), pltpu.VMEM((seq // BAND, HEAD_DIM, BAND), BF16)],
        compiler_params=_cparams("parallel", "parallel"),
        name="attn_b_prompt",
    )(sink, h3, h3, h3)


KV_ROWS_BLOCK = 512


def _kv_rows_kernel(k_ref, v_ref, o_ref, *, heads):
    npos = k_ref.shape[0]
    for kv, src in enumerate((k_ref, v_ref)):
        for hh in range(heads):
            o_ref[pl.ds(kv * heads + hh, npos, stride=2 * heads), :] = src[:, hh * HEAD_DIM:(hh + 1) * HEAD_DIM]


def kv_rows(h3, k_off, v_off, heads, window):
    n, seq, _ = h3.shape
    w = min(window, seq)
    hw = heads * HEAD_DIM
    npos = min(w, KV_ROWS_BLOCK)
    first = (seq - w) // npos
    rows = npos * 2 * heads
    out = pl.pallas_call(
        functools.partial(_kv_rows_kernel, heads=heads),
        grid=(n, w // npos),
        in_specs=[pl.BlockSpec((None, npos, hw), lambda b, c: (b, first + c, k_off // hw)),
                  pl.BlockSpec((None, npos, hw), lambda b, c: (b, first + c, v_off // hw))],
        out_specs=pl.BlockSpec((None, rows, HEAD_DIM), lambda b, c: (b, c, 0)),
        out_shape=jax.ShapeDtypeStruct((n, w * 2 * heads, HEAD_DIM), F32),
        compiler_params=_cparams("parallel", "parallel"),
        name="kv_rows",
    )(h3, h3)
    return out.reshape(n, w, 2, heads, HEAD_DIM)


DEC_CHUNK = 2048
DEC_UNROLL = 4


def _decode_kernel(*refs, heads, q_heads, dil, use_sink, ndim=1):
    if use_sink:
        sink_ref, q_ref, kn_ref, vn_ref, nf_ref, c_ref, nc_ref, out_ref, bias_s, biasn_s, l_s, xb_s = refs
    else:
        q_ref, kn_ref, vn_ref, nf_ref, c_ref, nc_ref, out_ref, m_ref, l_ref, bias_s, biasn_s, l_s, xb_s = refs
    slots = 2 * heads
    seqs, rows = c_ref.shape[:2]
    lb = rows // slots
    nq = q_ref.shape[1]
    t_new = kn_ref.shape[1] // heads
    per_kv = q_heads // heads
    chunk = min(DEC_CHUNK, rows)
    fresh = nf_ref.shape[1]
    keep = min(dil, t_new)
    period, kept = dil * slots, keep * slots
    cchunk = chunk * keep // dil

    @pl.when(functools.reduce(jnp.logical_and, [pl.program_id(d) == 0 for d in range(ndim)]))
    def _():
        ncols = rows * keep // dil
        row = _iota((nq, ncols), 0)
        col = _iota((nq, ncols), 1)
        t = _div(row, q_heads)
        kvh = _div(_mod(row, q_heads), per_kv)
        within = _mod(col, kept)
        pos = _div(col, kept) * dil + _div(within, slots)
        ok = (_mod(within, slots) == kvh) & (pos >= t) & (_mod(lb + t - pos, dil) == 0)
        bias_s[...] = jnp.where(ok, 0.0, NEG)
        rown = _iota((nq, t_new * heads), 0)
        coln = _iota((nq, t_new * heads), 1)
        tn = _div(rown, q_heads)
        kvhn = _div(_mod(rown, q_heads), per_kv)
        tk = _div(coln, heads)
        okn = (_mod(coln, heads) == kvhn) & (tk <= tn) & (_mod(tn - tk, dil) == 0)
        biasn_s[...] = jnp.where(okn, 0.0, NEG)

    def one_seq(b, carry):
        q = (q_ref[b] * ATTN_SCALE).astype(BF16)
        sn = _dot_nt(q, kn_ref[b].astype(BF16)) + biasn_s[...]
        m = sn.max(-1, keepdims=True)
        for c in range(rows // chunk):
            sl = slice(c * chunk, (c + 1) * chunk)
            x = c_ref[b, sl, :]
            if c == 0:
                nc_ref[b, 0:chunk - fresh, :] = x[fresh:]
            else:
                nc_ref[b, c * chunk - fresh:(c + 1) * chunk - fresh, :] = x
            if keep < dil:
                x = jnp.concatenate([x[u * period:u * period + kept] for u in range(chunk // period)], axis=0)
            sl = slice(c * cchunk, (c + 1) * cchunk)
            xb = x.astype(BF16)
            xb_s[b, sl, :] = xb
            s = _dot_nt(q, xb) + bias_s[:, sl]
            l_s[b, :, sl] = s
            m = jnp.maximum(m, s.max(-1, keepdims=True))
        pn = jnp.exp(sn - m)
        den = pn.sum(-1, keepdims=True)
        o = _dot(pn.astype(BF16), vn_ref[b].astype(BF16))
        for c in range(rows // chunk):
            sl = slice(c * cchunk, (c + 1) * cchunk)
            p = jnp.exp(l_s[b, :, sl] - m)
            den = den + p.sum(-1, keepdims=True)
            o = o + _dot(pltpu.roll(p, heads, axis=1).astype(BF16), xb_s[b, sl, :])

        if use_sink:
            sink = sink_ref[...]
            m2 = jnp.maximum(m, sink)
            a = jnp.exp(m - m2)
            out_ref[b] = o * (a / (den * a + jnp.exp(sink - m2)))
        else:
            out_ref[b] = o
            m_ref[b] = jnp.broadcast_to(m, (nq, HEAD_DIM))
            l_ref[b] = jnp.broadcast_to(den, (nq, HEAD_DIM))

        nc_ref[b, pl.ds(rows - fresh, fresh), :] = nf_ref[b]
        return carry

    if seqs == 1:
        one_seq(0, 0)
    else:
        lax.fori_loop(0, seqs, one_seq, 0, unroll=min(seqs, DEC_UNROLL))


DEC_BLOCK_BYTES = 4 * 1024 * 1024


def decode_parts(cache, q, k_new, v_new, heads, q_heads, dil, sink=None, seqs=None, block_index=None, ndim=1):
    n, lb = cache.shape[:2]
    rows = lb * 2 * heads
    flat = cache.reshape(n, rows, HEAD_DIM)
    k4 = k_new.reshape(n, -1, 1, heads, HEAD_DIM)
    v4 = v_new.reshape(n, -1, 1, heads, HEAD_DIM)
    new_flat = jnp.concatenate([k4, v4], axis=2).reshape(n, -1, HEAD_DIM)
    if seqs is None:
        seqs = 1
        while seqs * 2 * rows * HEAD_DIM * 4 <= DEC_BLOCK_BYTES and n % (seqs * 2) == 0:
            seqs *= 2
    if block_index is None:
        block_index = lambda b: b
    nq = q.shape[1]
    tk = k_new.shape[1]
    use_sink = sink is not None
    assert lb % dil == 0 and min(DEC_CHUNK, rows) % (dil * 2 * heads) == 0
    cols = rows * min(dil, tk // heads) // dil

    def per_seq(r):
        return pl.BlockSpec((seqs, r, HEAD_DIM), lambda *ids: (block_index(*ids), 0, 0))

    in_specs = [per_seq(nq), per_seq(tk), per_seq(tk), per_seq(new_flat.shape[1]), per_seq(rows)]
    args = [q, k_new, v_new, new_flat, flat]
    stat = jax.ShapeDtypeStruct((n, nq, HEAD_DIM), F32)
    out_shape = [jax.ShapeDtypeStruct(flat.shape, F32), stat]
    out_specs = [per_seq(rows), per_seq(nq)]
    if use_sink:
        in_specs = [pl.BlockSpec((nq, HEAD_DIM), lambda *ids: (0, 0))] + in_specs
        args = [sink] + args
    else:
        out_shape += [stat, stat]
        out_specs += [per_seq(nq), per_seq(nq)]
    return dict(
        kernel=functools.partial(_decode_kernel, heads=heads, q_heads=q_heads, dil=dil, use_sink=use_sink, ndim=ndim),
        in_specs=in_specs, args=args, out_specs=out_specs, out_shape=out_shape, steps=n // seqs,
        scratch_shapes=[pltpu.VMEM((nq, cols), F32), pltpu.VMEM((nq, tk), F32), pltpu.VMEM((seqs, nq, cols), F32),
                        pltpu.VMEM((seqs, cols, HEAD_DIM), BF16)],
        finish=lambda outs: (outs[0].reshape(cache.shape),) + tuple(outs[1:]))


def decode_attn(cache, q, k_new, v_new, heads, q_heads, dil, sink=None):
    parts = decode_parts(cache, q, k_new, v_new, heads, q_heads, dil, sink)
    outs = pl.pallas_call(
        parts["kernel"],
        grid=(parts["steps"],),
        in_specs=parts["in_specs"],
        out_specs=parts["out_specs"],
        out_shape=parts["out_shape"],
        scratch_shapes=parts["scratch_shapes"],
        compiler_params=_cparams("arbitrary"),
        name="decode_attn",
    )(*parts["args"])
    return parts["finish"](outs)


def _merge_kernel(*refs):
    o_ref = refs[-1]
    stats = [refs[3 * g:3 * g + 3] for g in range(A_NG)]
    mmax = functools.reduce(jnp.maximum, [m[...] for _, m, _ in stats])
    num = 0.0
    den = 0.0
    for o, m, l in stats:
        e = jnp.exp(m[...] - mmax)
        num = num + o[...] * e
        den = den + l[...] * e
    o_ref[...] = num / den


def merge_groups(stats):
    flat = [a.reshape(-1, HEAD_DIM) for st in stats for a in st]
    out = pl.pallas_call(_merge_kernel, out_shape=jax.ShapeDtypeStruct(flat[0].shape, F32),
                         compiler_params=pltpu.CompilerParams(vmem_limit_bytes=VMEM_LIMIT),
                         name="merge_groups")(*flat)
    return out


def _out_proj_kernel(*refs, pieces):
    x_ref = refs[0]
    o_refs = refs[1:1 + pieces]
    g_refs = refs[1 + pieces:1 + 2 * pieces]
    w_ref, gp_ref, y_ref = refs[1 + 2 * pieces:]
    acc = None
    for p in range(pieces):
        g = g_refs[p][...]
        mix = (o_refs[p][...] * (g * (1.0 / (1.0 + jnp.exp(-g))))).astype(BF16)
        d = _dot(mix, w_ref[p * GATE_W:(p + 1) * GATE_W, :])
        acc = d if acc is None else acc + d
    r = lax.rsqrt(jnp.mean(acc * acc, axis=-1, keepdims=True) + EPS)
    y_ref[...] = x_ref[...] + acc * r * gp_ref[...]


def out_proj(x, outs, h, gate_offs, w_out, g_post, tm=512):
    m, d = x.shape
    tm = min(tm, m)
    o_args, o_specs = [], []
    for arr in outs:
        for c in range(arr.shape[1] // GATE_W):
            o_args.append(arr)
            o_specs.append(pl.BlockSpec((tm, GATE_W), lambda i, c=c: (i, c)))
    g_specs = [pl.BlockSpec((tm, GATE_W), lambda i, c=off // GATE_W: (i, c)) for off in gate_offs]
    pieces = len(o_args)
    assert pieces == len(gate_offs) and pieces * GATE_W == w_out.shape[0]
    return pl.pallas_call(
        functools.partial(_out_proj_kernel, pieces=pieces),
        grid=(m // tm,),
        in_specs=[pl.BlockSpec((tm, d), lambda i: (i, 0))] + o_specs + g_specs + [
            pl.BlockSpec(w_out.shape, lambda i: (0, 0), pipeline_mode=pl.Buffered(1)),
            pl.BlockSpec((1, d), lambda i: (0, 0))],
        out_specs=pl.BlockSpec((tm, d), lambda i: (i, 0)),
        out_shape=jax.ShapeDtypeStruct((m, d), F32),
        compiler_params=_cparams("parallel"),
        name="out_proj",
    )(x, *o_args, *([h] * pieces), w_out, g_post.reshape(1, d))


def _layernorm(x, g, b):
    xc = x - jnp.mean(x, axis=-1, keepdims=True)
    y = xc * lax.rsqrt(jnp.mean(xc * xc, axis=-1, keepdims=True) + EPS)
    return y * g + b


def _cmix_prompt_kernel(cu_ref, cv_ref, lg_ref, lb_ref, ws_ref, bs_ref, o_ref):
    v = _layernorm(cv_ref[...], lg_ref[...], lb_ref[...]).astype(BF16)
    tril = _iota((C_CHUNK, C_CHUNK), 1) <= _iota((C_CHUNK, C_CHUNK), 0)
    for g in range(C_GROUPS):
        cols = slice(g * C_GDIM, (g + 1) * C_GDIM)
        w = jnp.where(tril, ws_ref[g], 0.0).astype(BF16)
        for c in range(cu_ref.shape[0] // C_CHUNK):
            rows = slice(c * C_CHUNK, (c + 1) * C_CHUNK)
            o_ref[rows, cols] = cu_ref[rows, cols] * (_dot(w, v[rows, cols]) + bs_ref[:, cols])


def cmix_prompt(h, ln_g, ln_b, ws, bs, tm=512):
    m = h.shape[0]
    bs_rows = jnp.repeat(bs.T, C_GDIM, axis=1)
    vec = pl.BlockSpec((1, C_WIDTH), lambda i: (0, 0))
    return pl.pallas_call(
        _cmix_prompt_kernel,
        grid=(m // tm,),
        in_specs=[pl.BlockSpec((tm, C_WIDTH), lambda i: (i, O_CU // C_WIDTH)),
                  pl.BlockSpec((tm, C_WIDTH), lambda i: (i, O_CV // C_WIDTH)),
                  vec, vec,
                  pl.BlockSpec(ws.shape, lambda i: (0, 0, 0)),
                  pl.BlockSpec(bs_rows.shape, lambda i: (0, 0))],
        out_specs=pl.BlockSpec((tm, C_WIDTH), lambda i: (i, 0)),
        out_shape=jax.ShapeDtypeStruct((m, C_WIDTH), F32),
        compiler_params=_cparams("parallel"),
        name="cmix_prompt",
    )(h, h, ln_g.reshape(1, -1), ln_b.reshape(1, -1), ws, bs_rows)


def _cmix_sample_kernel(cu_ref, cv_ref, lg_ref, lb_ref, w_ref, bs_ref, o_ref, v_ref):
    v = _layernorm(cv_ref[...], lg_ref[...], lb_ref[...])
    v_ref[...] = v
    t = v.shape[1]
    i = _iota((t, C_WIDTH), 0)
    acc = jnp.broadcast_to(bs_ref[...][None], v.shape)
    for j in range(t):
        wj = jnp.where(i >= j, w_ref[j], 0.0)
        acc = acc + wj[None] * v[:, j:j + 1, :]
    o_ref[...] = cu_ref[...] * acc


def cmix_sample(h3, ln_g, ln_b, ws, bs):
    n, t, _ = h3.shape
    w_cols = jnp.repeat(ws[:, :t, :t].transpose(2, 1, 0), C_GDIM, axis=2)
    bs_rows = jnp.repeat(bs[:, :t].T, C_GDIM, axis=1)
    vec = pl.BlockSpec((1, 1, C_WIDTH), lambda i: (0, 0, 0))
    shape = jax.ShapeDtypeStruct((n, t, C_WIDTH), F32)
    blk = pl.BlockSpec((n, t, C_WIDTH), lambda i: (0, 0, 0))
    return pl.pallas_call(
        _cmix_sample_kernel,
        grid=(1,),
        in_specs=[pl.BlockSpec((n, t, C_WIDTH), lambda i: (0, 0, O_CU // C_WIDTH)),
                  pl.BlockSpec((n, t, C_WIDTH), lambda i: (0, 0, O_CV // C_WIDTH)),
                  vec, vec,
                  pl.BlockSpec(w_cols.shape, lambda i: (0, 0, 0)),
                  pl.BlockSpec(bs_rows.shape, lambda i: (0, 0))],
        out_specs=[blk, blk],
        out_shape=[shape, shape],
        compiler_params=_cparams("arbitrary"),
        name="cmix_sample",
    )(h3, h3, ln_g.reshape(1, 1, -1), ln_b.reshape(1, 1, -1), w_cols, bs_rows)


def _top_blocks(scores, allowed, axis):
    nb = scores.shape[axis]
    bidx = _iota(scores.shape, axis).astype(F32)
    cur = jnp.where(allowed, scores, NEG)
    sel = jnp.zeros(scores.shape, F32)
    for _ in range(D_TOPK):
        mx = cur.max(axis, keepdims=True)
        first = jnp.where(cur == mx, bidx, float(nb)).min(axis, keepdims=True)
        pick = bidx == first
        sel = jnp.where(pick, jnp.where(allowed, 1.0, sel), sel)
        cur = jnp.where(pick, BELOW_NEG, cur)
    return sel


MOBA_LANE_HEADS = 4


def _moba_prompt_kernel(q_ref, k_ref, v_ref, o_ref, km_s, kb_s, vt_s, bias_s):
    nb = kb_s.shape[0]
    g = D_HEADS // D_KV
    nq = g * D_BLOCK
    qb = pl.program_id(2)

    @pl.when(qb == 0)
    def _():
        for b in range(nb):
            rows = slice(b * D_BLOCK, (b + 1) * D_BLOCK)
            kblk = k_ref[rows, :]
            km_s[b:b + 1, :] = jnp.mean(kblk, axis=0, keepdims=True)
            kb_s[b] = kblk.astype(BF16)
            vt_s[b] = v_ref[rows, :].T.astype(BF16)

    qf = jnp.concatenate([q_ref[:, gi * HEAD_DIM:(gi + 1) * HEAD_DIM] for gi in range(g)], axis=0) * ATTN_SCALE
    q = qf.astype(BF16)
    scores = _dot_nt(km_s[...], qf, precision=lax.Precision.HIGHEST)
    sel = _top_blocks(scores, _iota((nb, nq), 0) < qb, axis=0)
    bias_s[...] = jnp.where(sel > 0.5, 0.0, NEG)

    gl = MOBA_LANE_HEADS
    width = gl * D_BLOCK
    causal = _iota((D_BLOCK, width), 0) <= _mod(_iota((D_BLOCK, width), 1), D_BLOCK)
    for part in range(g // gl):
        lanes = slice(part * width, (part + 1) * width)
        qp = q[part * width:(part + 1) * width]
        s = jnp.where(causal, _dot_nt(kb_s[qb], qp), NEG)
        m = s.max(0, keepdims=True)
        p = jnp.exp(s - m)
        l = p.sum(0, keepdims=True)
        acc = _dot(vt_s[qb], p.astype(BF16))

        def body(j, carry, qp=qp, lanes=lanes):
            m, l, acc = carry
            s = _dot_nt(kb_s[j], qp) + bias_s[pl.ds(j, 1), lanes]
            m_new = jnp.maximum(m, s.max(0, keepdims=True))
            alpha = jnp.exp(m - m_new)
            p = jnp.exp(s - m_new)
            return m_new, alpha * l + p.sum(0, keepdims=True), alpha * acc + _dot(vt_s[j], p.astype(BF16))

        m, l, acc = lax.fori_loop(0, qb, body, (m, l, acc))
        out = acc / l
        for gi in range(gl):
            col = (part * gl + gi) * HEAD_DIM
            o_ref[:, col:col + HEAD_DIM] = out[:, gi * D_BLOCK:(gi + 1) * D_BLOCK].T


RIDER_VMEM_LIMIT = 56 * 1024 * 1024


def moba_prompt(h3, rider=None):
    n, seq, _ = h3.shape
    g = D_HEADS // D_KV
    gw = g * HEAD_DIM
    nb = seq // D_BLOCK
    in_specs = [pl.BlockSpec((None, D_BLOCK, gw), lambda b, hh, i: (b, i, O_DQ // gw + hh)),
                pl.BlockSpec((None, seq, HEAD_DIM), lambda b, hh, i: (b, 0, O_DK // HEAD_DIM + hh)),
                pl.BlockSpec((None, seq, HEAD_DIM), lambda b, hh, i: (b, 0, O_DV // HEAD_DIM + hh))]
    out_spec = pl.BlockSpec((None, D_BLOCK, gw), lambda b, hh, i: (b, i, hh))
    out_shape = jax.ShapeDtypeStruct((n, seq, D_WIDTH), F32)
    scratch = [pltpu.VMEM((nb, HEAD_DIM), F32),
               pltpu.VMEM((nb, D_BLOCK, HEAD_DIM), BF16),
               pltpu.VMEM((nb, HEAD_DIM, D_BLOCK), BF16),
               pltpu.VMEM((nb, g * D_BLOCK), F32)]
    if rider is None:
        return pl.pallas_call(
            _moba_prompt_kernel, grid=(n, D_KV, nb), in_specs=in_specs, out_specs=out_spec, out_shape=out_shape,
            scratch_shapes=scratch, compiler_params=_cparams("parallel", "parallel", "arbitrary"),
            name="moba_prompt")(h3, h3, h3)

    parts = rider(block_index=lambda b, hh, i: (b * D_KV + hh) * nb + i, ndim=3)
    assert parts["steps"] == n * D_KV * nb
    n_in, n_rin, n_rout, n_s = len(in_specs), len(parts["in_specs"]), len(parts["out_specs"]), len(scratch)

    def kernel(*refs):
        own_in, rider_in = refs[:n_in], refs[n_in:n_in + n_rin]
        p = n_in + n_rin
        own_out, rider_out = refs[p], refs[p + 1:p + 1 + n_rout]
        p += 1 + n_rout
        _moba_prompt_kernel(*own_in, own_out, *refs[p:p + n_s])
        parts["kernel"](*rider_in, *rider_out, *refs[p + n_s:])

    outs = pl.pallas_call(
        kernel, grid=(n, D_KV, nb),
        in_specs=in_specs + parts["in_specs"],
        out_specs=[out_spec] + parts["out_specs"],
        out_shape=[out_shape] + parts["out_shape"],
        scratch_shapes=scratch + parts["scratch_shapes"],
        compiler_params=pltpu.CompilerParams(dimension_semantics=("arbitrary",) * 3,
                                             vmem_limit_bytes=RIDER_VMEM_LIMIT),
        name="moba_prompt_decode")(h3, h3, h3, *parts["args"])
    return outs[0], parts["finish"](outs[1:])


KV_PAGES_BLOCK = 4


def _kv_pages_kernel(k_ref, v_ref, o_ref):
    for pg in range(o_ref.shape[0]):
        rows = slice(pg * PAGE_SIZE, (pg + 1) * PAGE_SIZE)
        for kv, src in enumerate((k_ref, v_ref)):
            for hh in range(D_KV):
                o_ref[pg, kv, hh] = src[rows, hh * HEAD_DIM:(hh + 1) * HEAD_DIM]


def kv_pages(h3):
    n, seq, _ = h3.shape
    kw = D_KV * HEAD_DIM
    pb = KV_PAGES_BLOCK
    return pl.pallas_call(
        _kv_pages_kernel,
        grid=(n, seq // (pb * PAGE_SIZE)),
        in_specs=[pl.BlockSpec((None, pb * PAGE_SIZE, kw), lambda b, p: (b, p, O_DK // kw)),
                  pl.BlockSpec((None, pb * PAGE_SIZE, kw), lambda b, p: (b, p, O_DV // kw))],
        out_specs=pl.BlockSpec((None, pb, 2, D_KV, PAGE_SIZE, HEAD_DIM), lambda b, p: (b, p, 0, 0, 0, 0)),
        out_shape=jax.ShapeDtypeStruct((n, seq // PAGE_SIZE, 2, D_KV, PAGE_SIZE, HEAD_DIM), F32),
        compiler_params=_cparams("parallel", "parallel"),
        name="kv_pages",
    )(h3, h3)


MOBA_CHUNK_PAGES = 16
MOBA_VMEM_LIMIT = 52 * 1024 * 1024


def _moba_decode_kernel(pt_ref, q_ref, kn_ref, vn_ref, pool_ref, o_ref, buf, sem, lg_s, km_s, p_s, bias_s, e_s):
    seq = pl.program_id(0)
    n_seq = pl.num_programs(0)
    n_pages = buf.shape[1]
    cp = MOBA_CHUNK_PAGES
    ppb = D_BLOCK // PAGE_SIZE
    n_blk = n_pages // ppb
    g = D_HEADS // D_KV
    nq = q_ref.shape[1]
    lane_blk = (nq, PAGE_SIZE)

    def page_copies(s, slot):
        return [pltpu.make_async_copy(pool_ref.at[pt_ref[s, pg]], buf.at[slot, pg], sem.at[slot])
                for pg in range(n_pages)]

    @pl.when(seq == 0)
    def _():
        for c in page_copies(0, 0):
            c.start()
        e_s[...] = jnp.where(_div(_iota(e_s.shape, 1), PAGE_SIZE) == _iota(e_s.shape, 0), 1.0, 0.0).astype(BF16)

    @pl.when(seq + 1 < n_seq)
    def _():
        for c in page_copies(seq + 1, (seq + 1) % 2):
            c.start()

    slot = seq % 2
    for c in page_copies(seq, slot):
        c.wait()

    for hk in range(D_KV):
        qf = q_ref[hk] * ATTN_SCALE
        q = qf.astype(BF16)
        for c in range(n_pages // cp):
            ks = [buf[slot, c * cp + i, 0, hk] for i in range(cp)]
            lg_s[hk, :, c * cp * PAGE_SIZE:(c + 1) * cp * PAGE_SIZE] = _dot_nt(
                q, jnp.concatenate([k.astype(BF16) for k in ks], axis=0))
            sums = [k.sum(axis=0, keepdims=True) for k in ks]
            means = [functools.reduce(lambda a, b: a + b, sums[b * ppb:(b + 1) * ppb]) * (1.0 / D_BLOCK)
                     for b in range(cp // ppb)]
            km_s[hk, c * (cp // ppb):(c + 1) * (cp // ppb), :] = jnp.concatenate(means, axis=0)

        scores = _dot_nt(qf, km_s[hk], precision=lax.Precision.HIGHEST)
        sel = _top_blocks(scores, _iota(scores.shape, 1) >= 0, axis=1)
        bias_s[...] = jnp.where(_dot(sel.astype(BF16), e_s[...]) > 0.5, 0.0, NEG)

        own = _dot_nt(q, kn_ref[hk].astype(BF16))
        own = jnp.where(_iota(own.shape, 1) <= _div(_iota(own.shape, 0), g), own, NEG)

        def masked(pg):
            b = pg // ppb
            return lg_s[hk, :, pg * PAGE_SIZE:(pg + 1) * PAGE_SIZE] + bias_s[:, b * PAGE_SIZE:(b + 1) * PAGE_SIZE]

        mx = masked(0)
        for pg in range(1, n_pages):
            mx = jnp.maximum(mx, masked(pg))
        m = jnp.maximum(mx.max(-1, keepdims=True), own.max(-1, keepdims=True))
        m_rep = jnp.broadcast_to(m, lane_blk)
        tot = jnp.zeros(lane_blk, F32)
        for pg in range(n_pages):
            p = jnp.exp(masked(pg) - m_rep)
            tot = tot + p
            p_s[hk, :, pg * PAGE_SIZE:(pg + 1) * PAGE_SIZE] = p.astype(BF16)
        po = jnp.exp(own - m)
        den = tot.sum(-1, keepdims=True) + po.sum(-1, keepdims=True)

        acc = _dot(po.astype(BF16), vn_ref[hk].astype(BF16))
        for c in range(n_pages // cp):
            v = jnp.concatenate([buf[slot, c * cp + i, 1, hk].astype(BF16) for i in range(cp)], axis=0)
            acc = acc + _dot(p_s[hk, :, c * cp * PAGE_SIZE:(c + 1) * cp * PAGE_SIZE], v)
        o_ref[hk] = acc / den


def moba_decode(pool, page_table, q, k_new, v_new):
    n, n_pages = page_table.shape
    assert (n_pages * PAGE_SIZE) % D_BLOCK == 0 and n_pages % MOBA_CHUNK_PAGES == 0
    nq, t = q.shape[2], k_new.shape[2]
    n_blk = n_pages * PAGE_SIZE // D_BLOCK
    keys = n_pages * PAGE_SIZE

    def per_seq(r):
        return pl.BlockSpec((None, D_KV, r, HEAD_DIM), lambda b, pt: (b, 0, 0, 0))

    grid_spec = pltpu.PrefetchScalarGridSpec(
        num_scalar_prefetch=1,
        grid=(n,),
        in_specs=[per_seq(nq), per_seq(t), per_seq(t), pl.BlockSpec(memory_space=pl.ANY)],
        out_specs=per_seq(nq),
        scratch_shapes=[pltpu.VMEM((2, n_pages) + pool.shape[1:], F32),
                        pltpu.SemaphoreType.DMA((2,)),
                        pltpu.VMEM((D_KV, nq, keys), F32),
                        pltpu.VMEM((D_KV, n_blk, HEAD_DIM), F32),
                        pltpu.VMEM((D_KV, nq, keys), BF16),
                        pltpu.VMEM((nq, n_blk * PAGE_SIZE), F32),
                        pltpu.VMEM((n_blk, n_blk * PAGE_SIZE), BF16)])
    return pl.pallas_call(
        _moba_decode_kernel,
        grid_spec=grid_spec,
        out_shape=jax.ShapeDtypeStruct((n, D_KV, nq, HEAD_DIM), F32),
        compiler_params=pltpu.CompilerParams(dimension_semantics=("arbitrary",), vmem_limit_bytes=MOBA_VMEM_LIMIT),
        name="moba_decode",
    )(page_table, q, k_new, v_new, pool)


def even_layer_prompt(x, w_in, w_out, g_pre, g_post, sink):
    n, s, d = x.shape
    x2 = x.reshape(n * s, d)
    h = norm_proj(x2, g_pre, w_in, tm=1024)
    h3 = h.reshape(n, s, IN_EVEN)
    a_out = attn_a_prompt(h3).reshape(n * s, A_WIDTH)
    b_out = attn_b_prompt(h3, sink).reshape(n * s, B_WIDTH)
    y = out_proj(x2, [a_out, b_out], h, [E_AG, E_BG, E_BG + GATE_W], w_out, g_post)
    rows = [kv_rows(h3, E_AK + gi * A_WIDTH, E_AV + gi * A_WIDTH, A_HEADS, win) for gi, (win, _) in enumerate(A_GROUPS)]
    rows.append(kv_rows(h3, E_BK, E_BV, B_KV, B_WINDOW))
    return y.reshape(n, s, d), rows


def even_layer_sample(x, bufs_a, buf_b, w_in, w_out, g_pre, g_post, sink, ride=None):
    n, t, d = x.shape
    x2 = x.reshape(n * t, d)
    h = norm_proj(x2, g_pre, w_in, tm=n * t)
    big = max(range(A_NG), key=lambda gi: bufs_a[gi].size)
    stats, new_bufs = [], []
    for gi, (_, dil) in enumerate(A_GROUPS):
        cols = slice(gi * A_WIDTH, (gi + 1) * A_WIDTH)
        q = h[:, E_AQ:E_AK][:, cols].reshape(n, t * A_HEADS, HEAD_DIM)
        k = h[:, E_AK:E_AV][:, cols].reshape(n, t * A_HEADS, HEAD_DIM)
        v = h[:, E_AV:E_AG][:, cols].reshape(n, t * A_HEADS, HEAD_DIM)
        if ride is not None and gi == big:
            buf, o, m, l = ride(functools.partial(decode_parts, bufs_a[gi], q, k, v, A_HEADS, A_HEADS, dil, seqs=1))
        else:
            buf, o, m, l = decode_attn(bufs_a[gi], q, k, v, A_HEADS, A_HEADS, dil)
        stats.append((o, m, l))
        new_bufs.append(buf)
    a_out = merge_groups(stats).reshape(n * t, A_WIDTH)
    q = h[:, E_BQ:E_BK].reshape(n, t * B_HEADS, HEAD_DIM)
    k = h[:, E_BK:E_BV].reshape(n, t * B_KV, HEAD_DIM)
    v = h[:, E_BV:E_BG].reshape(n, t * B_KV, HEAD_DIM)
    sink_rows = jnp.broadcast_to(jnp.tile(sink, t)[:, None], (t * B_HEADS, HEAD_DIM))
    buf, b_out = decode_attn(buf_b, q, k, v, B_KV, B_HEADS, 1, sink=sink_rows)
    new_bufs.append(buf)
    y = out_proj(x2, [a_out, b_out.reshape(n * t, B_WIDTH)], h, [E_AG, E_BG, E_BG + GATE_W], w_out, g_post)
    return y.reshape(n, t, d), new_bufs


def odd_layer_prompt(x, w_in, w_out, g_pre, g_post, ln_g, ln_b, ws, bs, rider=None):
    n, s, d = x.shape
    x2 = x.reshape(n * s, d)
    h = norm_proj(x2, g_pre, w_in, tm=1024)
    h3 = h.reshape(n, s, IN_ODD)
    c_out = cmix_prompt(h, ln_g, ln_b, ws, bs)
    if rider is None:
        d_out, rider_outs = moba_prompt(h3), None
    else:
        d_out, rider_outs = moba_prompt(h3, rider)
    gates = [O_CG, O_CG + GATE_W, O_DG, O_DG + GATE_W]
    y = out_proj(x2, [c_out, d_out.reshape(n * s, D_WIDTH)], h, gates, w_out, g_post)
    return y.reshape(n, s, d), kv_pages(h3), rider_outs


def odd_layer_sample(x, pool, page_table, w_in, w_out, g_pre, g_post, ln_g, ln_b, ws, bs):
    n, t, d = x.shape
    g = D_HEADS // D_KV
    x2 = x.reshape(n * t, d)
    h = norm_proj(x2, g_pre, w_in, tm=n * t)
    c_out, v_rows = cmix_sample(h.reshape(n, t, IN_ODD), ln_g, ln_b, ws, bs)
    q = h[:, O_DQ:O_DK].reshape(n, t, D_KV, g, HEAD_DIM).transpose(0, 2, 1, 3, 4).reshape(n, D_KV, t * g, HEAD_DIM)
    k = h[:, O_DK:O_DV].reshape(n, t, D_KV, HEAD_DIM).transpose(0, 2, 1, 3)
    v = h[:, O_DV:O_DG].reshape(n, t, D_KV, HEAD_DIM).transpose(0, 2, 1, 3)
    o = moba_decode(pool, page_table, q, k, v)
    d_out = o.reshape(n, D_KV, t, g, HEAD_DIM).transpose(0, 2, 1, 3, 4).reshape(n * t, D_WIDTH)
    gates = [O_CG, O_CG + GATE_W, O_DG, O_DG + GATE_W]
    y = out_proj(x2, [c_out.reshape(n * t, C_WIDTH), d_out], h, gates, w_out, g_post)
    return y.reshape(n, t, d), jnp.stack([k, v], axis=1), v_rows


def kernel(x_prompt, x_sample, cache_a1, cache_a2, cache_a3, cache_b, cache_d, page_table, norm_pre_even, norm_post_even, w_in_even, w_out_even, sink_b, norm_pre_odd, norm_post_odd, w_in_odd, w_out_odd, c_ln_g, c_ln_b, c_ws, c_bs):
    yp, ys = x_prompt, x_sample
    caches_a = (cache_a1, cache_a2, cache_a3)
    a_p, a_s = [[] for _ in range(A_NG)], [[] for _ in range(A_NG)]
    b_p, b_s, d_p, d_s, c_s = [], [], [], [], []

    def odd_prompt(x, li, rider=None):
        return odd_layer_prompt(x, w_in_odd[li].astype(BF16), w_out_odd[li].astype(BF16), norm_pre_odd[li],
                                norm_post_odd[li], c_ln_g[li], c_ln_b[li], c_ws[li], c_bs[li], rider)

    odd_done = None
    for layer in range(DEPTH):
        li = layer // 2
        if layer % 2 == 0:
            w_in, w_out = w_in_even[li].astype(BF16), w_out_even[li].astype(BF16)
            yp, rows = even_layer_prompt(yp, w_in, w_out, norm_pre_even[li], norm_post_even[li], sink_b[li])
            ride = None
            if layer + 1 < DEPTH and ys.shape[0] == yp.shape[0] * D_KV * (yp.shape[1] // D_BLOCK):
                def ride(make_parts, x=yp, lo=(layer + 1) // 2):
                    nonlocal odd_done
                    y_odd, kvp_odd, rider_outs = odd_prompt(x, lo, make_parts)
                    odd_done = (y_odd, kvp_odd)
                    return rider_outs
            ys, bufs = even_layer_sample(ys, [c[li] for c in caches_a], cache_b[li], w_in, w_out,
                                         norm_pre_even[li], norm_post_even[li], sink_b[li], ride)
            for gi in range(A_NG):
                a_p[gi].append(rows[gi])
                a_s[gi].append(bufs[gi])
            b_p.append(rows[A_NG])
            b_s.append(bufs[A_NG])
        else:
            w_in, w_out = w_in_odd[li].astype(BF16), w_out_odd[li].astype(BF16)
            if odd_done is not None:
                (yp, kvp), odd_done = odd_done, None
            else:
                yp, kvp, _ = odd_prompt(yp, li)
            ys, kvs, vrows = odd_layer_sample(ys, cache_d[li], page_table, w_in, w_out, norm_pre_odd[li],
                                              norm_post_odd[li], c_ln_g[li], c_ln_b[li], c_ws[li], c_bs[li])
            d_p.append(kvp)
            d_s.append(kvs)
            c_s.append(vrows)
    return (yp, ys, jnp.stack(a_p[0]), jnp.stack(a_s[0]), jnp.stack(a_p[1]), jnp.stack(a_s[1]),
            jnp.stack(a_p[2]), jnp.stack(a_s[2]), jnp.stack(b_p), jnp.stack(b_s),
            jnp.stack(d_p), jnp.stack(d_s), jnp.stack(c_s))
```
